```python
import math
import jax, jax.numpy as jnp
from jax import lax
import numpy as np

D_MODEL = 2048
BATCH = 4
SEQ = 4096
DEPTH = 2

N_META = 16
BLOCK = 128
PAD = BLOCK - N_META
MIX_W = D_MODEL
GROUP_W = MIX_W // 4
SSM_W = GROUP_W
SSM_H = 16
SSM_G = SSM_W // SSM_H
SSM_P = 64
ML_HEADS = 4
ML_DH = GROUP_W // ML_HEADS
ML_CONV = 4
FX_HEADS = 4
FX_DH = GROUP_W // FX_HEADS
POOL_WINDOWS = (2, 4, 8, 16)
POOL_GW = GROUP_W // len(POOL_WINDOWS)
D_FF = 4 * D_MODEL
EPS = 1e-6
NEG = -1e30
IN_SPLIT = (SSM_W, GROUP_W, GROUP_W, GROUP_W, GROUP_W, ML_HEADS, ML_HEADS, GROUP_W, GROUP_W, GROUP_W, FX_HEADS, GROUP_W)
IN_COLS = sum(IN_SPLIT)

kernel_name = "hymba_s5_mlstm_fox_pool_hybrid"

F32 = jnp.float32


def rmsnorm(x, g):
    xf = x.astype(F32)
    y = xf * lax.rsqrt(jnp.mean(xf * xf, axis=-1, keepdims=True) + EPS)
    return (y * g.astype(F32)).astype(x.dtype)


def pad_front(a):
    return jnp.pad(a, [(0, 0), (PAD, 0)] + [(0, 0)] * (a.ndim - 2))


def causal_dwconv(x, w):
    k = w.shape[0]
    return lax.conv_general_dilated(x, w.astype(F32)[:, None, :], window_strides=(1,), padding=[(k - 1, 0)],
                                    dimension_numbers=('NWC', 'WIO', 'NWC'), feature_group_count=x.shape[-1])


def s5_mixer(u, lam_re, lam_im, log_dt, b_re, b_im, c_re, c_im, d, glu_w, glu_b):
    bsz, L, _ = u.shape
    uf = u.astype(F32).reshape(bsz, L, SSM_G, SSM_H)
    lam = lax.complex(lam_re.astype(F32), lam_im.astype(F32))
    dt = jnp.exp(log_dt.astype(F32))[:, None]
    a_bar = jnp.exp(lam * dt)
    b = lax.complex(b_re.astype(F32), b_im.astype(F32))
    b_bar = ((a_bar - 1.0) / lam)[..., None] * b
    bu = lax.complex(jnp.einsum('blgh,gph->blgp', uf, jnp.real(b_bar)),
                     jnp.einsum('blgh,gph->blgp', uf, jnp.imag(b_bar)))
    a_seq = jnp.broadcast_to(a_bar[None, None], (1, L, SSM_G, SSM_P))

    def combine(e1, e2):
        a1, s1 = e1
        a2, s2 = e2
        return a2 * a1, a2 * s1 + s2

    _, states = lax.associative_scan(combine, (a_seq, bu), axis=1)
    y = (jnp.einsum('blgp,ghp->blgh', jnp.real(states), c_re.astype(F32))
         - jnp.einsum('blgp,ghp->blgh', jnp.imag(states), c_im.astype(F32))
         + d.astype(F32) * uf)
    y = jax.nn.gelu(y.reshape(bsz, L, SSM_W))
    z = y @ glu_w.astype(F32) + glu_b.astype(F32)
    val, gate = jnp.split(z, 2, axis=-1)
    return (val * jax.nn.sigmoid(gate)).astype(u.dtype)


def mlstm_mixer(q, k, v, o_pre, i_pre, f_pre, conv_w, norm_g):
    bsz, L, _ = q.shape
    qk = jax.nn.silu(causal_dwconv(jnp.concatenate([q, k], axis=-1).astype(F32), conv_w))
    q, k = jnp.split(qk, 2, axis=-1)
    lp = L + PAD
    nc = lp // BLOCK
    valid_c = (jnp.arange(lp) >= PAD).reshape(nc, BLOCK)

    def heads(a):
        a = pad_front(a.astype(F32)).reshape(bsz, nc, BLOCK, ML_HEADS, ML_DH)
        return a.transpose(0, 3, 1, 2, 4)

    def gates(a):
        return pad_front(a).reshape(bsz, nc, BLOCK, ML_HEADS).transpose(0, 3, 1, 2)

    qh, kh, vh = heads(q), heads(k) * ML_DH ** -0.5, heads(v)
    log_i = jnp.where(valid_c, gates(i_pre), NEG)
    log_f = jnp.where(valid_c, gates(jax.nn.log_sigmoid(f_pre)), 0.0)

    b = jnp.cumsum(log_f, axis=-1)
    g = b[..., -1]
    a_state = g[..., None] - b + log_i
    m_loc = jnp.max(a_state, axis=-1)
    w_state = jnp.exp(a_state - m_loc[..., None])
    c_loc = jnp.einsum('bhcsv,bhcsk->bhcvk', vh * w_state[..., None], kh)
    n_loc = jnp.einsum('bhcs,bhcsk->bhck', w_state, kh)

    def step(carry, xs):
        c_prev, n_prev, m_prev = carry
        g_c, m_loc_c, c_loc_c, n_loc_c = xs
        m_new = jnp.maximum(g_c + m_prev, m_loc_c)
        decay = jnp.exp(g_c + m_prev - m_new)
        scale = jnp.exp(m_loc_c - m_new)
        c_new = decay[..., None, None] * c_prev + scale[..., None, None] * c_loc_c
        n_new = decay[..., None] * n_prev + scale[..., None] * n_loc_c
        return (c_new, n_new, m_new), (c_prev, n_prev, m_prev)

    init = (jnp.zeros((bsz, ML_HEADS, ML_DH, ML_DH), F32), jnp.zeros((bsz, ML_HEADS, ML_DH), F32),
            jnp.zeros((bsz, ML_HEADS), F32))
    xs = (jnp.moveaxis(g, 2, 0), jnp.moveaxis(m_loc, 2, 0), jnp.moveaxis(c_loc, 2, 0), jnp.moveaxis(n_loc, 2, 0))
    _, (c_st, n_st, m_st) = lax.scan(step, init, xs)
    c_st, n_st, m_st = jnp.moveaxis(c_st, 0, 2), jnp.moveaxis(n_st, 0, 2), jnp.moveaxis(m_st, 0, 2)

    t_idx = jnp.arange(BLOCK)
    causal = t_idx[:, None] >= t_idx[None, :]
    d_log = jnp.where(causal, b[..., :, None] - b[..., None, :] + log_i[..., None, :], NEG)
    m_inter = b + m_st[..., None]
    m_comb = jnp.maximum(m_inter, jnp.max(d_log, axis=-1))
    scores = jnp.einsum('bhctd,bhcsd->bhcts', qh, kh) * jnp.exp(d_log - m_comb[..., None])
    inter_w = jnp.exp(m_inter - m_comb)
    num = (jnp.einsum('bhcts,bhcsv->bhctv', scores, vh)
           + inter_w[..., None] * jnp.einsum('bhctk,bhcvk->bhctv', qh, c_st))
    den = jnp.sum(scores, axis=-1) + inter_w * jnp.einsum('bhctk,bhck->bhct', qh, n_st)
    h = num / jnp.maximum(jnp.abs(den), jnp.exp(-m_comb))[..., None]
    h = h.transpose(0, 2, 3, 1, 4).reshape(bsz, lp, ML_HEADS, ML_DH)[:, PAD:]
    h = h * lax.rsqrt(jnp.mean(h * h, axis=-1, keepdims=True) + EPS) * norm_g.astype(F32).reshape(ML_HEADS, ML_DH)
    h = h.reshape(bsz, L, GROUP_W) * jax.nn.sigmoid(o_pre.astype(F32))
    return h.astype(o_pre.dtype)


def fox_mixer(q, k, v, f_pre):
    bsz, L, _ = q.shape
    lp = L + PAD
    nb = lp // BLOCK

    def heads(a):
        return pad_front(a.astype(F32)).reshape(bsz, lp, FX_HEADS, FX_DH).transpose(0, 2, 1, 3)

    qh, kh, vh = heads(q) * FX_DH ** -0.5, heads(k), heads(v)
    pos = jnp.arange(lp)
    log_f = jnp.where(pos >= PAD, pad_front(jax.nn.log_sigmoid(f_pre)).transpose(0, 2, 1), 0.0)
    cum = jnp.cumsum(log_f, axis=-1)
    q_blocks = qh.reshape(bsz, FX_HEADS, nb, BLOCK, FX_DH).transpose(2, 0, 1, 3, 4)
    cum_blocks = cum.reshape(bsz, FX_HEADS, nb, BLOCK).transpose(2, 0, 1, 3)
    starts = jnp.arange(nb) * BLOCK

    def attend(args):
        q_b, cum_b, start = args
        q_pos = start + jnp.arange(BLOCK)
        mask = (pos[None, :] <= q_pos[:, None]) & (pos[None, :] >= PAD)
        logits = jnp.einsum('bhtd,bhsd->bhts', q_b, kh) + cum_b[..., None] - cum[..., None, :]
        p = jax.nn.softmax(jnp.where(mask, logits, NEG), axis=-1)
        return jnp.einsum('bhts,bhsd->bhtd', p, vh)

    out = lax.map(attend, (q_blocks, cum_blocks, starts))
    out = out.transpose(1, 0, 3, 2, 4).reshape(bsz, lp, GROUP_W)[:, PAD:]
    return out.astype(q.dtype)


def pool_mixer(u, pool_w, pool_scale):
    bsz, L, _ = u.shape
    uf = u.astype(F32)
    t = jnp.arange(1, L + 1).astype(F32)
    outs = []
    for grp, w in zip(jnp.split(uf, len(POOL_WINDOWS), axis=-1), POOL_WINDOWS):
        cs = jnp.pad(jnp.cumsum(grp, axis=1), ((0, 0), (w, 0), (0, 0)))
        win_mean = (cs[:, w:] - cs[:, :L]) / jnp.minimum(t, float(w))[None, :, None]
        outs.append(win_mean - grp)
    pooled = jnp.stack(outs, axis=2)
    mixed = jnp.einsum('blgc,gcd->blgd', pooled, pool_w.astype(F32)).reshape(bsz, L, GROUP_W)
    return (mixed * pool_scale.astype(F32)).astype(u.dtype)


def setup_inputs(seed: int = 0) -> dict:
    key = jax.random.key(seed)
    ks = iter(jax.random.split(key, 32))

    def nrm(shape, scale):
        return scale * jax.random.normal(next(ks), shape, F32)

    def gain(shape):
        return 1.0 + nrm(shape, 0.02)

    gate_lin = jnp.linspace(3.0, 6.0, ML_HEADS, dtype=F32)[None]
    log_dt = jax.random.uniform(next(ks), (DEPTH, SSM_G), F32, math.log(1e-3), math.log(1e-1))
    return {
        "x": nrm((BATCH, SEQ, D_MODEL), 1.0),
        "meta_tokens": nrm((N_META, D_MODEL), 1.0),
        "g_pre_mix": gain((DEPTH, D_MODEL)),
        "g_post_mix": gain((DEPTH, D_MODEL)),
        "g_pre_ffn": gain((DEPTH, D_MODEL)),
        "g_post_ffn": gain((DEPTH, D_MODEL)),
        "w_in": nrm((DEPTH, D_MODEL, IN_COLS), D_MODEL ** -0.5),
        "ml_gate_bias": jnp.concatenate([nrm((DEPTH, ML_HEADS), 0.1), gate_lin + nrm((DEPTH, ML_HEADS), 0.1)], axis=-1),
        "fx_gate_bias": jnp.linspace(3.0, 6.0, FX_HEADS, dtype=F32)[None] + nrm((DEPTH, FX_HEADS), 0.1),
        "ssm_lam_re": -0.5 + nrm((DEPTH, SSM_G, SSM_P), 0.01),
        "ssm_lam_im": math.pi * jnp.arange(SSM_P, dtype=F32)[None, None] + nrm((DEPTH, SSM_G, SSM_P), 0.01),
        "ssm_log_dt": log_dt,
        "ssm_b_re": nrm((DEPTH, SSM_G, SSM_P, SSM_H), (2 * SSM_H) ** -0.5),
        "ssm_b_im": nrm((DEPTH, SSM_G, SSM_P, SSM_H), (2 * SSM_H) ** -0.5),
        "ssm_c_re": nrm((DEPTH, SSM_G, SSM_H, SSM_P), (2 * SSM_P) ** -0.5),
        "ssm_c_im": nrm((DEPTH, SSM_G, SSM_H, SSM_P), (2 * SSM_P) ** -0.5),
        "ssm_d": nrm((DEPTH, SSM_G, SSM_H), 1.0),
        "ssm_glu_w": nrm((DEPTH, SSM_W, 2 * SSM_W), SSM_W ** -0.5),
        "ssm_glu_b": nrm((DEPTH, 2 * SSM_W), 0.01),
        "ml_conv_w": nrm((DEPTH, ML_CONV, 2 * GROUP_W), ML_CONV ** -0.5),
        "ml_norm_g": gain((DEPTH, GROUP_W)),
        "pool_w": nrm((DEPTH, len(POOL_WINDOWS), POOL_GW, POOL_GW), POOL_GW ** -0.5),
        "pool_scale": gain((DEPTH, GROUP_W)),
        "w_out": nrm((DEPTH, MIX_W, D_MODEL), MIX_W ** -0.5),
        "mlp_w1": nrm((DEPTH, D_MODEL, D_FF), D_MODEL ** -0.5),
        "mlp_w2": nrm((DEPTH, D_FF, D_MODEL), D_FF ** -0.5),
    }


def reference(x, meta_tokens, g_pre_mix, g_post_mix, g_pre_ffn, g_post_ffn, w_in, ml_gate_bias, fx_gate_bias,
              ssm_lam_re, ssm_lam_im, ssm_log_dt, ssm_b_re, ssm_b_im, ssm_c_re, ssm_c_im, ssm_d, ssm_glu_w,
              ssm_glu_b, ml_conv_w, ml_norm_g, pool_w, pool_scale, w_out, mlp_w1, mlp_w2):
    bsz = x.shape[0]
    meta = jnp.broadcast_to(meta_tokens[None], (bsz, N_META, D_MODEL)).astype(x.dtype)
    h = jnp.concatenate([meta, x], axis=1)
    split_idx = np.cumsum(IN_SPLIT)[:-1].tolist()
    for l in range(DEPTH):
        xn = rmsnorm(h, g_pre_mix[l])
        proj = xn @ w_in[l]
        (s_u, m_q, m_k, m_v, m_o, m_i, m_f, f_q, f_k, f_v, f_f, p_u) = jnp.split(proj, split_idx, axis=-1)
        gb = ml_gate_bias[l].astype(F32)
        y_ssm = s5_mixer(s_u, ssm_lam_re[l], ssm_lam_im[l], ssm_log_dt[l], ssm_b_re[l], ssm_b_im[l],
                         ssm_c_re[l], ssm_c_im[l], ssm_d[l], ssm_glu_w[l], ssm_glu_b[l])
        y_ml = mlstm_mixer(m_q, m_k, m_v, m_o, m_i.astype(F32) + gb[:ML_HEADS], m_f.astype(F32) + gb[ML_HEADS:],
                           ml_conv_w[l], ml_norm_g[l])
        y_fx = fox_mixer(f_q, f_k, f_v, f_f.astype(F32) + fx_gate_bias[l].astype(F32))
        y_pool = pool_mixer(p_u, pool_w[l], pool_scale[l])
        mix = jnp.concatenate([y_ssm, y_ml, y_fx, y_pool], axis=-1) @ w_out[l]
        h = h + rmsnorm(mix, g_post_mix[l])
        hn = rmsnorm(h, g_pre_ffn[l])
        ff = jnp.square(jax.nn.relu(hn @ mlp_w1[l])) @ mlp_w2[l]
        h = h + rmsnorm(ff, g_post_ffn[l])
    return h[:, N_META:]
```

```python
import functools
import math

import jax
import jax.numpy as jnp
from jax import lax
from jax.experimental import pallas as pl
from jax.experimental.pallas import tpu as pltpu

F32 = jnp.float32
BF16 = jnp.bfloat16

D_MODEL = 2048
N_META = 16
BLOCK = 128
PAD = BLOCK - N_META
GROUP_W = 512
SSM_H = 16
SSM_G = GROUP_W // SSM_H
SSM_P = 64
SSM_N = SSM_G * SSM_P
HEADS = 4
DH = GROUP_W // HEADS
ML_CONV = 4
POOL_WINDOWS = (2, 4, 8, 16)
POOL_MAXW = 16
D_FF = 4 * D_MODEL
EPS = 1e-6
NEG = -1e30
N_MAIN = 9 * GROUP_W
LANES = 128
SUBLANES = 8
VMEM_LIMIT = 56 * 1024 * 1024

COL_SU, COL_MQ, COL_MK, COL_MV, COL_MO, COL_FQ, COL_FK, COL_FV, COL_PU = range(9)
GC_MI, GC_MF, GC_FF = 0, HEADS, 2 * HEADS


def _pick(n, target, mult):
    best = None
    for d in range(mult, min(n, target) + 1, mult):
        if n % d == 0:
            best = d
    assert best is not None, (n, target, mult)
    return best


def _params(sem):
    return pltpu.CompilerParams(dimension_semantics=sem, vmem_limit_bytes=VMEM_LIMIT)


def _rms(x, g):
    return x * lax.rsqrt(jnp.mean(x * x, axis=-1, keepdims=True) + EPS) * g


def _inproj_kernel(x_ref, g_ref, w_ref, wg_ref, cs_ref, o_ref, og_ref, xn_ref):
    @pl.when(pl.program_id(1) == 0)
    def _():
        xn = _rms(x_ref[...], g_ref[...]).astype(BF16)
        xn_ref[...] = xn
        og_ref[...] = jnp.dot(xn, wg_ref[...], preferred_element_type=F32)

    acc = jnp.dot(xn_ref[...], w_ref[...], preferred_element_type=F32)
    o_ref[...] = (acc * cs_ref[...]).astype(BF16)


def _inproj(h2, g, w_main, w_gate, colscale):
    m = h2.shape[0]
    tm = _pick(m, 1056, 16)
    tn = 1536
    return pl.pallas_call(
        _inproj_kernel,
        grid=(m // tm, N_MAIN // tn),
        in_specs=[
            pl.BlockSpec((tm, D_MODEL), lambda i, j: (i, 0)),
            pl.BlockSpec((1, D_MODEL), lambda i, j: (0, 0)),
            pl.BlockSpec((D_MODEL, tn), lambda i, j: (0, j)),
            pl.BlockSpec((D_MODEL, LANES), lambda i, j: (0, 0)),
            pl.BlockSpec((1, tn), lambda i, j: (0, j)),
        ],
        out_specs=[
            pl.BlockSpec((tm, tn), lambda i, j: (i, j)),
            pl.BlockSpec((tm, LANES), lambda i, j: (i, 0)),
        ],
        out_shape=[jax.ShapeDtypeStruct((m, N_MAIN), BF16), jax.ShapeDtypeStruct((m, LANES), F32)],
        scratch_shapes=[pltpu.VMEM((tm, D_MODEL), BF16)],
        compiler_params=_params(("arbitrary", "arbitrary")),
        name="inproj",
    )(h2, g, w_main, w_gate, colscale)


def _gates_kernel(raw_ref, bias_ref, col_ref, row_ref, fxrow_ref, carry_ref):
    c = pl.program_id(1)

    @pl.when(c == 0)
    def _():
        carry_ref[...] = jnp.zeros_like(carry_ref)

    g = raw_ref[...] + bias_ref[...]
    r_io = lax.broadcasted_iota(jnp.int32, (BLOCK, LANES), 0)
    c_io = lax.broadcasted_iota(jnp.int32, (BLOCK, LANES), 1)
    valid = (r_io + c * BLOCK) >= PAD
    logsig = jnp.minimum(g, 0.0) - jnp.log(1.0 + jnp.exp(-jnp.abs(g)))
    x = jnp.where(valid, logsig, 0.0)
    x = jnp.where(c_io >= GC_MF, jnp.where(c_io < GC_FF + HEADS, x, 0.0), 0.0)
    tri = (r_io >= c_io).astype(F32)
    cum = jnp.dot(tri, x, precision=lax.Precision.HIGHEST, preferred_element_type=F32)
    cum = cum + jnp.where(c_io >= GC_FF, carry_ref[...], 0.0)
    carry_ref[...] = cum[BLOCK - 1:BLOCK, :]
    log_i = jnp.where(valid, g, NEG)
    out = jnp.where(c_io < GC_MF, log_i, cum)
    col_ref[...] = out
    out_t = out.T
    row_ref[0, 0] = out_t[0:2 * SUBLANES, :]
    fxrow_ref[0] = out_t[GC_FF:GC_FF + SUBLANES, :]


def _gates(raw, bias, bsz, lp):
    nc = lp // BLOCK
    return pl.pallas_call(
        _gates_kernel,
        grid=(bsz, nc),
        in_specs=[
            pl.BlockSpec((BLOCK, LANES), lambda b, c: (b * nc + c, 0)),
            pl.BlockSpec((1, LANES), lambda b, c: (0, 0)),
        ],
        out_specs=[
            pl.BlockSpec((BLOCK, LANES), lambda b, c: (b * nc + c, 0)),
            pl.BlockSpec((1, 1, 2 * SUBLANES, BLOCK), lambda b, c: (b, c, 0, 0)),
            pl.BlockSpec((1, SUBLANES, BLOCK), lambda b, c: (b, 0, c)),
        ],
        out_shape=[
            jax.ShapeDtypeStruct((bsz * lp, LANES), F32),
            jax.ShapeDtypeStruct((bsz, nc, 2 * SUBLANES, BLOCK), F32),
            jax.ShapeDtypeStruct((bsz, SUBLANES, lp), F32),
        ],
        scratch_shapes=[pltpu.VMEM((1, LANES), F32)],
        compiler_params=_params(("arbitrary", "arbitrary")),
        name="gates",
    )(raw, bias)


def _s5_prep_kernel(lre_ref, lim_ref, ldt_ref, bre_ref, bim_ref, consts_ref, bbre_ref, bbim_ref):
    lre = lre_ref[...]
    lim = lim_ref[...]
    dt = jnp.exp(ldt_ref[...])
    r_io = lax.broadcasted_iota(jnp.int32, (SUBLANES, SSM_N), 0)
    kk = (r_io + 1).astype(F32)
    mag = jnp.exp(kk * (lre * dt))
    ang = kk * (lim * dt)
    pw_re = mag * jnp.cos(ang)
    pw_im = mag * jnp.sin(ang)
    for i, d in enumerate((1, 2, 4)):
        keep = r_io >= d
        consts_ref[16 * i:16 * i + 8, :] = jnp.where(keep, pw_re[d - 1:d, :], 0.0)
        consts_ref[16 * i + 8:16 * i + 16, :] = jnp.where(keep, pw_im[d - 1:d, :], 0.0)
    consts_ref[48:56, :] = pw_re
    consts_ref[56:64, :] = pw_im
    nr = pw_re[0:1, :] - 1.0
    ni = pw_im[0:1, :]
    den = lre * lre + lim * lim
    cr = (nr * lre + ni * lim) / den
    ci = (ni * lre - nr * lim) / den
    bre = bre_ref[...]
    bim = bim_ref[...]
    bbre_ref[...] = cr * bre - ci * bim
    bbim_ref[...] = cr * bim + ci * bre


def _s5_prep(lam_re, lam_im, log_dt, b_re, b_im):
    flat = lambda a: a.reshape(1, SSM_N)
    ldt = jnp.repeat(log_dt, SSM_P).reshape(1, SSM_N)
    bt = lambda a: a.transpose(2, 0, 1).reshape(SSM_H, SSM_N)
    return pl.pallas_call(
        _s5_prep_kernel,
        out_shape=[
            jax.ShapeDtypeStruct((64, SSM_N), F32),
            jax.ShapeDtypeStruct((SSM_H, SSM_N), F32),
            jax.ShapeDtypeStruct((SSM_H, SSM_N), F32),
        ],
        name="s5_prep",
    )(flat(lam_re), flat(lam_im), ldt, bt(b_re), bt(b_im))


S5_STRIPS = SSM_N // LANES
S5_STRIPS_PER_LOOP = 4


def _s5_kernel(u_ref, bmat_ref, cmat_ref, d_ref, consts_ref, gw_ref, gb_ref, y_ref, x_ref, carry_ref, *, tb):
    @pl.when(pl.program_id(1) == 0)
    def _():
        carry_ref[...] = jnp.zeros_like(carry_ref)

    u = u_ref[...]
    x_ref[...] = jnp.dot(u, bmat_ref[...], preferred_element_type=F32)

    def cmul_add(xr, xi, ar, ai, sr, si):
        return xr + ar * sr - ai * si, xi + ar * si + ai * sr

    for s0 in range(0, S5_STRIPS, S5_STRIPS_PER_LOOP):
        strips = range(s0, s0 + S5_STRIPS_PER_LOOP)

        def body(r, carry, strips=strips):
            rows = pl.ds(pl.multiple_of(r * SUBLANES, SUBLANES), SUBLANES)
            out = []
            for n, s in enumerate(strips):
                lr = slice(s * LANES, (s + 1) * LANES)
                li = slice(SSM_N + s * LANES, SSM_N + (s + 1) * LANES)
                xr = x_ref[rows, lr]
                xi = x_ref[rows, li]
                for i, d in enumerate((1, 2, 4)):
                    ar = consts_ref[16 * i:16 * i + 8, lr]
                    ai = consts_ref[16 * i + 8:16 * i + 16, lr]
                    xr, xi = cmul_add(xr, xi, ar, ai, pltpu.roll(xr, d, axis=0), pltpu.roll(xi, d, axis=0))
                xr, xi = cmul_add(xr, xi, consts_ref[48:56, lr], consts_ref[56:64, lr],
                                  carry[2 * n], carry[2 * n + 1])
                x_ref[rows, lr] = xr
                x_ref[rows, li] = xi
                out.append(jnp.broadcast_to(xr[SUBLANES - 1:SUBLANES, :], (SUBLANES, LANES)))
                out.append(jnp.broadcast_to(xi[SUBLANES - 1:SUBLANES, :], (SUBLANES, LANES)))
            return tuple(out)

        init = []
        for s in strips:
            init.append(carry_ref[:, s * LANES:(s + 1) * LANES])
            init.append(carry_ref[:, SSM_N + s * LANES:SSM_N + (s + 1) * LANES])
        fin = lax.fori_loop(0, tb // SUBLANES, body, tuple(init))
        for n, s in enumerate(strips):
            carry_ref[:, s * LANES:(s + 1) * LANES] = fin[2 * n]
            carry_ref[:, SSM_N + s * LANES:SSM_N + (s + 1) * LANES] = fin[2 * n + 1]

    y = jnp.dot(x_ref[...].astype(BF16), cmat_ref[...], preferred_element_type=F32)
    y = jax.nn.gelu(y + d_ref[...] * u.astype(F32), approximate=True)
    z = jnp.dot(y.astype(BF16), gw_ref[...], preferred_element_type=F32) + gb_ref[...]
    y_ref[...] = (z[:, :GROUP_W] * jax.nn.sigmoid(z[:, GROUP_W:])).astype(BF16)


def _s5(proj, bmat, cmat, dvec, consts, glu_w, glu_b, bsz, lp):
    tb = _pick(lp, 528, 16)
    nt = lp // tb
    const = lambda shape: pl.BlockSpec(shape, lambda b, t: (0,) * len(shape))
    return pl.pallas_call(
        functools.partial(_s5_kernel, tb=tb),
        grid=(bsz, nt),
        in_specs=[
            pl.BlockSpec((tb, GROUP_W), lambda b, t: (b * nt + t, COL_SU)),
            const((GROUP_W, 2 * SSM_N)),
            const((2 * SSM_N, GROUP_W)),
            const((1, GROUP_W)),
            const((64, SSM_N)),
            const((GROUP_W, 2 * GROUP_W)),
            const((1, 2 * GROUP_W)),
        ],
        out_specs=pl.BlockSpec((tb, GROUP_W), lambda b, t: (b * nt + t, 0)),
        out_shape=jax.ShapeDtypeStruct((bsz * lp, GROUP_W), BF16),
        scratch_shapes=[pltpu.VMEM((tb, 2 * SSM_N), F32), pltpu.VMEM((SUBLANES, 2 * SSM_N), F32)],
        compiler_params=_params(("arbitrary", "arbitrary")),
        name="s5",
    )(proj, bmat, cmat, dvec, consts, glu_w, glu_b)


def _mlstm_kernel(q_ref, k_ref, v_ref, o_ref, gcol_ref, grow_ref, cw_ref, ng_ref, y_ref,
                  buf_ref, c_ref, n_ref, m_ref):
    @pl.when(pl.program_id(1) == 0)
    def _():
        buf_ref[0:SUBLANES, :] = jnp.zeros((SUBLANES, 2 * GROUP_W), F32)
        c_ref[...] = jnp.zeros_like(c_ref)
        n_ref[...] = jnp.zeros_like(n_ref)
        m_ref[...] = jnp.zeros_like(m_ref)

    @pl.when(pl.program_id(1) > 0)
    def _():
        buf_ref[0:SUBLANES, :] = buf_ref[BLOCK:BLOCK + SUBLANES, :]

    buf_ref[SUBLANES:SUBLANES + BLOCK, 0:GROUP_W] = q_ref[...].astype(F32)
    buf_ref[SUBLANES:SUBLANES + BLOCK, GROUP_W:] = k_ref[...].astype(F32)
    conv = cw_ref[ML_CONV - 1:ML_CONV, :] * buf_ref[SUBLANES:SUBLANES + BLOCK, :]
    for j in range(1, ML_CONV):
        conv = conv + cw_ref[ML_CONV - 1 - j:ML_CONV - j, :] * buf_ref[SUBLANES - j:SUBLANES - j + BLOCK, :]
    qk = conv * jax.nn.sigmoid(conv)

    t_io = lax.broadcasted_iota(jnp.int32, (BLOCK, BLOCK), 0)
    s_io = lax.broadcasted_iota(jnp.int32, (BLOCK, BLOCK), 1)
    causal = t_io >= s_io
    gcol = gcol_ref[...]
    grow = grow_ref[0, 0]
    nt_dims = (((1,), (1,)), ((), ()))
    tn_dims = (((0,), (0,)), ((), ()))

    for h in range(HEADS):
        cols = slice(h * DH, (h + 1) * DH)
        qh = qk[:, cols].astype(BF16)
        kh_f = qk[:, GROUP_W + h * DH:GROUP_W + (h + 1) * DH] * (DH ** -0.5)
        kh = kh_f.astype(BF16)
        vh = v_ref[:, cols]
        li_col = gcol[:, GC_MI + h:GC_MI + h + 1]
        b_col = gcol[:, GC_MF + h:GC_MF + h + 1]
        li_row = grow[GC_MI + h:GC_MI + h + 1, :]
        b_row = grow[GC_MF + h:GC_MF + h + 1, :]
        g = b_row[:, BLOCK - 1:BLOCK]
        c_prev = c_ref[h]
        n_prev = n_ref[h]
        m_prev = m_ref[h][:, 0:1]

        d_log = jnp.where(causal, b_col - b_row + li_row, NEG)
        m_inter = b_col + m_prev
        m_comb = jnp.maximum(m_inter, jnp.max(d_log, axis=-1, keepdims=True))
        qk_s = lax.dot_general(qh, kh, nt_dims, preferred_element_type=F32)
        scores = qk_s * jnp.exp(d_log - m_comb)
        inter_w = jnp.exp(m_inter - m_comb)
        q_c = lax.dot_general(qh, c_prev.astype(BF16), nt_dims, preferred_element_type=F32)
        num = jnp.dot(scores.astype(BF16), vh, preferred_element_type=F32) + inter_w * q_c
        q_n = jnp.sum(qk[:, cols] * n_prev, axis=-1, keepdims=True)
        den = jnp.sum(scores, axis=-1, keepdims=True) + inter_w * q_n
        hh = num / jnp.maximum(jnp.abs(den), jnp.exp(-m_comb))
        hh = hh * lax.rsqrt(jnp.mean(hh * hh, axis=-1, keepdims=True) + EPS) * ng_ref[:, cols]
        y_ref[:, cols] = (hh * jax.nn.sigmoid(o_ref[:, cols].astype(F32))).astype(BF16)

        a_col = g - b_col + li_col
        m_loc = jnp.max(g - b_row + li_row, axis=-1, keepdims=True)
        w_col = jnp.exp(a_col - m_loc)
        vw = (vh.astype(F32) * w_col).astype(BF16)
        c_loc = lax.dot_general(vw, kh, tn_dims, preferred_element_type=F32)
        n_loc = jnp.sum(w_col * kh_f, axis=0, keepdims=True)
        m_new = jnp.maximum(g + m_prev, m_loc)
        decay = jnp.exp(g + m_prev - m_new)
        scale = jnp.exp(m_loc - m_new)
        c_ref[h] = decay * c_prev + scale * c_loc
        n_ref[h] = decay * n_prev + scale * n_loc
        m_ref[h] = jnp.broadcast_to(m_new, (1, LANES))


def _mlstm(proj, gcol, grow, conv_w, norm_g, bsz, lp):
    nc = lp // BLOCK
    blk = lambda col: pl.BlockSpec((BLOCK, GROUP_W), lambda b, c: (b * nc + c, col))
    return pl.pallas_call(
        _mlstm_kernel,
        grid=(bsz, nc),
        in_specs=[
            blk(COL_MQ), blk(COL_MK), blk(COL_MV), blk(COL_MO),
            pl.BlockSpec((BLOCK, LANES), lambda b, c: (b * nc + c, 0)),
            pl.BlockSpec((1, 1, 2 * SUBLANES, BLOCK), lambda b, c: (b, c, 0, 0)),
            pl.BlockSpec((ML_CONV, 2 * GROUP_W), lambda b, c: (0, 0)),
            pl.BlockSpec((1, GROUP_W), lambda b, c: (0, 0)),
        ],
        out_specs=pl.BlockSpec((BLOCK, GROUP_W), lambda b, c: (b * nc + c, 0)),
        out_shape=jax.ShapeDtypeStruct((bsz * lp, GROUP_W), BF16),
        scratch_shapes=[
            pltpu.VMEM((SUBLANES + BLOCK, 2 * GROUP_W), F32),
            pltpu.VMEM((HEADS, DH, DH), F32),
            pltpu.VMEM((HEADS, 1, DH), F32),
            pltpu.VMEM((HEADS, 1, LANES), F32),
        ],
        compiler_params=_params(("arbitrary", "arbitrary")),
        name="mlstm",
    )(proj, proj, proj, proj, gcol, grow, conv_w, norm_g)


def _fox_kernel(q_ref, k_ref, v_ref, ccol_ref, crow_ref, y_ref, *, tq, tk):
    qi = pl.program_id(1)
    q0 = qi * tq
    n_kb = (q0 + tq + tk - 1) // tk
    t_pos = q0 + lax.broadcasted_iota(jnp.int32, (tq, tk), 0)
    s_io = lax.broadcasted_iota(jnp.int32, (tq, tk), 1)
    nt_dims = (((1,), (1,)), ((), ()))

    for h in range(HEADS):
        cols = slice(h * DH, (h + 1) * DH)
        q = q_ref[:, cols]
        ccol = ccol_ref[:, GC_FF + h:GC_FF + h + 1]

        def body(kb, carry, h=h, cols=cols, q=q, ccol=ccol):
            m, l, acc = carry
            k0 = pl.multiple_of(kb * tk, tk)
            k = k_ref[pl.ds(k0, tk), cols]
            v = v_ref[pl.ds(k0, tk), cols]
            crow = crow_ref[0, h:h + 1, pl.ds(k0, tk)]
            s = lax.dot_general(q, k, nt_dims, preferred_element_type=F32) + (ccol - crow)
            s_pos = s_io + k0
            s = jnp.where(s_pos <= t_pos, jnp.where(s_pos >= PAD, s, NEG), NEG)
            m_new = jnp.maximum(m, jnp.max(s, axis=-1, keepdims=True))
            alpha = jnp.exp(m - m_new)
            p = jnp.exp(s - m_new)
            l = alpha * l + jnp.sum(p, axis=-1, keepdims=True)
            acc = alpha * acc + jnp.dot(p.astype(BF16), v, preferred_element_type=F32)
            return m_new, l, acc

        init = (jnp.full((tq, 1), NEG, F32), jnp.zeros((tq, 1), F32), jnp.zeros((tq, DH), F32))
        _, l, acc = lax.fori_loop(0, n_kb, body, init)
        y_ref[:, cols] = (acc / l).astype(BF16)


def _fox(proj, gcol, fxrow, bsz, lp):
    tq = _pick(lp, 384, BLOCK)
    tk = tq
    nq = lp // tq
    proj3 = proj.reshape(bsz, lp, N_MAIN)
    return pl.pallas_call(
        functools.partial(_fox_kernel, tq=tq, tk=tk),
        grid=(bsz, nq),
        in_specs=[
            pl.BlockSpec((None, tq, GROUP_W), lambda b, i: (b, i, COL_FQ)),
            pl.BlockSpec((None, lp, GROUP_W), lambda b, i: (b, 0, COL_FK)),
            pl.BlockSpec((None, lp, GROUP_W), lambda b, i: (b, 0, COL_FV)),
            pl.BlockSpec((tq, LANES), lambda b, i: (b * nq + i, 0)),
            pl.BlockSpec((1, SUBLANES, lp), lambda b, i: (b, 0, 0)),
        ],
        out_specs=pl.BlockSpec((tq, GROUP_W), lambda b, i: (b * nq + i, 0)),
        out_shape=jax.ShapeDtypeStruct((bsz * lp, GROUP_W), BF16),
        compiler_params=_params(("arbitrary", "arbitrary")),
        name="fox",
    )(proj3, proj3, proj3, gcol, fxrow)


def _pool_kernel(u_ref, w_ref, sc_ref, y_ref, buf_ref, *, tb):
    t = pl.program_id(1)

    @pl.when(t == 0)
    def _():
        buf_ref[0:POOL_MAXW, :] = jnp.zeros((POOL_MAXW, GROUP_W), F32)

    @pl.when(t > 0)
    def _():
        buf_ref[0:POOL_MAXW, :] = buf_ref[tb:tb + POOL_MAXW, :]

    buf_ref[POOL_MAXW:POOL_MAXW + tb, :] = u_ref[...].astype(F32)
    pos = t * tb + lax.broadcasted_iota(jnp.int32, (tb, 1), 0) - (PAD - 1)
    posf = jnp.maximum(pos, 1).astype(F32)
    gw = GROUP_W // len(POOL_WINDOWS)
    for gi, w in enumerate(POOL_WINDOWS):
        cols = slice(gi * gw, (gi + 1) * gw)
        x = buf_ref[POOL_MAXW:POOL_MAXW + tb, cols]
        s = x
        for j in range(1, w):
            s = s + buf_ref[POOL_MAXW - j:POOL_MAXW - j + tb, cols]
        pooled = s / jnp.minimum(posf, float(w)) - x
        mixed = jnp.dot(pooled.astype(BF16), w_ref[gi], preferred_element_type=F32)
        y_ref[:, cols] = (mixed * sc_ref[:, cols]).astype(BF16)


def _pool(proj, pool_w, pool_scale, bsz, lp):
    tb = _pick(lp, 528, 16)
    nt = lp // tb
    gw = GROUP_W // len(POOL_WINDOWS)
    return pl.pallas_call(
        functools.partial(_pool_kernel, tb=tb),
        grid=(bsz, nt),
        in_specs=[
            pl.BlockSpec((tb, GROUP_W), lambda b, t: (b * nt + t, COL_PU)),
            pl.BlockSpec((len(POOL_WINDOWS), gw, gw), lambda b, t: (0, 0, 0)),
            pl.BlockSpec((1, GROUP_W), lambda b, t: (0, 0)),
        ],
        out_specs=pl.BlockSpec((tb, GROUP_W), lambda b, t: (b * nt + t, 0)),
        out_shape=jax.ShapeDtypeStruct((bsz * lp, GROUP_W), BF16),
        scratch_shapes=[pltpu.VMEM((POOL_MAXW + tb, GROUP_W), F32)],
        compiler_params=_params(("arbitrary", "arbitrary")),
        name="pool",
    )(proj, pool_w, pool_scale)


def _outproj_kernel(ys_ref, ym_ref, yf_ref, yp_ref, w_ref, g_ref, h_ref, o_ref, *, tm, blocks_per_seq):
    mix = jnp.dot(ys_ref[...], w_ref[0:GROUP_W, :], preferred_element_type=F32)
    for n, y_ref in enumerate((ym_ref, yf_ref, yp_ref), start=1):
        mix = mix + jnp.dot(y_ref[...], w_ref[n * GROUP_W:(n + 1) * GROUP_W, :], preferred_element_type=F32)
    off = (pl.program_id(0) % blocks_per_seq) * tm
    valid = (off + lax.broadcasted_iota(jnp.int32, (tm, 1), 0)) >= PAD
    o_ref[...] = jnp.where(valid, h_ref[...] + _rms(mix, g_ref[...]), 0.0)


def _outproj(ys, ym, yf, yp, w_out, g, h2, lp):
    m = h2.shape[0]
    tm = _pick(lp, 528, 16)
    yspec = pl.BlockSpec((tm, GROUP_W), lambda i: (i, 0))
    return pl.pallas_call(
        functools.partial(_outproj_kernel, tm=tm, blocks_per_seq=lp // tm),
        grid=(m // tm,),
        in_specs=[
            yspec, yspec, yspec, yspec,
            pl.BlockSpec((D_MODEL, D_MODEL), lambda i: (0, 0)),
            pl.BlockSpec((1, D_MODEL), lambda i: (0, 0)),
            pl.BlockSpec((tm, D_MODEL), lambda i: (i, 0)),
        ],
        out_specs=pl.BlockSpec((tm, D_MODEL), lambda i: (i, 0)),
        out_shape=jax.ShapeDtypeStruct((m, D_MODEL), F32),
        input_output_aliases={6: 0},
        compiler_params=_params(("arbitrary",)),
        name="outproj",
    )(ys, ym, yf, yp, w_out, g, h2)


def _ffn_kernel(h_ref, g1_ref, w1_ref, w2_ref, g2_ref, o_ref, hn_ref, acc_ref):
    f = pl.program_id(1)

    @pl.when(f == 0)
    def _():
        hn_ref[...] = _rms(h_ref[...], g1_ref[...]).astype(BF16)
        acc_ref[...] = jnp.zeros_like(acc_ref)

    a = jnp.maximum(jnp.dot(hn_ref[...], w1_ref[...], preferred_element_type=F32), 0.0)
    acc_ref[...] += jnp.dot((a * a).astype(BF16), w2_ref[...], preferred_element_type=F32)

    @pl.when(f == pl.num_programs(1) - 1)
    def _():
        o_ref[...] = h_ref[...] + _rms(acc_ref[...], g2_ref[...])


def _ffn(h2, g1, w1, w2, g2):
    m = h2.shape[0]
    tm = _pick(m, 768, 16)
    tf = 512
    return pl.pallas_call(
        _ffn_kernel,
        grid=(m // tm, D_FF // tf),
        in_specs=[
            pl.BlockSpec((tm, D_MODEL), lambda i, f: (i, 0)),
            pl.BlockSpec((1, D_MODEL), lambda i, f: (0, 0)),
            pl.BlockSpec((D_MODEL, tf), lambda i, f: (0, f)),
            pl.BlockSpec((tf, D_MODEL), lambda i, f: (f, 0)),
            pl.BlockSpec((1, D_MODEL), lambda i, f: (0, 0)),
        ],
        out_specs=pl.BlockSpec((tm, D_MODEL), lambda i, f: (i, 0)),
        out_shape=jax.ShapeDtypeStruct((m, D_MODEL), F32),
        scratch_shapes=[pltpu.VMEM((tm, D_MODEL), BF16), pltpu.VMEM((tm, D_MODEL), F32)],
        input_output_aliases={0: 0},
        compiler_params=_params(("arbitrary", "arbitrary")),
        name="ffn",
    )(h2, g1, w1, w2, g2)


def _split_w_in(w):
    o_mi = 5 * GROUP_W
    o_fq = o_mi + 2 * HEADS
    o_ff = o_fq + 3 * GROUP_W
    o_pu = o_ff + HEADS
    main = jnp.concatenate([w[:, :o_mi], w[:, o_fq:o_ff], w[:, o_pu:]], axis=1)
    gate = jnp.concatenate([w[:, o_mi:o_fq], w[:, o_ff:o_pu]], axis=1)
    gate = jnp.pad(gate, ((0, 0), (0, LANES - 3 * HEADS)))
    scale = jnp.ones((1, N_MAIN), F32).at[:, COL_FQ * GROUP_W:(COL_FQ + 1) * GROUP_W].set(DH ** -0.5)
    return main.astype(BF16), gate.astype(BF16), scale


def _s5_matrices(bbre, bbim, c_re, c_im):
    grp_rows = jnp.arange(GROUP_W) // SSM_H
    grp_cols = jnp.arange(SSM_N) // SSM_P
    mask = (grp_rows[:, None] == grp_cols[None, :]).astype(F32)
    expand = lambda bb: jnp.tile(bb, (SSM_G, 1)) * mask
    bmat = jnp.concatenate([expand(bbre), expand(bbim)], axis=1)
    cont = lambda c: jnp.tile(c.transpose(0, 2, 1).reshape(SSM_N, SSM_H), (1, SSM_G)) * mask.T
    cmat = jnp.concatenate([cont(c_re), -cont(c_im)], axis=0)
    return bmat.astype(BF16), cmat.astype(BF16)


def kernel(x, meta_tokens, g_pre_mix, g_post_mix, g_pre_ffn, g_post_ffn, w_in, ml_gate_bias, fx_gate_bias, ssm_lam_re, ssm_lam_im, ssm_log_dt, ssm_b_re, ssm_b_im, ssm_c_re, ssm_c_im, ssm_d, ssm_glu_w, ssm_glu_b, ml_conv_w, ml_norm_g, pool_w, pool_scale, w_out, mlp_w1, mlp_w2):
    bsz, seq, d = x.shape
    depth = w_in.shape[0]
    lp = PAD + N_META + seq
    assert d == D_MODEL and lp % BLOCK == 0
    meta = jnp.broadcast_to(meta_tokens[None].astype(x.dtype), (bsz, N_META, d))
    h = jnp.concatenate([jnp.zeros((bsz, PAD, d), x.dtype), meta, x], axis=1).reshape(bsz * lp, d)
    row = lambda a: a.reshape(1, -1).astype(F32)

    for l in range(depth):
        w_main, w_gate, colscale = _split_w_in(w_in[l])
        proj, graw = _inproj(h, row(g_pre_mix[l]), w_main, w_gate, colscale)

        gbias = jnp.pad(jnp.concatenate([ml_gate_bias[l], fx_gate_bias[l]]).astype(F32), (0, LANES - 3 * HEADS))
        gcol, grow, fxrow = _gates(graw, gbias.reshape(1, LANES), bsz, lp)

        consts, bbre, bbim = _s5_prep(ssm_lam_re[l], ssm_lam_im[l], ssm_log_dt[l], ssm_b_re[l], ssm_b_im[l])
        bmat, cmat = _s5_matrices(bbre, bbim, ssm_c_re[l], ssm_c_im[l])
        y_ssm = _s5(proj, bmat, cmat, row(ssm_d[l]), consts, ssm_glu_w[l].astype(BF16), row(ssm_glu_b[l]), bsz, lp)
        y_ml = _mlstm(proj, gcol, grow, ml_conv_w[l].astype(F32), row(ml_norm_g[l]), bsz, lp)
        y_fx = _fox(proj, gcol, fxrow, bsz, lp)
        y_pool = _pool(proj, pool_w[l].astype(BF16), row(pool_scale[l]), bsz, lp)

        h = _outproj(y_ssm, y_ml, y_fx, y_pool, w_out[l].astype(BF16), row(g_post_mix[l]), h, lp)
        h = _ffn(h, row(g_pre_ffn[l]), mlp_w1[l].astype(BF16), mlp_w2[l].astype(BF16), row(g_post_ffn[l]))

    return h.reshape(bsz, lp, d)[:, PAD + N_META:]
```

```python
import functools
import math

import jax
import jax.numpy as jnp
from jax import lax
from jax.experimental import pallas as pl
from jax.experimental.pallas import tpu as pltpu

F32 = jnp.float32
BF16 = jnp.bfloat16

D_MODEL = 2048
N_META = 16
BLOCK = 128
PAD = BLOCK - N_META
GROUP_W = 512
SSM_H = 16
SSM_G = GROUP_W // SSM_H
SSM_P = 64
SSM_N = SSM_G * SSM_P
HEADS = 4
DH = GROUP_W // HEADS
ML_CONV = 4
POOL_WINDOWS = (2, 4, 8, 16)
POOL_MAXW = 16
D_FF = 4 * D_MODEL
EPS = 1e-6
NEG = -1e30
N_MAIN = 9 * GROUP_W
LANES = 128
SUBLANES = 8
VMEM_LIMIT = 56 * 1024 * 1024

COL_SU, COL_MQ, COL_MK, COL_MV, COL_MO, COL_FQ, COL_FK, COL_FV, COL_PU = range(9)
GC_MI, GC_MF, GC_FF = 0, HEADS, 2 * HEADS


def _pick(n, target, mult):
    best = None
    for d in range(mult, min(n, target) + 1, mult):
        if n % d == 0:
            best = d
    assert best is not None, (n, target, mult)
    return best


def _params(sem):
    return pltpu.CompilerParams(dimension_semantics=sem, vmem_limit_bytes=VMEM_LIMIT)


def _rms(x, g):
    return x * lax.rsqrt(jnp.mean(x * x, axis=-1, keepdims=True) + EPS) * g


def _inproj_kernel(x_ref, g_ref, w_ref, wg_ref, cs_ref, o_ref, og_ref, xn_ref):
    @pl.when(pl.program_id(1) == 0)
    def _():
        xn = _rms(x_ref[...], g_ref[...]).astype(BF16)
        xn_ref[...] = xn
        og_ref[...] = jnp.dot(xn, wg_ref[...], preferred_element_type=F32)

    acc = jnp.dot(xn_ref[...], w_ref[...], preferred_element_type=F32)
    o_ref[...] = (acc * cs_ref[...]).astype(BF16)


def _inproj(h2, g, w_main, w_gate, colscale):
    m = h2.shape[0]
    tm = _pick(m, 1056, 16)
    tn = 1536
    return pl.pallas_call(
        _inproj_kernel,
        grid=(m // tm, N_MAIN // tn),
        in_specs=[
            pl.BlockSpec((tm, D_MODEL), lambda i, j: (i, 0)),
            pl.BlockSpec((1, D_MODEL), lambda i, j: (0, 0)),
            pl.BlockSpec((D_MODEL, tn), lambda i, j: (0, j)),
            pl.BlockSpec((D_MODEL, LANES), lambda i, j: (0, 0)),
            pl.BlockSpec((1, tn), lambda i, j: (0, j)),
        ],
        out_specs=[
            pl.BlockSpec((tm, tn), lambda i, j: (i, j)),
            pl.BlockSpec((tm, LANES), lambda i, j: (i, 0)),
        ],
        out_shape=[jax.ShapeDtypeStruct((m, N_MAIN), BF16), jax.ShapeDtypeStruct((m, LANES), F32)],
        scratch_shapes=[pltpu.VMEM((tm, D_MODEL), BF16)],
        compiler_params=_params(("arbitrary", "arbitrary")),
        name="inproj",
    )(h2, g, w_main, w_gate, colscale)


def _gates_kernel(raw_ref, bias_ref, col_ref, row_ref, carry_ref):
    c = pl.program_id(1)

    @pl.when(c == 0)
    def _():
        carry_ref[...] = jnp.zeros_like(carry_ref)

    g = raw_ref[...] + bias_ref[...]
    r_io = lax.broadcasted_iota(jnp.int32, (BLOCK, LANES), 0)
    c_io = lax.broadcasted_iota(jnp.int32, (BLOCK, LANES), 1)
    valid = (r_io + c * BLOCK) >= PAD
    logsig = jnp.minimum(g, 0.0) - jnp.log(1.0 + jnp.exp(-jnp.abs(g)))
    x = jnp.where(valid, logsig, 0.0)
    x = jnp.where(c_io >= GC_MF, jnp.where(c_io < GC_FF + HEADS, x, 0.0), 0.0)
    tri = (r_io >= c_io).astype(F32)
    cum = jnp.dot(tri, x, precision=lax.Precision.HIGHEST, preferred_element_type=F32)
    cum = cum + jnp.where(c_io >= GC_FF, carry_ref[...], 0.0)
    carry_ref[...] = cum[BLOCK - 1:BLOCK, :]
    log_i = jnp.where(valid, g, NEG)
    out = jnp.where(c_io < GC_MF, log_i, cum)
    col_ref[...] = out
    row_ref[0, 0] = out.T[0:2 * SUBLANES, :]


def _gates(raw, bias, bsz, lp):
    nc = lp // BLOCK
    return pl.pallas_call(
        _gates_kernel,
        grid=(bsz, nc),
        in_specs=[
            pl.BlockSpec((BLOCK, LANES), lambda b, c: (b * nc + c, 0)),
            pl.BlockSpec((1, LANES), lambda b, c: (0, 0)),
        ],
        out_specs=[
            pl.BlockSpec((BLOCK, LANES), lambda b, c: (b * nc + c, 0)),
            pl.BlockSpec((1, 1, 2 * SUBLANES, BLOCK), lambda b, c: (b, c, 0, 0)),
        ],
        out_shape=[
            jax.ShapeDtypeStruct((bsz * lp, LANES), F32),
            jax.ShapeDtypeStruct((bsz, nc, 2 * SUBLANES, BLOCK), F32),
        ],
        scratch_shapes=[pltpu.VMEM((1, LANES), F32)],
        compiler_params=_params(("arbitrary", "arbitrary")),
        name="gates",
    )(raw, bias)


def _s5_prep_kernel(lre_ref, lim_ref, ldt_ref, bre_ref, bim_ref, consts_ref, bbre_ref, bbim_ref):
    lre = lre_ref[...]
    lim = lim_ref[...]
    dt = jnp.exp(ldt_ref[...])
    r_io = lax.broadcasted_iota(jnp.int32, (SUBLANES, SSM_N), 0)
    kk = (r_io + 1).astype(F32)
    mag = jnp.exp(kk * (lre * dt))
    ang = kk * (lim * dt)
    pw_re = mag * jnp.cos(ang)
    pw_im = mag * jnp.sin(ang)
    for i, d in enumerate((1, 2, 4)):
        keep = r_io >= d
        consts_ref[16 * i:16 * i + 8, :] = jnp.where(keep, pw_re[d - 1:d, :], 0.0)
        consts_ref[16 * i + 8:16 * i + 16, :] = jnp.where(keep, pw_im[d - 1:d, :], 0.0)
    consts_ref[48:56, :] = pw_re
    consts_ref[56:64, :] = pw_im
    nr = pw_re[0:1, :] - 1.0
    ni = pw_im[0:1, :]
    den = lre * lre + lim * lim
    cr = (nr * lre + ni * lim) / den
    ci = (ni * lre - nr * lim) / den
    bre = bre_ref[...]
    bim = bim_ref[...]
    bbre_ref[...] = cr * bre - ci * bim
    bbim_ref[...] = cr * bim + ci * bre


def _s5_prep(lam_re, lam_im, log_dt, b_re, b_im):
    flat = lambda a: a.reshape(1, SSM_N)
    ldt = jnp.repeat(log_dt, SSM_P).reshape(1, SSM_N)
    bt = lambda a: a.transpose(2, 0, 1).reshape(SSM_H, SSM_N)
    return pl.pallas_call(
        _s5_prep_kernel,
        out_shape=[
            jax.ShapeDtypeStruct((64, SSM_N), F32),
            jax.ShapeDtypeStruct((SSM_H, SSM_N), F32),
            jax.ShapeDtypeStruct((SSM_H, SSM_N), F32),
        ],
        name="s5_prep",
    )(flat(lam_re), flat(lam_im), ldt, bt(b_re), bt(b_im))


S5_STRIPS = SSM_N // LANES
S5_STRIPS_PER_LOOP = 4


def _s5_kernel(u_ref, bmat_ref, cmat_ref, d_ref, consts_ref, gw_ref, gb_ref, y_ref, x_ref, carry_ref, *, tb):
    @pl.when(pl.program_id(1) == 0)
    def _():
        carry_ref[...] = jnp.zeros_like(carry_ref)

    u = u_ref[...]
    x_ref[...] = jnp.dot(u, bmat_ref[...], preferred_element_type=F32)

    def cmul_add(xr, xi, ar, ai, sr, si):
        return xr + ar * sr - ai * si, xi + ar * si + ai * sr

    for s0 in range(0, S5_STRIPS, S5_STRIPS_PER_LOOP):
        strips = range(s0, s0 + S5_STRIPS_PER_LOOP)

        def body(r, carry, strips=strips):
            rows = pl.ds(pl.multiple_of(r * SUBLANES, SUBLANES), SUBLANES)
            out = []
            for n, s in enumerate(strips):
                lr = slice(s * LANES, (s + 1) * LANES)
                li = slice(SSM_N + s * LANES, SSM_N + (s + 1) * LANES)
                xr = x_ref[rows, lr]
                xi = x_ref[rows, li]
                for i, d in enumerate((1, 2, 4)):
                    ar = consts_ref[16 * i:16 * i + 8, lr]
                    ai = consts_ref[16 * i + 8:16 * i + 16, lr]
                    xr, xi = cmul_add(xr, xi, ar, ai, pltpu.roll(xr, d, axis=0), pltpu.roll(xi, d, axis=0))
                xr, xi = cmul_add(xr, xi, consts_ref[48:56, lr], consts_ref[56:64, lr],
                                  carry[2 * n], carry[2 * n + 1])
                x_ref[rows, lr] = xr
                x_ref[rows, li] = xi
                out.append(jnp.broadcast_to(xr[SUBLANES - 1:SUBLANES, :], (SUBLANES, LANES)))
                out.append(jnp.broadcast_to(xi[SUBLANES - 1:SUBLANES, :], (SUBLANES, LANES)))
            return tuple(out)

        init = []
        for s in strips:
            init.append(carry_ref[:, s * LANES:(s + 1) * LANES])
            init.append(carry_ref[:, SSM_N + s * LANES:SSM_N + (s + 1) * LANES])
        fin = lax.fori_loop(0, tb // SUBLANES, body, tuple(init))
        for n, s in enumerate(strips):
            carry_ref[:, s * LANES:(s + 1) * LANES] = fin[2 * n]
            carry_ref[:, SSM_N + s * LANES:SSM_N + (s + 1) * LANES] = fin[2 * n + 1]

    y = jnp.dot(x_ref[...].astype(BF16), cmat_ref[...], preferred_element_type=F32)
    y = jax.nn.gelu(y + d_ref[...] * u.astype(F32), approximate=True)
    z = jnp.dot(y.astype(BF16), gw_ref[...], preferred_element_type=F32) + gb_ref[...]
    y_ref[...] = (z[:, :GROUP_W] * jax.nn.sigmoid(z[:, GROUP_W:])).astype(BF16)


def _s5(proj, bmat, cmat, dvec, consts, glu_w, glu_b, bsz, lp):
    tb = _pick(lp, 528, 16)
    nt = lp // tb
    const = lambda shape: pl.BlockSpec(shape, lambda b, t: (0,) * len(shape))
    return pl.pallas_call(
        functools.partial(_s5_kernel, tb=tb),
        grid=(bsz, nt),
        in_specs=[
            pl.BlockSpec((tb, GROUP_W), lambda b, t: (b * nt + t, COL_SU)),
            const((GROUP_W, 2 * SSM_N)),
            const((2 * SSM_N, GROUP_W)),
            const((1, GROUP_W)),
            const((64, SSM_N)),
            const((GROUP_W, 2 * GROUP_W)),
            const((1, 2 * GROUP_W)),
        ],
        out_specs=pl.BlockSpec((tb, GROUP_W), lambda b, t: (b * nt + t, 0)),
        out_shape=jax.ShapeDtypeStruct((bsz * lp, GROUP_W), BF16),
        scratch_shapes=[pltpu.VMEM((tb, 2 * SSM_N), F32), pltpu.VMEM((SUBLANES, 2 * SSM_N), F32)],
        compiler_params=_params(("arbitrary", "arbitrary")),
        name="s5",
    )(proj, bmat, cmat, dvec, consts, glu_w, glu_b)


def _mlstm_kernel(q_ref, k_ref, v_ref, o_ref, gcol_ref, grow_ref, cw_ref, ng_ref, y_ref,
                  buf_ref, c_ref, n_ref, m_ref):
    @pl.when(pl.program_id(1) == 0)
    def _():
        buf_ref[0:SUBLANES, :] = jnp.zeros((SUBLANES, 2 * GROUP_W), F32)
        c_ref[...] = jnp.zeros_like(c_ref)
        n_ref[...] = jnp.zeros_like(n_ref)
        m_ref[...] = jnp.zeros_like(m_ref)

    @pl.when(pl.program_id(1) > 0)
    def _():
        buf_ref[0:SUBLANES, :] = buf_ref[BLOCK:BLOCK + SUBLANES, :]

    buf_ref[SUBLANES:SUBLANES + BLOCK, 0:GROUP_W] = q_ref[...].astype(F32)
    buf_ref[SUBLANES:SUBLANES + BLOCK, GROUP_W:] = k_ref[...].astype(F32)
    conv = cw_ref[ML_CONV - 1:ML_CONV, :] * buf_ref[SUBLANES:SUBLANES + BLOCK, :]
    for j in range(1, ML_CONV):
        conv = conv + cw_ref[ML_CONV - 1 - j:ML_CONV - j, :] * buf_ref[SUBLANES - j:SUBLANES - j + BLOCK, :]
    qk = conv * jax.nn.sigmoid(conv)

    t_io = lax.broadcasted_iota(jnp.int32, (BLOCK, BLOCK), 0)
    s_io = lax.broadcasted_iota(jnp.int32, (BLOCK, BLOCK), 1)
    causal = t_io >= s_io
    gcol = gcol_ref[...]
    grow = grow_ref[0, 0]
    nt_dims = (((1,), (1,)), ((), ()))
    tn_dims = (((0,), (0,)), ((), ()))

    for h in range(HEADS):
        cols = slice(h * DH, (h + 1) * DH)
        qh = qk[:, cols].astype(BF16)
        kh_f = qk[:, GROUP_W + h * DH:GROUP_W + (h + 1) * DH] * (DH ** -0.5)
        kh = kh_f.astype(BF16)
        vh = v_ref[:, cols]
        li_col = gcol[:, GC_MI + h:GC_MI + h + 1]
        b_col = gcol[:, GC_MF + h:GC_MF + h + 1]
        li_row = grow[GC_MI + h:GC_MI + h + 1, :]
        b_row = grow[GC_MF + h:GC_MF + h + 1, :]
        g = b_row[:, BLOCK - 1:BLOCK]
        c_prev = c_ref[h]
        n_prev = n_ref[h]
        m_prev = m_ref[h][:, 0:1]

        d_log = jnp.where(causal, b_col - b_row + li_row, NEG)
        m_inter = b_col + m_prev
        m_comb = jnp.maximum(m_inter, jnp.max(d_log, axis=-1, keepdims=True))
        qk_s = lax.dot_general(qh, kh, nt_dims, preferred_element_type=F32)
        scores = qk_s * jnp.exp(d_log - m_comb)
        inter_w = jnp.exp(m_inter - m_comb)
        q_c = lax.dot_general(qh, c_prev.astype(BF16), nt_dims, preferred_element_type=F32)
        num = jnp.dot(scores.astype(BF16), vh, preferred_element_type=F32) + inter_w * q_c
        q_n = jnp.sum(qk[:, cols] * n_prev, axis=-1, keepdims=True)
        den = jnp.sum(scores, axis=-1, keepdims=True) + inter_w * q_n
        hh = num / jnp.maximum(jnp.abs(den), jnp.exp(-m_comb))
        hh = hh * lax.rsqrt(jnp.mean(hh * hh, axis=-1, keepdims=True) + EPS) * ng_ref[:, cols]
        y_ref[:, cols] = (hh * jax.nn.sigmoid(o_ref[:, cols].astype(F32))).astype(BF16)

        a_col = g - b_col + li_col
        m_loc = jnp.max(g - b_row + li_row, axis=-1, keepdims=True)
        w_col = jnp.exp(a_col - m_loc)
        vw = (vh.astype(F32) * w_col).astype(BF16)
        c_loc = lax.dot_general(vw, kh, tn_dims, preferred_element_type=F32)
        n_loc = jnp.sum(w_col * kh_f, axis=0, keepdims=True)
        m_new = jnp.maximum(g + m_prev, m_loc)
        decay = jnp.exp(g + m_prev - m_new)
        scale = jnp.exp(m_loc - m_new)
        c_ref[h] = decay * c_prev + scale * c_loc
        n_ref[h] = decay * n_prev + scale * n_loc
        m_ref[h] = jnp.broadcast_to(m_new, (1, LANES))


def _mlstm(proj, gcol, grow, conv_w, norm_g, bsz, lp):
    nc = lp // BLOCK
    blk = lambda col: pl.BlockSpec((BLOCK, GROUP_W), lambda b, c: (b * nc + c, col))
    return pl.pallas_call(
        _mlstm_kernel,
        grid=(bsz, nc),
        in_specs=[
            blk(COL_MQ), blk(COL_MK), blk(COL_MV), blk(COL_MO),
            pl.BlockSpec((BLOCK, LANES), lambda b, c: (b * nc + c, 0)),
            pl.BlockSpec((1, 1, 2 * SUBLANES, BLOCK), lambda b, c: (b, c, 0, 0)),
            pl.BlockSpec((ML_CONV, 2 * GROUP_W), lambda b, c: (0, 0)),
            pl.BlockSpec((1, GROUP_W), lambda b, c: (0, 0)),
        ],
        out_specs=pl.BlockSpec((BLOCK, GROUP_W), lambda b, c: (b * nc + c, 0)),
        out_shape=jax.ShapeDtypeStruct((bsz * lp, GROUP_W), BF16),
        scratch_shapes=[
            pltpu.VMEM((SUBLANES + BLOCK, 2 * GROUP_W), F32),
            pltpu.VMEM((HEADS, DH, DH), F32),
            pltpu.VMEM((HEADS, 1, DH), F32),
            pltpu.VMEM((HEADS, 1, LANES), F32),
        ],
        compiler_params=_params(("arbitrary", "arbitrary")),
        name="mlstm",
    )(proj, proj, proj, proj, gcol, grow, conv_w, norm_g)


FOX_ROWS = 32
FOX_BIG = 1e30


def _split3(c):
    hi = c.astype(BF16).astype(F32)
    r = c - hi
    mid = r.astype(BF16).astype(F32)
    return hi, mid, r - mid


def _fox_kernel(q_ref, k_ref, v_ref, gcol_ref, y_ref, kx_ref, vx_ref, qx_ref, s_ref, p_ref, acc_ref,
                m_ref, al_ref, *, tq, lp):
    qi = pl.program_id(1)
    nq = lp // tq
    lane = lax.broadcasted_iota(jnp.int32, (tq, LANES), 1)
    nt_dims = (((1,), (1,)), ((), ()))

    @pl.when(qi == 0)
    def _():
        def build(cb, carry):
            r0 = pl.multiple_of(cb * tq, tq)
            rows = pl.ds(r0, tq)
            is_token = (r0 + lax.broadcasted_iota(jnp.int32, (tq, 1), 0)) >= PAD
            g = gcol_ref[rows, :]
            ones_col = jnp.where(lane == 0, 1.0, 0.0).astype(BF16)
            for h in range(HEADS):
                cols = slice(h * DH, (h + 1) * DH)
                hi, mid, lo = _split3(jnp.where(is_token, g[:, GC_FF + h:GC_FF + h + 1], FOX_BIG))
                ext = jnp.where(lane < 3, 1.0, jnp.where(lane == 3, -hi, jnp.where(lane == 4, -mid,
                                jnp.where(lane == 5, -lo, 0.0))))
                kx_ref[h, rows, 0:DH] = k_ref[rows, cols]
                kx_ref[h, rows, DH:] = ext.astype(BF16)
                vx_ref[h, rows, 0:DH] = v_ref[rows, cols]
                vx_ref[h, rows, DH:] = ones_col
            return carry

        lax.fori_loop(0, nq, build, 0)

    gq = gcol_ref[pl.ds(pl.multiple_of(qi * tq, tq), tq), :]
    row_io = lax.broadcasted_iota(jnp.int32, (FOX_ROWS, tq), 0)
    col_io = lax.broadcasted_iota(jnp.int32, (FOX_ROWS, tq), 1)

    for h in range(HEADS):
        hi, mid, lo = _split3(gq[:, GC_FF + h:GC_FF + h + 1])
        ext = jnp.where(lane == 0, hi, jnp.where(lane == 1, mid, jnp.where(lane == 2, lo,
                        jnp.where(lane < 6, 1.0, 0.0))))
        qx_ref[h, :, 0:DH] = q_ref[:, h * DH:(h + 1) * DH]
        qx_ref[h, :, DH:] = ext.astype(BF16)
    m_ref[...] = jnp.full(m_ref.shape, NEG, F32)
    acc_ref[...] = jnp.zeros_like(acc_ref)

    def scores(kb, h):
        krows = pl.ds(pl.multiple_of(kb * tq, tq), tq)
        s_ref[h] = lax.dot_general(qx_ref[h], kx_ref[h, krows, :], nt_dims, preferred_element_type=F32)

    def step(kb, diagonal):
        krows = pl.ds(pl.multiple_of(kb * tq, tq), tq)
        for h in range(HEADS):
            for r0 in range(0, tq, FOX_ROWS):
                rr = slice(r0, r0 + FOX_ROWS)
                sc = s_ref[h, rr, :]
                if diagonal:
                    sc = jnp.where(col_io <= row_io + r0, sc, NEG)
                m_old = m_ref[h, rr, :]
                m_new = jnp.maximum(m_old, jnp.max(sc, axis=-1, keepdims=True))
                al_ref[h, rr, :] = jnp.exp(m_old - m_new)
                m_ref[h, rr, :] = m_new
                p_ref[h, rr, :] = jnp.exp(sc - m_new).astype(BF16)
            if not diagonal:
                scores(kb + 1, h)
            pv = jnp.dot(p_ref[h], vx_ref[h, krows, :], preferred_element_type=F32)
            acc_ref[h] = al_ref[h] * acc_ref[h] + pv

    def off_diagonal(kb, carry):
        step(kb, False)
        return carry

    for h in range(HEADS):
        scores(0, h)
    lax.fori_loop(0, qi, off_diagonal, 0)
    step(qi, True)
    for h in range(HEADS):
        acc = acc_ref[h]
        y_ref[:, h * DH:(h + 1) * DH] = (acc[:, 0:DH] / acc[:, DH:DH + 1]).astype(BF16)


def _fox(proj, gcol, bsz, lp):
    tq = _pick(lp, 384, BLOCK)
    nq = lp // tq
    proj3 = proj.reshape(bsz, lp, N_MAIN)
    return pl.pallas_call(
        functools.partial(_fox_kernel, tq=tq, lp=lp),
        grid=(bsz, nq),
        in_specs=[
            pl.BlockSpec((None, tq, GROUP_W), lambda b, i: (b, i, COL_FQ)),
            pl.BlockSpec((None, lp, GROUP_W), lambda b, i: (b, 0, COL_FK)),
            pl.BlockSpec((None, lp, GROUP_W), lambda b, i: (b, 0, COL_FV)),
            pl.BlockSpec((lp, LANES), lambda b, i: (b, 0)),
        ],
        out_specs=pl.BlockSpec((tq, GROUP_W), lambda b, i: (b * nq + i, 0)),
        out_shape=jax.ShapeDtypeStruct((bsz * lp, GROUP_W), BF16),
        scratch_shapes=[
            pltpu.VMEM((HEADS, lp, 2 * DH), BF16),
            pltpu.VMEM((HEADS, lp, 2 * DH), BF16),
            pltpu.VMEM((HEADS, tq, 2 * DH), BF16),
            pltpu.VMEM((HEADS, tq, tq), F32),
            pltpu.VMEM((HEADS, tq, tq), BF16),
            pltpu.VMEM((HEADS, tq, 2 * DH), F32),
            pltpu.VMEM((HEADS, tq, 1), F32),
            pltpu.VMEM((HEADS, tq, 1), F32),
        ],
        compiler_params=_params(("arbitrary", "arbitrary")),
        name="fox",
    )(proj3, proj3, proj3, gcol)


def _pool_kernel(u_ref, w_ref, sc_ref, y_ref, buf_ref, *, tb):
    t = pl.program_id(1)

    @pl.when(t == 0)
    def _():
        buf_ref[0:POOL_MAXW, :] = jnp.zeros((POOL_MAXW, GROUP_W), F32)

    @pl.when(t > 0)
    def _():
        buf_ref[0:POOL_MAXW, :] = buf_ref[tb:tb + POOL_MAXW, :]

    buf_ref[POOL_MAXW:POOL_MAXW + tb, :] = u_ref[...].astype(F32)
    pos = t * tb + lax.broadcasted_iota(jnp.int32, (tb, 1), 0) - (PAD - 1)
    posf = jnp.maximum(pos, 1).astype(F32)
    gw = GROUP_W // len(POOL_WINDOWS)
    for gi, w in enumerate(POOL_WINDOWS):
        cols = slice(gi * gw, (gi + 1) * gw)
        x = buf_ref[POOL_MAXW:POOL_MAXW + tb, cols]
        s = x
        for j in range(1, w):
            s = s + buf_ref[POOL_MAXW - j:POOL_MAXW - j + tb, cols]
        pooled = s / jnp.minimum(posf, float(w)) - x
        mixed = jnp.dot(pooled.astype(BF16), w_ref[gi], preferred_element_type=F32)
        y_ref[:, cols] = (mixed * sc_ref[:, cols]).astype(BF16)


def _pool(proj, pool_w, pool_scale, bsz, lp):
    tb = _pick(lp, 528, 16)
    nt = lp // tb
    gw = GROUP_W // len(POOL_WINDOWS)
    return pl.pallas_call(
        functools.partial(_pool_kernel, tb=tb),
        grid=(bsz, nt),
        in_specs=[
            pl.BlockSpec((tb, GROUP_W), lambda b, t: (b * nt + t, COL_PU)),
            pl.BlockSpec((len(POOL_WINDOWS), gw, gw), lambda b, t: (0, 0, 0)),
            pl.BlockSpec((1, GROUP_W), lambda b, t: (0, 0)),
        ],
        out_specs=pl.BlockSpec((tb, GROUP_W), lambda b, t: (b * nt + t, 0)),
        out_shape=jax.ShapeDtypeStruct((bsz * lp, GROUP_W), BF16),
        scratch_shapes=[pltpu.VMEM((POOL_MAXW + tb, GROUP_W), F32)],
        compiler_params=_params(("arbitrary", "arbitrary")),
        name="pool",
    )(proj, pool_w, pool_scale)


def _outproj_kernel(ys_ref, ym_ref, yf_ref, yp_ref, w_ref, g_ref, h_ref, o_ref, *, tm, blocks_per_seq):
    mix = jnp.dot(ys_ref[...], w_ref[0:GROUP_W, :], preferred_element_type=F32)
    for n, y_ref in enumerate((ym_ref, yf_ref, yp_ref), start=1):
        mix = mix + jnp.dot(y_ref[...], w_ref[n * GROUP_W:(n + 1) * GROUP_W, :], preferred_element_type=F32)
    off = (pl.program_id(0) % blocks_per_seq) * tm
    valid = (off + lax.broadcasted_iota(jnp.int32, (tm, 1), 0)) >= PAD
    o_ref[...] = jnp.where(valid, h_ref[...] + _rms(mix, g_ref[...]), 0.0)


def _outproj(ys, ym, yf, yp, w_out, g, h2, lp):
    m = h2.shape[0]
    tm = _pick(lp, 528, 16)
    yspec = pl.BlockSpec((tm, GROUP_W), lambda i: (i, 0))
    return pl.pallas_call(
        functools.partial(_outproj_kernel, tm=tm, blocks_per_seq=lp // tm),
        grid=(m // tm,),
        in_specs=[
            yspec, yspec, yspec, yspec,
            pl.BlockSpec((D_MODEL, D_MODEL), lambda i: (0, 0)),
            pl.BlockSpec((1, D_MODEL), lambda i: (0, 0)),
            pl.BlockSpec((tm, D_MODEL), lambda i: (i, 0)),
        ],
        out_specs=pl.BlockSpec((tm, D_MODEL), lambda i: (i, 0)),
        out_shape=jax.ShapeDtypeStruct((m, D_MODEL), F32),
        input_output_aliases={6: 0},
        compiler_params=_params(("arbitrary",)),
        name="outproj",
    )(ys, ym, yf, yp, w_out, g, h2)


def _ffn_kernel(h_ref, g1_ref, w1_ref, w2_ref, g2_ref, o_ref, hn_ref, acc_ref):
    f = pl.program_id(1)

    @pl.when(f == 0)
    def _():
        hn_ref[...] = _rms(h_ref[...], g1_ref[...]).astype(BF16)
        acc_ref[...] = jnp.zeros_like(acc_ref)

    a = jnp.maximum(jnp.dot(hn_ref[...], w1_ref[...], preferred_element_type=F32), 0.0)
    acc_ref[...] += jnp.dot((a * a).astype(BF16), w2_ref[...], preferred_element_type=F32)

    @pl.when(f == pl.num_programs(1) - 1)
    def _():
        o_ref[...] = h_ref[...] + _rms(acc_ref[...], g2_ref[...])


def _ffn(h2, g1, w1, w2, g2):
    m = h2.shape[0]
    tm = _pick(m, 768, 16)
    tf = 512
    return pl.pallas_call(
        _ffn_kernel,
        grid=(m // tm, D_FF // tf),
        in_specs=[
            pl.BlockSpec((tm, D_MODEL), lambda i, f: (i, 0)),
            pl.BlockSpec((1, D_MODEL), lambda i, f: (0, 0)),
            pl.BlockSpec((D_MODEL, tf), lambda i, f: (0, f)),
            pl.BlockSpec((tf, D_MODEL), lambda i, f: (f, 0)),
            pl.BlockSpec((1, D_MODEL), lambda i, f: (0, 0)),
        ],
        out_specs=pl.BlockSpec((tm, D_MODEL), lambda i, f: (i, 0)),
        out_shape=jax.ShapeDtypeStruct((m, D_MODEL), F32),
        scratch_shapes=[pltpu.VMEM((tm, D_MODEL), BF16), pltpu.VMEM((tm, D_MODEL), F32)],
        input_output_aliases={0: 0},
        compiler_params=_params(("arbitrary", "arbitrary")),
        name="ffn",
    )(h2, g1, w1, w2, g2)


IN_COLS = N_MAIN + 3 * HEADS
IN_COLS_PAD = -(-IN_COLS // LANES) * LANES
O_MI = 5 * GROUP_W
O_FQ = O_MI + 2 * HEADS
O_FF = O_FQ + 3 * GROUP_W
O_PU = O_FF + HEADS


def _split_w_in_kernel(w_ref, main_ref, gate_ref):
    rows = w_ref.shape[0]
    lane = lax.broadcasted_iota(jnp.int32, (rows, LANES), 1)
    main_ref[:, 0:O_MI] = w_ref[:, 0:O_MI].astype(BF16)

    def shifted_copy(dst0, src0, width):
        shift = src0 % LANES
        base = src0 - shift
        for c in range(0, width, LANES):
            lo = w_ref[:, base + c:base + c + LANES]
            hi = w_ref[:, base + c + LANES:base + c + 2 * LANES]
            mixed = jnp.where(lane >= shift, lo, hi)
            main_ref[:, dst0 + c:dst0 + c + LANES] = pltpu.roll(mixed, LANES - shift, axis=1).astype(BF16)

    shifted_copy(O_MI, O_FQ, 3 * GROUP_W)
    shifted_copy(O_MI + 3 * GROUP_W, O_PU, GROUP_W)
    g_ml = w_ref[:, O_MI:O_MI + LANES]
    ff0 = O_FF - O_FF % LANES
    g_fx = w_ref[:, ff0:ff0 + LANES]
    assert O_FF - ff0 == GC_FF
    gate = jnp.where(lane < GC_FF, g_ml, jnp.where(lane < GC_FF + HEADS, g_fx, 0.0))
    gate_ref[...] = gate.astype(BF16)


def _split_w_in(w_all, l):
    tr = 256
    return pl.pallas_call(
        _split_w_in_kernel,
        grid=(D_MODEL // tr,),
        in_specs=[pl.BlockSpec((None, tr, IN_COLS_PAD), lambda i: (l, i, 0))],
        out_specs=[pl.BlockSpec((tr, N_MAIN), lambda i: (i, 0)), pl.BlockSpec((tr, LANES), lambda i: (i, 0))],
        out_shape=[jax.ShapeDtypeStruct((D_MODEL, N_MAIN), BF16), jax.ShapeDtypeStruct((D_MODEL, LANES), BF16)],
        compiler_params=_params(("arbitrary",)),
        name="split_w_in",
    )(w_all)


def _s5_matrices(bbre, bbim, c_re, c_im):
    grp_rows = jnp.arange(GROUP_W) // SSM_H
    grp_cols = jnp.arange(SSM_N) // SSM_P
    mask = (grp_rows[:, None] == grp_cols[None, :]).astype(F32)
    expand = lambda bb: jnp.tile(bb, (SSM_G, 1)) * mask
    bmat = jnp.concatenate([expand(bbre), expand(bbim)], axis=1)
    cont = lambda c: jnp.tile(c.transpose(0, 2, 1).reshape(SSM_N, SSM_H), (1, SSM_G)) * mask.T
    cmat = jnp.concatenate([cont(c_re), -cont(c_im)], axis=0)
    return bmat.astype(BF16), cmat.astype(BF16)


def kernel(x, meta_tokens, g_pre_mix, g_post_mix, g_pre_ffn, g_post_ffn, w_in, ml_gate_bias, fx_gate_bias, ssm_lam_re, ssm_lam_im, ssm_log_dt, ssm_b_re, ssm_b_im, ssm_c_re, ssm_c_im, ssm_d, ssm_glu_w, ssm_glu_b, ml_conv_w, ml_norm_g, pool_w, pool_scale, w_out, mlp_w1, mlp_w2):
    bsz, seq, d = x.shape
    depth = w_in.shape[0]
    lp = PAD + N_META + seq
    assert d == D_MODEL and lp % BLOCK == 0
    meta = jnp.broadcast_to(meta_tokens[None].astype(x.dtype), (bsz, N_META, d))
    h = jnp.concatenate([jnp.zeros((bsz, PAD, d), x.dtype), meta, x], axis=1).reshape(bsz * lp, d)
    row = lambda a: a.reshape(1, -1).astype(F32)

    colscale = jnp.ones((1, N_MAIN), F32).at[:, COL_FQ * GROUP_W:(COL_FQ + 1) * GROUP_W].set(DH ** -0.5)
    for l in range(depth):
        w_main, w_gate = _split_w_in(w_in, l)
        proj, graw = _inproj(h, row(g_pre_mix[l]), w_main, w_gate, colscale)

        gbias = jnp.pad(jnp.concatenate([ml_gate_bias[l], fx_gate_bias[l]]).astype(F32), (0, LANES - 3 * HEADS))
        gcol, grow = _gates(graw, gbias.reshape(1, LANES), bsz, lp)

        consts, bbre, bbim = _s5_prep(ssm_lam_re[l], ssm_lam_im[l], ssm_log_dt[l], ssm_b_re[l], ssm_b_im[l])
        bmat, cmat = _s5_matrices(bbre, bbim, ssm_c_re[l], ssm_c_im[l])
        y_ssm = _s5(proj, bmat, cmat, row(ssm_d[l]), consts, ssm_glu_w[l].astype(BF16), row(ssm_glu_b[l]), bsz, lp)
        y_ml = _mlstm(proj, gcol, grow, ml_conv_w[l].astype(F32), row(ml_norm_g[l]), bsz, lp)
        y_fx = _fox(proj, gcol, bsz, lp)
        y_pool = _pool(proj, pool_w[l].astype(BF16), row(pool_scale[l]), bsz, lp)

        h = _outproj(y_ssm, y_ml, y_fx, y_pool, w_out[l].astype(BF16), row(g_post_mix[l]), h, lp)
        h = _ffn(h, row(g_pre_ffn[l]), mlp_w1[l].astype(BF16), mlp_w2[l].astype(BF16), row(g_post_ffn[l]))

    return h.reshape(bsz, lp, d)[:, PAD + N_META:]
```

```python
import functools
import math

import jax
import jax.numpy as jnp
from jax import lax
from jax.experimental import pallas as pl
from jax.experimental.pallas import tpu as pltpu

F32 = jnp.float32
BF16 = jnp.bfloat16

D_MODEL = 2048
N_META = 16
BLOCK = 128
PAD = BLOCK - N_META
GROUP_W = 512
SSM_H = 16
SSM_G = GROUP_W // SSM_H
SSM_P = 64
SSM_N = SSM_G * SSM_P
HEADS = 4
DH = GROUP_W // HEADS
ML_CONV = 4
POOL_WINDOWS = (2, 4, 8, 16)
POOL_MAXW = 16
D_FF = 4 * D_MODEL
EPS = 1e-6
NEG = -1e30
N_MAIN = 9 * GROUP_W
LANES = 128
SUBLANES = 8
VMEM_LIMIT = 56 * 1024 * 1024

COL_SU, COL_MQ, COL_MK, COL_MV, COL_MO, COL_FQ, COL_FK, COL_FV, COL_PU = range(9)
GC_MI, GC_MF, GC_FF = 0, HEADS, 2 * HEADS


def _pick(n, target, mult):
    best = None
    for d in range(mult, min(n, target) + 1, mult):
        if n % d == 0:
            best = d
    assert best is not None, (n, target, mult)
    return best


def _params(sem):
    return pltpu.CompilerParams(dimension_semantics=sem, vmem_limit_bytes=VMEM_LIMIT)


def _rms(x, g):
    return x * lax.rsqrt(jnp.mean(x * x, axis=-1, keepdims=True) + EPS) * g


def _inproj_kernel(x_ref, g_ref, w_ref, wg_ref, cs_ref, o_ref, og_ref, xn_ref):
    @pl.when(pl.program_id(1) == 0)
    def _():
        xn = _rms(x_ref[...], g_ref[...]).astype(BF16)
        xn_ref[...] = xn
        og_ref[...] = jnp.dot(xn, wg_ref[...], preferred_element_type=F32)

    acc = jnp.dot(xn_ref[...], w_ref[...], preferred_element_type=F32)
    o_ref[...] = (acc * cs_ref[...]).astype(BF16)


def _inproj(h2, g, w_main, w_gate, colscale):
    m = h2.shape[0]
    tm = _pick(m, 1056, 16)
    tn = 1536
    return pl.pallas_call(
        _inproj_kernel,
        grid=(m // tm, N_MAIN // tn),
        in_specs=[
            pl.BlockSpec((tm, D_MODEL), lambda i, j: (i, 0)),
            pl.BlockSpec((1, D_MODEL), lambda i, j: (0, 0)),
            pl.BlockSpec((D_MODEL, tn), lambda i, j: (0, j)),
            pl.BlockSpec((D_MODEL, LANES), lambda i, j: (0, 0)),
            pl.BlockSpec((1, tn), lambda i, j: (0, j)),
        ],
        out_specs=[
            pl.BlockSpec((tm, tn), lambda i, j: (i, j)),
            pl.BlockSpec((tm, LANES), lambda i, j: (i, 0)),
        ],
        out_shape=[jax.ShapeDtypeStruct((m, N_MAIN), BF16), jax.ShapeDtypeStruct((m, LANES), F32)],
        scratch_shapes=[pltpu.VMEM((tm, D_MODEL), BF16)],
        compiler_params=_params(("arbitrary", "arbitrary")),
        name="inproj",
    )(h2, g, w_main, w_gate, colscale)


def _gates_kernel(raw_ref, bias_ref, col_ref, row_ref, carry_ref):
    c = pl.program_id(0)

    @pl.when(c == 0)
    def _():
        carry_ref[...] = jnp.zeros_like(carry_ref)

    r_io = lax.broadcasted_iota(jnp.int32, (BLOCK, LANES), 0)
    c_io = lax.broadcasted_iota(jnp.int32, (BLOCK, LANES), 1)
    valid = (r_io + c * BLOCK) >= PAD
    tri = (r_io >= c_io).astype(F32)
    for b in range(raw_ref.shape[0]):
        g = raw_ref[b] + bias_ref[...]
        logsig = jnp.minimum(g, 0.0) - jnp.log(1.0 + jnp.exp(-jnp.abs(g)))
        x = jnp.where(valid, logsig, 0.0)
        x = jnp.where(c_io >= GC_MF, jnp.where(c_io < GC_FF + HEADS, x, 0.0), 0.0)
        cum = jnp.dot(tri, x, precision=lax.Precision.HIGHEST, preferred_element_type=F32)
        cum = cum + jnp.where(c_io >= GC_FF, carry_ref[b], 0.0)
        carry_ref[b] = cum[BLOCK - 1:BLOCK, :]
        log_i = jnp.where(valid, g, NEG)
        out = jnp.where(c_io < GC_MF, log_i, cum)
        col_ref[b] = out
        row_ref[b, 0] = out.T[0:2 * SUBLANES, :]


def _gates(raw, bias, bsz, lp):
    nc = lp // BLOCK
    return pl.pallas_call(
        _gates_kernel,
        grid=(nc,),
        in_specs=[
            pl.BlockSpec((bsz, BLOCK, LANES), lambda c: (0, c, 0)),
            pl.BlockSpec((1, LANES), lambda c: (0, 0)),
        ],
        out_specs=[
            pl.BlockSpec((bsz, BLOCK, LANES), lambda c: (0, c, 0)),
            pl.BlockSpec((bsz, 1, 2 * SUBLANES, BLOCK), lambda c: (0, c, 0, 0)),
        ],
        out_shape=[
            jax.ShapeDtypeStruct((bsz, lp, LANES), F32),
            jax.ShapeDtypeStruct((bsz, nc, 2 * SUBLANES, BLOCK), F32),
        ],
        scratch_shapes=[pltpu.VMEM((bsz, 1, LANES), F32)],
        compiler_params=_params(("arbitrary",)),
        name="gates",
    )(raw.reshape(bsz, lp, LANES), bias)


S5_CONST_ROWS = 64
S5_STRIPS = SSM_N // LANES
S5_STRIPS_PER_LOOP = 4
S5_HALF = SSM_N // 2


def _s5_hs_steps(bsz):
    return [d for d in (1, 2, 4) if d < SUBLANES // bsz]


def _s5_prep_kernel(lre_ref, lim_ref, ldt_ref, bre_ref, bim_ref, consts_ref, bbre_ref, bbim_ref, *, bsz):
    lre = lre_ref[...]
    lim = lim_ref[...]
    dt = jnp.exp(ldt_ref[...])
    r_io = lax.broadcasted_iota(jnp.int32, (SUBLANES, SSM_N), 0)
    kk = (r_io + 1).astype(F32)
    mag = jnp.exp(kk * (lre * dt))
    ang = kk * (lim * dt)
    pw_re = mag * jnp.cos(ang)
    pw_im = mag * jnp.sin(ang)
    consts_ref[...] = jnp.zeros_like(consts_ref)
    steps = _s5_hs_steps(bsz)
    for i, d in enumerate(steps):
        keep = r_io >= d * bsz
        consts_ref[16 * i:16 * i + 8, :] = jnp.where(keep, pw_re[d - 1:d, :], 0.0)
        consts_ref[16 * i + 8:16 * i + 16, :] = jnp.where(keep, pw_im[d - 1:d, :], 0.0)
    t_in_vreg = lax.shift_right_logical(r_io, bsz.bit_length() - 1)
    cw_re = jnp.zeros((SUBLANES, SSM_N), F32)
    cw_im = jnp.zeros((SUBLANES, SSM_N), F32)
    for j in range(SUBLANES // bsz):
        cw_re = jnp.where(t_in_vreg == j, pw_re[j:j + 1, :], cw_re)
        cw_im = jnp.where(t_in_vreg == j, pw_im[j:j + 1, :], cw_im)
    base = 16 * len(steps)
    consts_ref[base:base + 8, :] = cw_re
    consts_ref[base + 8:base + 16, :] = cw_im
    nr = pw_re[0:1, :] - 1.0
    ni = pw_im[0:1, :]
    den = lre * lre + lim * lim
    cr = (nr * lre + ni * lim) / den
    ci = (ni * lre - nr * lim) / den
    bre = bre_ref[...]
    bim = bim_ref[...]
    bbre_ref[...] = cr * bre - ci * bim
    bbim_ref[...] = cr * bim + ci * bre


def _s5_prep(lam_re, lam_im, log_dt, b_re, b_im, bsz):
    flat = lambda a: a.reshape(1, SSM_N)
    ldt = jnp.repeat(log_dt, SSM_P).reshape(1, SSM_N)
    bt = lambda a: a.transpose(2, 0, 1).reshape(SSM_H, SSM_N)
    return pl.pallas_call(
        functools.partial(_s5_prep_kernel, bsz=bsz),
        out_shape=[
            jax.ShapeDtypeStruct((S5_CONST_ROWS, SSM_N), F32),
            jax.ShapeDtypeStruct((SSM_H, SSM_N), F32),
            jax.ShapeDtypeStruct((SSM_H, SSM_N), F32),
        ],
        name="s5_prep",
    )(flat(lam_re), flat(lam_im), ldt, bt(b_re), bt(b_im))


def _s5_kernel(u_ref, bmat_ref, cmat_ref, d_ref, consts_ref, gw_ref, gb_ref, y_ref,
               uil_ref, oil_ref, x_ref, carry_ref, *, tb, bsz):
    rows = bsz * tb
    n_us = GROUP_W // LANES

    @pl.when(pl.program_id(0) == 0)
    def _():
        carry_ref[...] = jnp.zeros_like(carry_ref)

    for b in range(bsz):
        ub = u_ref[b].astype(F32)
        for s in range(n_us):
            uil_ref[s, pl.ds(b, tb, stride=bsz), :] = ub[:, s * LANES:(s + 1) * LANES]
    u = jnp.concatenate([uil_ref[s] for s in range(n_us)], axis=1)
    u16 = u.astype(BF16)
    half_w = GROUP_W // 2
    for k in range(2):
        x_ref[:, 2 * S5_HALF * k:2 * S5_HALF * (k + 1)] = jnp.dot(
            u16[:, half_w * k:half_w * (k + 1)], bmat_ref[k], preferred_element_type=F32)

    def cmul_add(xr, xi, ar, ai, sr, si):
        return xr + ar * sr - ai * si, xi + ar * si + ai * sr

    row8 = lax.broadcasted_iota(jnp.int32, (SUBLANES, LANES), 0)
    steps = _s5_hs_steps(bsz)
    cbase = 16 * len(steps)

    def last_step_tile(c):
        span = SUBLANES
        while span > bsz:
            half = span // 2
            c = jnp.where((row8 & (span - 1)) < half, pltpu.roll(c, half, axis=0), c)
            span = half
        return c

    def lanes_of(s):
        k, j = divmod(s, S5_STRIPS // 2)
        re0 = 2 * S5_HALF * k + j * LANES
        return slice(re0, re0 + LANES), slice(re0 + S5_HALF, re0 + S5_HALF + LANES), slice(s * LANES, (s + 1) * LANES)

    for s0 in range(0, S5_STRIPS, S5_STRIPS_PER_LOOP):
        strips = range(s0, s0 + S5_STRIPS_PER_LOOP)

        def body(r, carry, strips=strips):
            r8 = pl.ds(pl.multiple_of(r * SUBLANES, SUBLANES), SUBLANES)
            out = []
            for n, s in enumerate(strips):
                lr, li, lc = lanes_of(s)
                xr = x_ref[r8, lr]
                xi = x_ref[r8, li]
                for i, d in enumerate(steps):
                    xr, xi = cmul_add(xr, xi, consts_ref[16 * i:16 * i + 8, lc], consts_ref[16 * i + 8:16 * i + 16, lc],
                                      pltpu.roll(xr, d * bsz, axis=0), pltpu.roll(xi, d * bsz, axis=0))
                xr, xi = cmul_add(xr, xi, consts_ref[cbase:cbase + 8, lc], consts_ref[cbase + 8:cbase + 16, lc],
                                  carry[2 * n], carry[2 * n + 1])
                x_ref[r8, lr] = xr
                x_ref[r8, li] = xi
                out.append(last_step_tile(xr))
                out.append(last_step_tile(xi))
            return tuple(out)

        init = []
        for s in strips:
            lr, li, _ = lanes_of(s)
            init += [carry_ref[:, lr], carry_ref[:, li]]
        fin = lax.fori_loop(0, rows // SUBLANES, body, tuple(init))
        for n, s in enumerate(strips):
            lr, li, _ = lanes_of(s)
            carry_ref[:, lr] = fin[2 * n]
            carry_ref[:, li] = fin[2 * n + 1]

    y = jnp.concatenate(
        [jnp.dot(x_ref[:, 2 * S5_HALF * k:2 * S5_HALF * (k + 1)].astype(BF16), cmat_ref[k],
                 preferred_element_type=F32) for k in range(2)], axis=1)
    y = jax.nn.gelu(y + d_ref[...] * u, approximate=True)
    z = jnp.dot(y.astype(BF16), gw_ref[...], preferred_element_type=F32) + gb_ref[...]
    out = z[:, :GROUP_W] * jax.nn.sigmoid(z[:, GROUP_W:])
    for s in range(n_us):
        oil_ref[s] = out[:, s * LANES:(s + 1) * LANES]
    for b in range(bsz):
        y_ref[b] = jnp.concatenate([oil_ref[s, pl.ds(b, tb, stride=bsz), :] for s in range(n_us)],
                                   axis=1).astype(BF16)


def _s5(proj3, bmat, cmat, dvec, consts, glu_w, glu_b):
    bsz, lp, _ = proj3.shape
    assert SUBLANES % bsz == 0
    tb = _pick(lp, 176, 16)
    rows = bsz * tb
    const = lambda shape: pl.BlockSpec(shape, lambda t: (0,) * len(shape))
    return pl.pallas_call(
        functools.partial(_s5_kernel, tb=tb, bsz=bsz),
        grid=(lp // tb,),
        in_specs=[
            pl.BlockSpec((bsz, tb, GROUP_W), lambda t: (0, t, COL_SU)),
            const((2, GROUP_W // 2, 2 * S5_HALF)),
            const((2, 2 * S5_HALF, GROUP_W // 2)),
            const((1, GROUP_W)),
            const((S5_CONST_ROWS, SSM_N)),
            const((GROUP_W, 2 * GROUP_W)),
            const((1, 2 * GROUP_W)),
        ],
        out_specs=pl.BlockSpec((bsz, tb, GROUP_W), lambda t: (0, t, 0)),
        out_shape=jax.ShapeDtypeStruct((bsz, lp, GROUP_W), BF16),
        scratch_shapes=[
            pltpu.VMEM((GROUP_W // LANES, rows, LANES), F32),
            pltpu.VMEM((GROUP_W // LANES, rows, LANES), F32),
            pltpu.VMEM((rows, 2 * SSM_N), F32),
            pltpu.VMEM((SUBLANES, 2 * SSM_N), F32),
        ],
        compiler_params=_params(("arbitrary",)),
        name="s5",
    )(proj3, bmat, cmat, dvec, consts, glu_w, glu_b)


def _mlstm_kernel(q_ref, k_ref, v_ref, o_ref, gcol_ref, grow_ref, cw_ref, ng_ref, y_ref,
                  buf_ref, cn_ref, m_ref):
    bsz = q_ref.shape[0]

    @pl.when(pl.program_id(0) == 0)
    def _():
        buf_ref[:, 0:SUBLANES, :] = jnp.zeros((bsz, SUBLANES, 2 * GROUP_W), F32)
        cn_ref[...] = jnp.zeros_like(cn_ref)
        m_ref[...] = jnp.zeros_like(m_ref)

    @pl.when(pl.program_id(0) > 0)
    def _():
        buf_ref[:, 0:SUBLANES, :] = buf_ref[:, BLOCK:BLOCK + SUBLANES, :]

    t_io = lax.broadcasted_iota(jnp.int32, (BLOCK, BLOCK), 0)
    s_io = lax.broadcasted_iota(jnp.int32, (BLOCK, BLOCK), 1)
    causal = t_io >= s_io
    ones_col = jnp.where(s_io == 0, 1.0, 0.0).astype(BF16)
    nt_dims = (((1,), (1,)), ((), ()))
    tn_dims = (((0,), (0,)), ((), ()))

    for b in range(bsz):
        buf_ref[b, SUBLANES:SUBLANES + BLOCK, 0:GROUP_W] = q_ref[b].astype(F32)
        buf_ref[b, SUBLANES:SUBLANES + BLOCK, GROUP_W:] = k_ref[b].astype(F32)
        conv = cw_ref[ML_CONV - 1:ML_CONV, :] * buf_ref[b, SUBLANES:SUBLANES + BLOCK, :]
        for j in range(1, ML_CONV):
            conv = conv + cw_ref[ML_CONV - 1 - j:ML_CONV - j, :] * buf_ref[b, SUBLANES - j:SUBLANES - j + BLOCK, :]
        qk = conv * jax.nn.sigmoid(conv)
        gcol = gcol_ref[b]
        grow = grow_ref[b, 0]

        for h in range(HEADS):
            idx = b * HEADS + h
            cols = slice(h * DH, (h + 1) * DH)
            qh = qk[:, cols].astype(BF16)
            kh = (qk[:, GROUP_W + h * DH:GROUP_W + (h + 1) * DH] * (DH ** -0.5)).astype(BF16)
            v_ext = jnp.concatenate([v_ref[b, :, cols], ones_col], axis=1)
            li_col = gcol[:, GC_MI + h:GC_MI + h + 1]
            b_col = gcol[:, GC_MF + h:GC_MF + h + 1]
            li_row = grow[GC_MI + h:GC_MI + h + 1, :]
            b_row = grow[GC_MF + h:GC_MF + h + 1, :]
            g = b_row[:, BLOCK - 1:BLOCK]
            cn_prev = cn_ref[idx]
            m_prev = m_ref[idx][:, 0:1]

            d_log = jnp.where(causal, b_col - b_row + li_row, NEG)
            m_inter = b_col + m_prev
            m_comb = jnp.maximum(m_inter, jnp.max(d_log, axis=-1, keepdims=True))
            qk_s = lax.dot_general(qh, kh, nt_dims, preferred_element_type=F32)
            scores = qk_s * jnp.exp(d_log - m_comb)
            inter_w = jnp.exp(m_inter - m_comb)
            tot = (jnp.dot(scores.astype(BF16), v_ext, preferred_element_type=F32)
                   + inter_w * jnp.dot(qh, cn_prev.astype(BF16), preferred_element_type=F32))
            den = tot[:, DH:DH + 1]
            hh = tot[:, 0:DH] / jnp.maximum(jnp.abs(den), jnp.exp(-m_comb))
            hh = hh * lax.rsqrt(jnp.mean(hh * hh, axis=-1, keepdims=True) + EPS) * ng_ref[:, cols]
            y_ref[b, :, cols] = (hh * jax.nn.sigmoid(o_ref[b, :, cols].astype(F32))).astype(BF16)

            a_col = g - b_col + li_col
            m_loc = jnp.max(g - b_row + li_row, axis=-1, keepdims=True)
            w_col = jnp.exp(a_col - m_loc)
            vw_ext = (v_ext.astype(F32) * w_col).astype(BF16)
            cn_loc = lax.dot_general(kh, vw_ext, tn_dims, preferred_element_type=F32)
            m_new = jnp.maximum(g + m_prev, m_loc)
            decay = jnp.exp(g + m_prev - m_new)
            scale = jnp.exp(m_loc - m_new)
            cn_ref[idx] = decay * cn_prev + scale * cn_loc
            m_ref[idx] = jnp.broadcast_to(m_new, (1, LANES))


def _mlstm(proj3, gcol, grow, conv_w, norm_g):
    bsz, lp, _ = proj3.shape
    blk = lambda col: pl.BlockSpec((bsz, BLOCK, GROUP_W), lambda c: (0, c, col))
    return pl.pallas_call(
        _mlstm_kernel,
        grid=(lp // BLOCK,),
        in_specs=[
            blk(COL_MQ), blk(COL_MK), blk(COL_MV), blk(COL_MO),
            pl.BlockSpec((bsz, BLOCK, LANES), lambda c: (0, c, 0)),
            pl.BlockSpec((bsz, 1, 2 * SUBLANES, BLOCK), lambda c: (0, c, 0, 0)),
            pl.BlockSpec((ML_CONV, 2 * GROUP_W), lambda c: (0, 0)),
            pl.BlockSpec((1, GROUP_W), lambda c: (0, 0)),
        ],
        out_specs=pl.BlockSpec((bsz, BLOCK, GROUP_W), lambda c: (0, c, 0)),
        out_shape=jax.ShapeDtypeStruct((bsz, lp, GROUP_W), BF16),
        scratch_shapes=[
            pltpu.VMEM((bsz, SUBLANES + BLOCK, 2 * GROUP_W), F32),
            pltpu.VMEM((bsz * HEADS, DH, 2 * DH), F32),
            pltpu.VMEM((bsz * HEADS, 1, LANES), F32),
        ],
        compiler_params=_params(("arbitrary",)),
        name="mlstm",
    )(proj3, proj3, proj3, proj3, gcol, grow, conv_w, norm_g)


FOX_ROWS = 32
FOX_BIG = 1e30


def _split3(c):
    hi = c.astype(BF16).astype(F32)
    r = c - hi
    mid = r.astype(BF16).astype(F32)
    return hi, mid, r - mid


def _fox_kernel(q_ref, k_ref, v_ref, gcol_ref, y_ref, kx_ref, vx_ref, qx_ref, s_ref, p_ref, acc_ref,
                m_ref, al_ref, *, tq, lp):
    qi = pl.program_id(1)
    nq = lp // tq
    lane = lax.broadcasted_iota(jnp.int32, (tq, LANES), 1)
    nt_dims = (((1,), (1,)), ((), ()))

    @pl.when(qi == 0)
    def _():
        def build(cb, carry):
            r0 = pl.multiple_of(cb * tq, tq)
            rows = pl.ds(r0, tq)
            is_token = (r0 + lax.broadcasted_iota(jnp.int32, (tq, 1), 0)) >= PAD
            g = gcol_ref[rows, :]
            ones_col = jnp.where(lane == 0, 1.0, 0.0).astype(BF16)
            for h in range(HEADS):
                cols = slice(h * DH, (h + 1) * DH)
                hi, mid, lo = _split3(jnp.where(is_token, g[:, GC_FF + h:GC_FF + h + 1], FOX_BIG))
                ext = jnp.where(lane < 3, 1.0, jnp.where(lane == 3, -hi, jnp.where(lane == 4, -mid,
                                jnp.where(lane == 5, -lo, 0.0))))
                kx_ref[h, rows, 0:DH] = k_ref[rows, cols]
                kx_ref[h, rows, DH:] = ext.astype(BF16)
                vx_ref[h, rows, 0:DH] = v_ref[rows, cols]
                vx_ref[h, rows, DH:] = ones_col
            return carry

        lax.fori_loop(0, nq, build, 0)

    gq = gcol_ref[pl.ds(pl.multiple_of(qi * tq, tq), tq), :]
    row_io = lax.broadcasted_iota(jnp.int32, (FOX_ROWS, tq), 0)
    col_io = lax.broadcasted_iota(jnp.int32, (FOX_ROWS, tq), 1)

    for h in range(HEADS):
        hi, mid, lo = _split3(gq[:, GC_FF + h:GC_FF + h + 1])
        ext = jnp.where(lane == 0, hi, jnp.where(lane == 1, mid, jnp.where(lane == 2, lo,
                        jnp.where(lane < 6, 1.0, 0.0))))
        qx_ref[h, :, 0:DH] = q_ref[:, h * DH:(h + 1) * DH]
        qx_ref[h, :, DH:] = ext.astype(BF16)
    m_ref[...] = jnp.full(m_ref.shape, NEG, F32)
    acc_ref[...] = jnp.zeros_like(acc_ref)

    def scores(kb, h):
        krows = pl.ds(pl.multiple_of(kb * tq, tq), tq)
        s_ref[h] = lax.dot_general(qx_ref[h], kx_ref[h, krows, :], nt_dims, preferred_element_type=F32)

    def step(kb, diagonal):
        krows = pl.ds(pl.multiple_of(kb * tq, tq), tq)
        for h in range(HEADS):
            for r0 in range(0, tq, FOX_ROWS):
                rr = slice(r0, r0 + FOX_ROWS)
                sc = s_ref[h, rr, :]
                if diagonal:
                    sc = jnp.where(col_io <= row_io + r0, sc, NEG)
                m_old = m_ref[h, rr, :]
                m_new = jnp.maximum(m_old, jnp.max(sc, axis=-1, keepdims=True))
                al_ref[h, rr, :] = jnp.exp(m_old - m_new)
                m_ref[h, rr, :] = m_new
                p_ref[h, rr, :] = jnp.exp(sc - m_new).astype(BF16)
            if not diagonal:
                scores(kb + 1, h)
            pv = jnp.dot(p_ref[h], vx_ref[h, krows, :], preferred_element_type=F32)
            acc_ref[h] = al_ref[h] * acc_ref[h] + pv

    def off_diagonal(kb, carry):
        step(kb, False)
        return carry

    for h in range(HEADS):
        scores(0, h)
    lax.fori_loop(0, qi, off_diagonal, 0)
    step(qi, True)
    for h in range(HEADS):
        acc = acc_ref[h]
        y_ref[:, h * DH:(h + 1) * DH] = (acc[:, 0:DH] / acc[:, DH:DH + 1]).astype(BF16)


def _fox(proj3, gcol):
    bsz, lp, _ = proj3.shape
    tq = _pick(lp, 384, BLOCK)
    nq = lp // tq
    return pl.pallas_call(
        functools.partial(_fox_kernel, tq=tq, lp=lp),
        grid=(bsz, nq),
        in_specs=[
            pl.BlockSpec((None, tq, GROUP_W), lambda b, i: (b, i, COL_FQ)),
            pl.BlockSpec((None, lp, GROUP_W), lambda b, i: (b, 0, COL_FK)),
            pl.BlockSpec((None, lp, GROUP_W), lambda b, i: (b, 0, COL_FV)),
            pl.BlockSpec((None, lp, LANES), lambda b, i: (b, 0, 0)),
        ],
        out_specs=pl.BlockSpec((tq, GROUP_W), lambda b, i: (b * nq + i, 0)),
        out_shape=jax.ShapeDtypeStruct((bsz * lp, GROUP_W), BF16),
        scratch_shapes=[
            pltpu.VMEM((HEADS, lp, 2 * DH), BF16),
            pltpu.VMEM((HEADS, lp, 2 * DH), BF16),
            pltpu.VMEM((HEADS, tq, 2 * DH), BF16),
            pltpu.VMEM((HEADS, tq, tq), F32),
            pltpu.VMEM((HEADS, tq, tq), BF16),
            pltpu.VMEM((HEADS, tq, 2 * DH), F32),
            pltpu.VMEM((HEADS, tq, 1), F32),
            pltpu.VMEM((HEADS, tq, 1), F32),
        ],
        compiler_params=_params(("arbitrary", "arbitrary")),
        name="fox",
    )(proj3, proj3, proj3, gcol)


def _pool_kernel(u_ref, w_ref, sc_ref, y_ref, buf_ref, *, tb):
    t = pl.program_id(1)

    @pl.when(t == 0)
    def _():
        buf_ref[0:POOL_MAXW, :] = jnp.zeros((POOL_MAXW, GROUP_W), F32)

    @pl.when(t > 0)
    def _():
        buf_ref[0:POOL_MAXW, :] = buf_ref[tb:tb + POOL_MAXW, :]

    buf_ref[POOL_MAXW:POOL_MAXW + tb, :] = u_ref[...].astype(F32)
    pos = t * tb + lax.broadcasted_iota(jnp.int32, (tb, 1), 0) - (PAD - 1)
    posf = jnp.maximum(pos, 1).astype(F32)
    gw = GROUP_W // len(POOL_WINDOWS)
    for gi, w in enumerate(POOL_WINDOWS):
        cols = slice(gi * gw, (gi + 1) * gw)
        x = buf_ref[POOL_MAXW:POOL_MAXW + tb, cols]
        s = x
        for j in range(1, w):
            s = s + buf_ref[POOL_MAXW - j:POOL_MAXW - j + tb, cols]
        pooled = s / jnp.minimum(posf, float(w)) - x
        mixed = jnp.dot(pooled.astype(BF16), w_ref[gi], preferred_element_type=F32)
        y_ref[:, cols] = (mixed * sc_ref[:, cols]).astype(BF16)


def _pool(proj, pool_w, pool_scale, bsz, lp):
    tb = _pick(lp, 528, 16)
    nt = lp // tb
    gw = GROUP_W // len(POOL_WINDOWS)
    return pl.pallas_call(
        functools.partial(_pool_kernel, tb=tb),
        grid=(bsz, nt),
        in_specs=[
            pl.BlockSpec((tb, GROUP_W), lambda b, t: (b * nt + t, COL_PU)),
            pl.BlockSpec((len(POOL_WINDOWS), gw, gw), lambda b, t: (0, 0, 0)),
            pl.BlockSpec((1, GROUP_W), lambda b, t: (0, 0)),
        ],
        out_specs=pl.BlockSpec((tb, GROUP_W), lambda b, t: (b * nt + t, 0)),
        out_shape=jax.ShapeDtypeStruct((bsz * lp, GROUP_W), BF16),
        scratch_shapes=[pltpu.VMEM((POOL_MAXW + tb, GROUP_W), F32)],
        compiler_params=_params(("arbitrary", "arbitrary")),
        name="pool",
    )(proj, pool_w, pool_scale)


def _outproj_kernel(ys_ref, ym_ref, yf_ref, yp_ref, w_ref, g_ref, h_ref, o_ref, *, tm, blocks_per_seq):
    mix = jnp.dot(ys_ref[...], w_ref[0:GROUP_W, :], preferred_element_type=F32)
    for n, y_ref in enumerate((ym_ref, yf_ref, yp_ref), start=1):
        mix = mix + jnp.dot(y_ref[...], w_ref[n * GROUP_W:(n + 1) * GROUP_W, :], preferred_element_type=F32)
    off = (pl.program_id(0) % blocks_per_seq) * tm
    valid = (off + lax.broadcasted_iota(jnp.int32, (tm, 1), 0)) >= PAD
    o_ref[...] = jnp.where(valid, h_ref[...] + _rms(mix, g_ref[...]), 0.0)


def _outproj(ys, ym, yf, yp, w_out, g, h2, lp):
    m = h2.shape[0]
    tm = _pick(lp, 528, 16)
    yspec = pl.BlockSpec((tm, GROUP_W), lambda i: (i, 0))
    return pl.pallas_call(
        functools.partial(_outproj_kernel, tm=tm, blocks_per_seq=lp // tm),
        grid=(m // tm,),
        in_specs=[
            yspec, yspec, yspec, yspec,
            pl.BlockSpec((D_MODEL, D_MODEL), lambda i: (0, 0)),
            pl.BlockSpec((1, D_MODEL), lambda i: (0, 0)),
            pl.BlockSpec((tm, D_MODEL), lambda i: (i, 0)),
        ],
        out_specs=pl.BlockSpec((tm, D_MODEL), lambda i: (i, 0)),
        out_shape=jax.ShapeDtypeStruct((m, D_MODEL), F32),
        input_output_aliases={6: 0},
        compiler_params=_params(("arbitrary",)),
        name="outproj",
    )(ys, ym, yf, yp, w_out, g, h2)


def _ffn_kernel(h_ref, g1_ref, w1_ref, w2_ref, g2_ref, o_ref, hn_ref, acc_ref):
    f = pl.program_id(1)

    @pl.when(f == 0)
    def _():
        hn_ref[...] = _rms(h_ref[...], g1_ref[...]).astype(BF16)
        acc_ref[...] = jnp.zeros_like(acc_ref)

    a = jnp.maximum(jnp.dot(hn_ref[...], w1_ref[...], preferred_element_type=F32), 0.0)
    acc_ref[...] += jnp.dot((a * a).astype(BF16), w2_ref[...], preferred_element_type=F32)

    @pl.when(f == pl.num_programs(1) - 1)
    def _():
        o_ref[...] = h_ref[...] + _rms(acc_ref[...], g2_ref[...])


def _ffn(h2, g1, w1, w2, g2):
    m = h2.shape[0]
    tm = _pick(m, 768, 16)
    tf = 512
    return pl.pallas_call(
        _ffn_kernel,
        grid=(m // tm, D_FF // tf),
        in_specs=[
            pl.BlockSpec((tm, D_MODEL), lambda i, f: (i, 0)),
            pl.BlockSpec((1, D_MODEL), lambda i, f: (0, 0)),
            pl.BlockSpec((D_MODEL, tf), lambda i, f: (0, f)),
            pl.BlockSpec((tf, D_MODEL), lambda i, f: (f, 0)),
            pl.BlockSpec((1, D_MODEL), lambda i, f: (0, 0)),
        ],
        out_specs=pl.BlockSpec((tm, D_MODEL), lambda i, f: (i, 0)),
        out_shape=jax.ShapeDtypeStruct((m, D_MODEL), F32),
        scratch_shapes=[pltpu.VMEM((tm, D_MODEL), BF16), pltpu.VMEM((tm, D_MODEL), F32)],
        input_output_aliases={0: 0},
        compiler_params=_params(("arbitrary", "arbitrary")),
        name="ffn",
    )(h2, g1, w1, w2, g2)


IN_COLS = N_MAIN + 3 * HEADS
IN_COLS_PAD = -(-IN_COLS // LANES) * LANES
O_MI = 5 * GROUP_W
O_FQ = O_MI + 2 * HEADS
O_FF = O_FQ + 3 * GROUP_W
O_PU = O_FF + HEADS


def _split_w_in_kernel(w_ref, main_ref, gate_ref):
    rows = w_ref.shape[0]
    lane = lax.broadcasted_iota(jnp.int32, (rows, LANES), 1)
    main_ref[:, 0:O_MI] = w_ref[:, 0:O_MI].astype(BF16)

    def shifted_copy(dst0, src0, width):
        shift = src0 % LANES
        base = src0 - shift
        for c in range(0, width, LANES):
            lo = w_ref[:, base + c:base + c + LANES]
            hi = w_ref[:, base + c + LANES:base + c + 2 * LANES]
            mixed = jnp.where(lane >= shift, lo, hi)
            main_ref[:, dst0 + c:dst0 + c + LANES] = pltpu.roll(mixed, LANES - shift, axis=1).astype(BF16)

    shifted_copy(O_MI, O_FQ, 3 * GROUP_W)
    shifted_copy(O_MI + 3 * GROUP_W, O_PU, GROUP_W)
    g_ml = w_ref[:, O_MI:O_MI + LANES]
    ff0 = O_FF - O_FF % LANES
    g_fx = w_ref[:, ff0:ff0 + LANES]
    assert O_FF - ff0 == GC_FF
    gate = jnp.where(lane < GC_FF, g_ml, jnp.where(lane < GC_FF + HEADS, g_fx, 0.0))
    gate_ref[...] = gate.astype(BF16)


def _split_w_in(w_all, l):
    tr = 256
    return pl.pallas_call(
        _split_w_in_kernel,
        grid=(D_MODEL // tr,),
        in_specs=[pl.BlockSpec((None, tr, IN_COLS_PAD), lambda i: (l, i, 0))],
        out_specs=[pl.BlockSpec((tr, N_MAIN), lambda i: (i, 0)), pl.BlockSpec((tr, LANES), lambda i: (i, 0))],
        out_shape=[jax.ShapeDtypeStruct((D_MODEL, N_MAIN), BF16), jax.ShapeDtypeStruct((D_MODEL, LANES), BF16)],
        compiler_params=_params(("arbitrary",)),
        name="split_w_in",
    )(w_all)


def _s5_matrices(bbre, bbim, c_re, c_im):
    gh = SSM_G // 2
    mask = (jnp.arange(GROUP_W // 2)[:, None] // SSM_H == jnp.arange(S5_HALF)[None, :] // SSM_P).astype(F32)
    c_t = lambda c: c.transpose(0, 2, 1).reshape(SSM_N, SSM_H)
    bmats, cmats = [], []
    for k in range(2):
        st = slice(S5_HALF * k, S5_HALF * (k + 1))
        expand = lambda bb: jnp.tile(bb[:, st], (gh, 1)) * mask
        cont = lambda c: jnp.tile(c_t(c)[st, :], (1, gh)) * mask.T
        bmats.append(jnp.concatenate([expand(bbre), expand(bbim)], axis=1))
        cmats.append(jnp.concatenate([cont(c_re), -cont(c_im)], axis=0))
    return jnp.stack(bmats).astype(BF16), jnp.stack(cmats).astype(BF16)


def kernel(x, meta_tokens, g_pre_mix, g_post_mix, g_pre_ffn, g_post_ffn, w_in, ml_gate_bias, fx_gate_bias, ssm_lam_re, ssm_lam_im, ssm_log_dt, ssm_b_re, ssm_b_im, ssm_c_re, ssm_c_im, ssm_d, ssm_glu_w, ssm_glu_b, ml_conv_w, ml_norm_g, pool_w, pool_scale, w_out, mlp_w1, mlp_w2):
    bsz, seq, d = x.shape
    depth = w_in.shape[0]
    lp = PAD + N_META + seq
    m = bsz * lp
    assert d == D_MODEL and lp % BLOCK == 0
    meta = jnp.broadcast_to(meta_tokens[None].astype(x.dtype), (bsz, N_META, d))
    h = jnp.concatenate([jnp.zeros((bsz, PAD, d), x.dtype), meta, x], axis=1).reshape(m, d)
    row = lambda a: a.reshape(1, -1).astype(F32)
    colscale = jnp.ones((1, N_MAIN), F32).at[:, COL_FQ * GROUP_W:(COL_FQ + 1) * GROUP_W].set(DH ** -0.5)

    for l in range(depth):
        w_main, w_gate = _split_w_in(w_in, l)
        proj, graw = _inproj(h, row(g_pre_mix[l]), w_main, w_gate, colscale)
        proj3 = proj.reshape(bsz, lp, N_MAIN)

        gbias = jnp.pad(jnp.concatenate([ml_gate_bias[l], fx_gate_bias[l]]).astype(F32), (0, LANES - 3 * HEADS))
        gcol, grow = _gates(graw, gbias.reshape(1, LANES), bsz, lp)

        consts, bbre, bbim = _s5_prep(ssm_lam_re[l], ssm_lam_im[l], ssm_log_dt[l], ssm_b_re[l], ssm_b_im[l], bsz)
        bmat, cmat = _s5_matrices(bbre, bbim, ssm_c_re[l], ssm_c_im[l])
        y_ssm = _s5(proj3, bmat, cmat, row(ssm_d[l]), consts, ssm_glu_w[l].astype(BF16), row(ssm_glu_b[l]))
        y_ml = _mlstm(proj3, gcol, grow, ml_conv_w[l].astype(F32), row(ml_norm_g[l]))
        y_fx = _fox(proj3, gcol)
        y_pool = _pool(proj, pool_w[l].astype(BF16), row(pool_scale[l]), bsz, lp)

        h = _outproj(y_ssm.reshape(m, GROUP_W), y_ml.reshape(m, GROUP_W), y_fx, y_pool,
                     w_out[l].astype(BF16), row(g_post_mix[l]), h, lp)
        h = _ffn(h, row(g_pre_ffn[l]), mlp_w1[l].astype(BF16), mlp_w2[l].astype(BF16), row(g_post_ffn[l]))

    return h.reshape(bsz, lp, d)[:, PAD + N_META:]
```

```python
import functools
import math

import jax
import jax.numpy as jnp
from jax import lax
from jax.experimental import pallas as pl
from jax.experimental.pallas import tpu as pltpu

F32 = jnp.float32
BF16 = jnp.bfloat16

D_MODEL = 2048
N_META = 16
BLOCK = 128
PAD = BLOCK - N_META
GROUP_W = 512
SSM_H = 16
SSM_G = GROUP_W // SSM_H
SSM_P = 64
SSM_N = SSM_G * SSM_P
HEADS = 4
DH = GROUP_W // HEADS
ML_CONV = 4
POOL_WINDOWS = (2, 4, 8, 16)
POOL_MAXW = 16
D_FF = 4 * D_MODEL
EPS = 1e-6
NEG = -1e30
N_MAIN = 9 * GROUP_W
LANES = 128
SUBLANES = 8
VMEM_LIMIT = 60 * 1024 * 1024

COL_SU, COL_MQ, COL_MK, COL_MV, COL_MO, COL_FQ, COL_FK, COL_FV, COL_PU = range(9)
GC_MI, GC_MF, GC_FF, GC_RM = 0, HEADS, 2 * HEADS, 3 * HEADS


def _pick(n, target, mult):
    best = None
    for d in range(mult, min(n, target) + 1, mult):
        if n % d == 0:
            best = d
    assert best is not None, (n, target, mult)
    return best


def _params(sem):
    return pltpu.CompilerParams(dimension_semantics=sem, vmem_limit_bytes=VMEM_LIMIT)


def _rms(x, g):
    return x * lax.rsqrt(jnp.mean(x * x, axis=-1, keepdims=True) + EPS) * g


def _inproj_kernel(x_ref, g_ref, w_ref, wg_ref, cs_ref, o_ref, og_ref, xn_ref):
    @pl.when(pl.program_id(1) == 0)
    def _():
        xn = _rms(x_ref[...], g_ref[...]).astype(BF16)
        xn_ref[...] = xn
        og_ref[...] = jnp.dot(xn, wg_ref[...], preferred_element_type=F32)

    acc = jnp.dot(xn_ref[...], w_ref[...], preferred_element_type=F32)
    o_ref[...] = (acc * cs_ref[...]).astype(BF16)


def _inproj(h2, g, w_main, w_gate, colscale):
    m = h2.shape[0]
    tm = _pick(m, 1056, 16)
    tn = 1536
    return pl.pallas_call(
        _inproj_kernel,
        grid=(m // tm, N_MAIN // tn),
        in_specs=[
            pl.BlockSpec((tm, D_MODEL), lambda i, j: (i, 0)),
            pl.BlockSpec((1, D_MODEL), lambda i, j: (0, 0)),
            pl.BlockSpec((D_MODEL, tn), lambda i, j: (0, j)),
            pl.BlockSpec((D_MODEL, LANES), lambda i, j: (0, 0)),
            pl.BlockSpec((1, tn), lambda i, j: (0, j)),
        ],
        out_specs=[
            pl.BlockSpec((tm, tn), lambda i, j: (i, j)),
            pl.BlockSpec((tm, LANES), lambda i, j: (i, 0)),
        ],
        out_shape=[jax.ShapeDtypeStruct((m, N_MAIN), BF16), jax.ShapeDtypeStruct((m, LANES), F32)],
        scratch_shapes=[pltpu.VMEM((tm, D_MODEL), BF16)],
        compiler_params=_params(("arbitrary", "arbitrary")),
        name="inproj",
    )(h2, g, w_main, w_gate, colscale)


def _gates_kernel(raw_ref, bias_ref, col_ref, row_ref, carry_ref):
    c = pl.program_id(0)

    @pl.when(c == 0)
    def _():
        carry_ref[...] = jnp.zeros_like(carry_ref)

    r_io = lax.broadcasted_iota(jnp.int32, (BLOCK, LANES), 0)
    c_io = lax.broadcasted_iota(jnp.int32, (BLOCK, LANES), 1)
    valid = (r_io + c * BLOCK) >= PAD
    tri = (r_io >= c_io).astype(F32)
    for b in range(raw_ref.shape[0]):
        g = raw_ref[b] + bias_ref[...]
        logsig = jnp.minimum(g, 0.0) - jnp.log(1.0 + jnp.exp(-jnp.abs(g)))
        x = jnp.where(valid, logsig, 0.0)
        x = jnp.where(c_io >= GC_MF, jnp.where(c_io < GC_FF + HEADS, x, 0.0), 0.0)
        cum = jnp.dot(tri, x, precision=lax.Precision.HIGHEST, preferred_element_type=F32)
        cum = cum + jnp.where(c_io >= GC_FF, carry_ref[b], 0.0)
        carry_ref[b] = cum[BLOCK - 1:BLOCK, :]
        log_i = jnp.where(valid, g, NEG)
        r = log_i - pltpu.roll(cum, LANES - GC_MF, axis=1)
        rmax = r
        d = 1
        while d < BLOCK:
            rmax = jnp.maximum(rmax, jnp.where(r_io >= d, pltpu.roll(rmax, d, axis=0), NEG))
            d *= 2
        out = jnp.where(c_io < GC_MF, r, jnp.where(c_io < GC_RM, cum, jnp.where(
            c_io < GC_RM + HEADS, pltpu.roll(rmax, GC_RM, axis=1), 0.0)))
        col_ref[b] = out
        row_ref[b, 0] = out.T[0:2 * SUBLANES, :]


def _gates(raw, bias, bsz, lp):
    nc = lp // BLOCK
    return pl.pallas_call(
        _gates_kernel,
        grid=(nc,),
        in_specs=[
            pl.BlockSpec((bsz, BLOCK, LANES), lambda c: (0, c, 0)),
            pl.BlockSpec((1, LANES), lambda c: (0, 0)),
        ],
        out_specs=[
            pl.BlockSpec((bsz, BLOCK, LANES), lambda c: (0, c, 0)),
            pl.BlockSpec((bsz, 1, 2 * SUBLANES, BLOCK), lambda c: (0, c, 0, 0)),
        ],
        out_shape=[
            jax.ShapeDtypeStruct((bsz, lp, LANES), F32),
            jax.ShapeDtypeStruct((bsz, nc, 2 * SUBLANES, BLOCK), F32),
        ],
        scratch_shapes=[pltpu.VMEM((bsz, 1, LANES), F32)],
        compiler_params=_params(("arbitrary",)),
        name="gates",
    )(raw.reshape(bsz, lp, LANES), bias)


S5_CONST_ROWS = 64
S5_STRIPS = SSM_N // LANES
S5_STRIPS_PER_LOOP = 4
S5_HALF = SSM_N // 2


def _s5_hs_steps(bsz):
    return [d for d in (1, 2, 4) if d < SUBLANES // bsz]


def _s5_prep_kernel(lre_ref, lim_ref, ldt_ref, bre_ref, bim_ref, consts_ref, bbre_ref, bbim_ref, *, bsz):
    lre = lre_ref[...]
    lim = lim_ref[...]
    dt = jnp.exp(ldt_ref[...])
    r_io = lax.broadcasted_iota(jnp.int32, (SUBLANES, SSM_N), 0)
    kk = (r_io + 1).astype(F32)
    mag = jnp.exp(kk * (lre * dt))
    ang = kk * (lim * dt)
    pw_re = mag * jnp.cos(ang)
    pw_im = mag * jnp.sin(ang)
    consts_ref[...] = jnp.zeros_like(consts_ref)
    steps = _s5_hs_steps(bsz)
    for i, d in enumerate(steps):
        keep = r_io >= d * bsz
        consts_ref[16 * i:16 * i + 8, :] = jnp.where(keep, pw_re[d - 1:d, :], 0.0)
        consts_ref[16 * i + 8:16 * i + 16, :] = jnp.where(keep, pw_im[d - 1:d, :], 0.0)
    t_in_vreg = lax.shift_right_logical(r_io, bsz.bit_length() - 1)
    cw_re = jnp.zeros((SUBLANES, SSM_N), F32)
    cw_im = jnp.zeros((SUBLANES, SSM_N), F32)
    for j in range(SUBLANES // bsz):
        cw_re = jnp.where(t_in_vreg == j, pw_re[j:j + 1, :], cw_re)
        cw_im = jnp.where(t_in_vreg == j, pw_im[j:j + 1, :], cw_im)
    base = 16 * len(steps)
    consts_ref[base:base + 8, :] = cw_re
    consts_ref[base + 8:base + 16, :] = cw_im
    nr = pw_re[0:1, :] - 1.0
    ni = pw_im[0:1, :]
    den = lre * lre + lim * lim
    cr = (nr * lre + ni * lim) / den
    ci = (ni * lre - nr * lim) / den
    bre = bre_ref[...]
    bim = bim_ref[...]
    bbre_ref[...] = cr * bre - ci * bim
    bbim_ref[...] = cr * bim + ci * bre


def _s5_prep(lam_re, lam_im, log_dt, b_re, b_im, bsz):
    flat = lambda a: a.reshape(1, SSM_N)
    ldt = jnp.repeat(log_dt, SSM_P).reshape(1, SSM_N)
    bt = lambda a: a.transpose(2, 0, 1).reshape(SSM_H, SSM_N)
    return pl.pallas_call(
        functools.partial(_s5_prep_kernel, bsz=bsz),
        out_shape=[
            jax.ShapeDtypeStruct((S5_CONST_ROWS, SSM_N), F32),
            jax.ShapeDtypeStruct((SSM_H, SSM_N), F32),
            jax.ShapeDtypeStruct((SSM_H, SSM_N), F32),
        ],
        name="s5_prep",
    )(flat(lam_re), flat(lam_im), ldt, bt(b_re), bt(b_im))


def _s5_kernel(u_ref, bmat_ref, cmat_ref, d_ref, consts_ref, gw_ref, gb_ref, y_ref,
               uil_ref, oil_ref, x_ref, carry_ref, *, tb, bsz):
    rows = bsz * tb
    n_us = GROUP_W // LANES

    @pl.when(pl.program_id(0) == 0)
    def _():
        carry_ref[...] = jnp.zeros_like(carry_ref)

    for b in range(bsz):
        ub = u_ref[b].astype(F32)
        for s in range(n_us):
            uil_ref[s, pl.ds(b, tb, stride=bsz), :] = ub[:, s * LANES:(s + 1) * LANES]
    u = jnp.concatenate([uil_ref[s] for s in range(n_us)], axis=1)
    u16 = u.astype(BF16)
    half_w = GROUP_W // 2
    for k in range(2):
        x_ref[:, 2 * S5_HALF * k:2 * S5_HALF * (k + 1)] = jnp.dot(
            u16[:, half_w * k:half_w * (k + 1)], bmat_ref[k], preferred_element_type=F32)

    def cmul_add(xr, xi, ar, ai, sr, si):
        return xr + ar * sr - ai * si, xi + ar * si + ai * sr

    row8 = lax.broadcasted_iota(jnp.int32, (SUBLANES, LANES), 0)
    steps = _s5_hs_steps(bsz)
    cbase = 16 * len(steps)

    def last_step_tile(c):
        span = SUBLANES
        while span > bsz:
            half = span // 2
            c = jnp.where((row8 & (span - 1)) < half, pltpu.roll(c, half, axis=0), c)
            span = half
        return c

    def lanes_of(s):
        k, j = divmod(s, S5_STRIPS // 2)
        re0 = 2 * S5_HALF * k + j * LANES
        return slice(re0, re0 + LANES), slice(re0 + S5_HALF, re0 + S5_HALF + LANES), slice(s * LANES, (s + 1) * LANES)

    for s0 in range(0, S5_STRIPS, S5_STRIPS_PER_LOOP):
        strips = range(s0, s0 + S5_STRIPS_PER_LOOP)

        def body(r, carry, strips=strips):
            r8 = pl.ds(pl.multiple_of(r * SUBLANES, SUBLANES), SUBLANES)
            out = []
            for n, s in enumerate(strips):
                lr, li, lc = lanes_of(s)
                xr = x_ref[r8, lr]
                xi = x_ref[r8, li]
                for i, d in enumerate(steps):
                    xr, xi = cmul_add(xr, xi, consts_ref[16 * i:16 * i + 8, lc], consts_ref[16 * i + 8:16 * i + 16, lc],
                                      pltpu.roll(xr, d * bsz, axis=0), pltpu.roll(xi, d * bsz, axis=0))
                xr, xi = cmul_add(xr, xi, consts_ref[cbase:cbase + 8, lc], consts_ref[cbase + 8:cbase + 16, lc],
                                  carry[2 * n], carry[2 * n + 1])
                x_ref[r8, lr] = xr
                x_ref[r8, li] = xi
                out.append(last_step_tile(xr))
                out.append(last_step_tile(xi))
            return tuple(out)

        init = []
        for s in strips:
            lr, li, _ = lanes_of(s)
            init += [carry_ref[:, lr], carry_ref[:, li]]
        fin = lax.fori_loop(0, rows // SUBLANES, body, tuple(init))
        for n, s in enumerate(strips):
            lr, li, _ = lanes_of(s)
            carry_ref[:, lr] = fin[2 * n]
            carry_ref[:, li] = fin[2 * n + 1]

    y = jnp.concatenate(
        [jnp.dot(x_ref[:, 2 * S5_HALF * k:2 * S5_HALF * (k + 1)].astype(BF16), cmat_ref[k],
                 preferred_element_type=F32) for k in range(2)], axis=1)
    y = jax.nn.gelu(y + d_ref[...] * u, approximate=True)
    z = jnp.dot(y.astype(BF16), gw_ref[...], preferred_element_type=F32) + gb_ref[...]
    out = z[:, :GROUP_W] * jax.nn.sigmoid(z[:, GROUP_W:])
    for s in range(n_us):
        oil_ref[s] = out[:, s * LANES:(s + 1) * LANES]
    for b in range(bsz):
        y_ref[b] = jnp.concatenate([oil_ref[s, pl.ds(b, tb, stride=bsz), :] for s in range(n_us)],
                                   axis=1).astype(BF16)


def _s5(proj3, bmat, cmat, dvec, consts, glu_w, glu_b):
    bsz, lp, _ = proj3.shape
    assert SUBLANES % bsz == 0
    tb = _pick(lp, 176, 16)
    rows = bsz * tb
    const = lambda shape: pl.BlockSpec(shape, lambda t: (0,) * len(shape))
    return pl.pallas_call(
        functools.partial(_s5_kernel, tb=tb, bsz=bsz),
        grid=(lp // tb,),
        in_specs=[
            pl.BlockSpec((bsz, tb, GROUP_W), lambda t: (0, t, COL_SU)),
            const((2, GROUP_W // 2, 2 * S5_HALF)),
            const((2, 2 * S5_HALF, GROUP_W // 2)),
            const((1, GROUP_W)),
            const((S5_CONST_ROWS, SSM_N)),
            const((GROUP_W, 2 * GROUP_W)),
            const((1, 2 * GROUP_W)),
        ],
        out_specs=pl.BlockSpec((bsz, tb, GROUP_W), lambda t: (0, t, 0)),
        out_shape=jax.ShapeDtypeStruct((bsz, lp, GROUP_W), BF16),
        scratch_shapes=[
            pltpu.VMEM((GROUP_W // LANES, rows, LANES), F32),
            pltpu.VMEM((GROUP_W // LANES, rows, LANES), F32),
            pltpu.VMEM((rows, 2 * SSM_N), F32),
            pltpu.VMEM((SUBLANES, 2 * SSM_N), F32),
        ],
        compiler_params=_params(("arbitrary",)),
        name="s5",
    )(proj3, bmat, cmat, dvec, consts, glu_w, glu_b)


def _mlstm_kernel(q_ref, k_ref, v_ref, o_ref, gcol_ref, grow_ref, cw_ref, ng_ref, y_ref,
                  buf_ref, cn_ref, m_ref):
    bsz = q_ref.shape[0]

    @pl.when(pl.program_id(0) == 0)
    def _():
        buf_ref[:, 0:SUBLANES, :] = jnp.zeros((bsz, SUBLANES, 2 * GROUP_W), F32)
        cn_ref[...] = jnp.zeros_like(cn_ref)
        m_ref[...] = jnp.zeros_like(m_ref)

    @pl.when(pl.program_id(0) > 0)
    def _():
        buf_ref[:, 0:SUBLANES, :] = buf_ref[:, BLOCK:BLOCK + SUBLANES, :]

    s_io = lax.broadcasted_iota(jnp.int32, (BLOCK, BLOCK), 0)
    t_io = lax.broadcasted_iota(jnp.int32, (BLOCK, BLOCK), 1)
    causal = s_io <= t_io
    ones_row = jnp.where(s_io == 0, 1.0, 0.0)
    nt_dims = (((1,), (1,)), ((), ()))

    for b in range(bsz):
        buf_ref[b, SUBLANES:SUBLANES + BLOCK, 0:GROUP_W] = q_ref[b].astype(F32)
        buf_ref[b, SUBLANES:SUBLANES + BLOCK, GROUP_W:] = k_ref[b].astype(F32)
        conv = cw_ref[ML_CONV - 1:ML_CONV, :] * buf_ref[b, SUBLANES:SUBLANES + BLOCK, :]
        for j in range(1, ML_CONV):
            conv = conv + cw_ref[ML_CONV - 1 - j:ML_CONV - j, :] * buf_ref[b, SUBLANES - j:SUBLANES - j + BLOCK, :]
        qk = conv * jax.nn.sigmoid(conv)
        gcol = gcol_ref[b]
        grow = grow_ref[b, 0]

        for h in range(HEADS):
            idx = b * HEADS + h
            cols = slice(h * DH, (h + 1) * DH)
            qh = qk[:, cols].astype(BF16)
            kh = (qk[:, GROUP_W + h * DH:GROUP_W + (h + 1) * DH] * (DH ** -0.5)).astype(BF16)
            v_t = jnp.concatenate([v_ref[b, :, cols].astype(F32).T, ones_row], axis=0)
            r_col = gcol[:, GC_MI + h:GC_MI + h + 1]
            r_row = grow[GC_MI + h:GC_MI + h + 1, :]
            b_row = grow[GC_MF + h:GC_MF + h + 1, :]
            rm_row = grow[GC_RM + h:GC_RM + h + 1, :]
            g = b_row[:, BLOCK - 1:BLOCK]
            r_max = rm_row[:, BLOCK - 1:BLOCK]
            cn_prev = cn_ref[idx]
            m_prev = m_ref[idx][:, 0:1]

            big_m = jnp.maximum(m_prev, rm_row)
            decay_t = jnp.where(causal, jnp.exp(r_col - big_m), 0.0)
            qk_t = lax.dot_general(kh, qh, nt_dims, preferred_element_type=F32)
            p_t = (qk_t * decay_t).astype(BF16)
            inter_w = jnp.exp(m_prev - big_m)
            tot = (jnp.dot(v_t.astype(BF16), p_t, preferred_element_type=F32)
                   + inter_w * lax.dot_general(cn_prev.astype(BF16), qh, nt_dims, preferred_element_type=F32))
            num = tot[0:DH, :]
            inv = 1.0 / jnp.maximum(jnp.abs(tot[DH:DH + 1, :]), jnp.exp(-(b_row + big_m)))
            scale = inv * lax.rsqrt(inv * inv * jnp.mean(num * num, axis=0, keepdims=True) + EPS)
            hn = (num * scale).T * ng_ref[:, cols]
            y_ref[b, :, cols] = (hn * jax.nn.sigmoid(o_ref[b, :, cols].astype(F32))).astype(BF16)

            vw_t = (v_t * jnp.exp(r_row - r_max)).astype(BF16)
            cn_loc = jnp.dot(vw_t, kh, preferred_element_type=F32)
            m_keep = jnp.maximum(m_prev, r_max)
            cn_ref[idx] = jnp.exp(m_prev - m_keep) * cn_prev + jnp.exp(r_max - m_keep) * cn_loc
            m_ref[idx] = jnp.broadcast_to(g + m_keep, (1, LANES))


def _mlstm(proj3, gcol, grow, conv_w, norm_g):
    bsz, lp, _ = proj3.shape
    blk = lambda col: pl.BlockSpec((bsz, BLOCK, GROUP_W), lambda c: (0, c, col))
    return pl.pallas_call(
        _mlstm_kernel,
        grid=(lp // BLOCK,),
        in_specs=[
            blk(COL_MQ), blk(COL_MK), blk(COL_MV), blk(COL_MO),
            pl.BlockSpec((bsz, BLOCK, LANES), lambda c: (0, c, 0)),
            pl.BlockSpec((bsz, 1, 2 * SUBLANES, BLOCK), lambda c: (0, c, 0, 0)),
            pl.BlockSpec((ML_CONV, 2 * GROUP_W), lambda c: (0, 0)),
            pl.BlockSpec((1, GROUP_W), lambda c: (0, 0)),
        ],
        out_specs=pl.BlockSpec((bsz, BLOCK, GROUP_W), lambda c: (0, c, 0)),
        out_shape=jax.ShapeDtypeStruct((bsz, lp, GROUP_W), BF16),
        scratch_shapes=[
            pltpu.VMEM((bsz, SUBLANES + BLOCK, 2 * GROUP_W), F32),
            pltpu.VMEM((bsz * HEADS, 2 * DH, DH), F32),
            pltpu.VMEM((bsz * HEADS, 1, LANES), F32),
        ],
        compiler_params=_params(("arbitrary",)),
        name="mlstm",
    )(proj3, proj3, proj3, proj3, gcol, grow, conv_w, norm_g)


FOX_ROWS = 32
FOX_BIG = 1e30


def _split3(c):
    hi = c.astype(BF16).astype(F32)
    r = c - hi
    mid = r.astype(BF16).astype(F32)
    return hi, mid, r - mid


def _fox_kernel(q_ref, k_ref, v_ref, gcol_ref, y_ref, kx_ref, vx_ref, qx_ref, s_ref, p_ref, acc_ref,
                m_ref, al_ref, *, tq, lp):
    qi = pl.program_id(1)
    nq = lp // tq
    lane = lax.broadcasted_iota(jnp.int32, (tq, LANES), 1)
    nt_dims = (((1,), (1,)), ((), ()))

    @pl.when(qi == 0)
    def _():
        def build(cb, carry):
            r0 = pl.multiple_of(cb * tq, tq)
            rows = pl.ds(r0, tq)
            is_token = (r0 + lax.broadcasted_iota(jnp.int32, (tq, 1), 0)) >= PAD
            g = gcol_ref[rows, :]
            ones_col = jnp.where(lane == 0, 1.0, 0.0).astype(BF16)
            for h in range(HEADS):
                cols = slice(h * DH, (h + 1) * DH)
                hi, mid, lo = _split3(jnp.where(is_token, g[:, GC_FF + h:GC_FF + h + 1], FOX_BIG))
                ext = jnp.where(lane < 3, 1.0, jnp.where(lane == 3, -hi, jnp.where(lane == 4, -mid,
                                jnp.where(lane == 5, -lo, 0.0))))
                kx_ref[h, rows, 0:DH] = k_ref[rows, cols]
                kx_ref[h, rows, DH:] = ext.astype(BF16)
                vx_ref[h, rows, 0:DH] = v_ref[rows, cols]
                vx_ref[h, rows, DH:] = ones_col
            return carry

        lax.fori_loop(0, nq, build, 0)

    gq = gcol_ref[pl.ds(pl.multiple_of(qi * tq, tq), tq), :]
    row_io = lax.broadcasted_iota(jnp.int32, (FOX_ROWS, tq), 0)
    col_io = lax.broadcasted_iota(jnp.int32, (FOX_ROWS, tq), 1)

    for h in range(HEADS):
        hi, mid, lo = _split3(gq[:, GC_FF + h:GC_FF + h + 1])
        ext = jnp.where(lane == 0, hi, jnp.where(lane == 1, mid, jnp.where(lane == 2, lo,
                        jnp.where(lane < 6, 1.0, 0.0))))
        qx_ref[h, :, 0:DH] = q_ref[:, h * DH:(h + 1) * DH]
        qx_ref[h, :, DH:] = ext.astype(BF16)
    m_ref[...] = jnp.full(m_ref.shape, NEG, F32)
    acc_ref[...] = jnp.zeros_like(acc_ref)

    def scores(kb, h):
        krows = pl.ds(pl.multiple_of(kb * tq, tq), tq)
        s_ref[h] = lax.dot_general(qx_ref[h], kx_ref[h, krows, :], nt_dims, preferred_element_type=F32)

    def step(kb, diagonal):
        krows = pl.ds(pl.multiple_of(kb * tq, tq), tq)
        for h in range(HEADS):
            for r0 in range(0, tq, FOX_ROWS):
                rr = slice(r0, r0 + FOX_ROWS)
                sc = s_ref[h, rr, :]
                if diagonal:
                    sc = jnp.where(col_io <= row_io + r0, sc, NEG)
                m_old = m_ref[h, rr, :]
                m_new = jnp.maximum(m_old, jnp.max(sc, axis=-1, keepdims=True))
                al_ref[h, rr, :] = jnp.exp(m_old - m_new)
                m_ref[h, rr, :] = m_new
                p_ref[h, rr, :] = jnp.exp(sc - m_new).astype(BF16)
            if not diagonal:
                scores(kb + 1, h)
            pv = jnp.dot(p_ref[h], vx_ref[h, krows, :], preferred_element_type=F32)
            acc_ref[h] = al_ref[h] * acc_ref[h] + pv

    def off_diagonal(kb, carry):
        step(kb, False)
        return carry

    for h in range(HEADS):
        scores(0, h)
    lax.fori_loop(0, qi, off_diagonal, 0)
    step(qi, True)
    for h in range(HEADS):
        acc = acc_ref[h]
        y_ref[:, h * DH:(h + 1) * DH] = (acc[:, 0:DH] / acc[:, DH:DH + 1]).astype(BF16)


def _fox(proj3, gcol):
    bsz, lp, _ = proj3.shape
    tq = _pick(lp, 384, BLOCK)
    nq = lp // tq
    return pl.pallas_call(
        functools.partial(_fox_kernel, tq=tq, lp=lp),
        grid=(bsz, nq),
        in_specs=[
            pl.BlockSpec((None, tq, GROUP_W), lambda b, i: (b, i, COL_FQ)),
            pl.BlockSpec((None, lp, GROUP_W), lambda b, i: (b, 0, COL_FK)),
            pl.BlockSpec((None, lp, GROUP_W), lambda b, i: (b, 0, COL_FV)),
            pl.BlockSpec((None, lp, LANES), lambda b, i: (b, 0, 0)),
        ],
        out_specs=pl.BlockSpec((tq, GROUP_W), lambda b, i: (b * nq + i, 0)),
        out_shape=jax.ShapeDtypeStruct((bsz * lp, GROUP_W), BF16),
        scratch_shapes=[
            pltpu.VMEM((HEADS, lp, 2 * DH), BF16),
            pltpu.VMEM((HEADS, lp, 2 * DH), BF16),
            pltpu.VMEM((HEADS, tq, 2 * DH), BF16),
            pltpu.VMEM((HEADS, tq, tq), F32),
            pltpu.VMEM((HEADS, tq, tq), BF16),
            pltpu.VMEM((HEADS, tq, 2 * DH), F32),
            pltpu.VMEM((HEADS, tq, 1), F32),
            pltpu.VMEM((HEADS, tq, 1), F32),
        ],
        compiler_params=_params(("arbitrary", "arbitrary")),
        name="fox",
    )(proj3, proj3, proj3, gcol)


def _pool_kernel(u_ref, w_ref, sc_ref, y_ref, buf_ref, *, tb):
    t = pl.program_id(1)

    @pl.when(t == 0)
    def _():
        buf_ref[0:POOL_MAXW, :] = jnp.zeros((POOL_MAXW, GROUP_W), F32)

    @pl.when(t > 0)
    def _():
        buf_ref[0:POOL_MAXW, :] = buf_ref[tb:tb + POOL_MAXW, :]

    buf_ref[POOL_MAXW:POOL_MAXW + tb, :] = u_ref[...].astype(F32)
    pos = t * tb + lax.broadcasted_iota(jnp.int32, (tb, 1), 0) - (PAD - 1)
    posf = jnp.maximum(pos, 1).astype(F32)
    gw = GROUP_W // len(POOL_WINDOWS)
    for gi, w in enumerate(POOL_WINDOWS):
        cols = slice(gi * gw, (gi + 1) * gw)
        x = buf_ref[POOL_MAXW:POOL_MAXW + tb, cols]
        s = x
        for j in range(1, w):
            s = s + buf_ref[POOL_MAXW - j:POOL_MAXW - j + tb, cols]
        pooled = s / jnp.minimum(posf, float(w)) - x
        mixed = jnp.dot(pooled.astype(BF16), w_ref[gi], preferred_element_type=F32)
        y_ref[:, cols] = (mixed * sc_ref[:, cols]).astype(BF16)


def _pool(proj, pool_w, pool_scale, bsz, lp):
    tb = _pick(lp, 528, 16)
    nt = lp // tb
    gw = GROUP_W // len(POOL_WINDOWS)
    return pl.pallas_call(
        functools.partial(_pool_kernel, tb=tb),
        grid=(bsz, nt),
        in_specs=[
            pl.BlockSpec((tb, GROUP_W), lambda b, t: (b * nt + t, COL_PU)),
            pl.BlockSpec((len(POOL_WINDOWS), gw, gw), lambda b, t: (0, 0, 0)),
            pl.BlockSpec((1, GROUP_W), lambda b, t: (0, 0)),
        ],
        out_specs=pl.BlockSpec((tb, GROUP_W), lambda b, t: (b * nt + t, 0)),
        out_shape=jax.ShapeDtypeStruct((bsz * lp, GROUP_W), BF16),
        scratch_shapes=[pltpu.VMEM((POOL_MAXW + tb, GROUP_W), F32)],
        compiler_params=_params(("arbitrary", "arbitrary")),
        name="pool",
    )(proj, pool_w, pool_scale)


def _outproj_kernel(ys_ref, ym_ref, yf_ref, yp_ref, w_ref, g_ref, h_ref, o_ref, *, tm, blocks_per_seq):
    mix = jnp.dot(ys_ref[...], w_ref[0:GROUP_W, :], preferred_element_type=F32)
    for n, y_ref in enumerate((ym_ref, yf_ref, yp_ref), start=1):
        mix = mix + jnp.dot(y_ref[...], w_ref[n * GROUP_W:(n + 1) * GROUP_W, :], preferred_element_type=F32)
    off = (pl.program_id(0) % blocks_per_seq) * tm
    valid = (off + lax.broadcasted_iota(jnp.int32, (tm, 1), 0)) >= PAD
    o_ref[...] = jnp.where(valid, h_ref[...] + _rms(mix, g_ref[...]), 0.0)


def _outproj(ys, ym, yf, yp, w_out_all, g, h2, lp, l):
    m = h2.shape[0]
    tm = _pick(lp, 528, 16)
    yspec = pl.BlockSpec((tm, GROUP_W), lambda i: (i, 0))
    return pl.pallas_call(
        functools.partial(_outproj_kernel, tm=tm, blocks_per_seq=lp // tm),
        grid=(m // tm,),
        in_specs=[
            yspec, yspec, yspec, yspec,
            pl.BlockSpec((None, D_MODEL, D_MODEL), lambda i: (l, 0, 0)),
            pl.BlockSpec((1, D_MODEL), lambda i: (0, 0)),
            pl.BlockSpec((tm, D_MODEL), lambda i: (i, 0)),
        ],
        out_specs=pl.BlockSpec((tm, D_MODEL), lambda i: (i, 0)),
        out_shape=jax.ShapeDtypeStruct((m, D_MODEL), F32),
        input_output_aliases={6: 0},
        compiler_params=_params(("arbitrary",)),
        name="outproj",
    )(ys, ym, yf, yp, w_out_all, g, h2)


def _ffn_kernel(h_ref, g1_ref, w1_ref, w2_ref, g2_ref, o_ref, hn_ref, acc_ref):
    f = pl.program_id(1)

    @pl.when(f == 0)
    def _():
        hn_ref[...] = _rms(h_ref[...], g1_ref[...]).astype(BF16)
        acc_ref[...] = jnp.zeros_like(acc_ref)

    a = jnp.maximum(jnp.dot(hn_ref[...], w1_ref[...], preferred_element_type=F32), 0.0)
    acc_ref[...] += jnp.dot((a * a).astype(BF16), w2_ref[...], preferred_element_type=F32)

    @pl.when(f == pl.num_programs(1) - 1)
    def _():
        o_ref[...] = h_ref[...] + _rms(acc_ref[...], g2_ref[...])


FFN_TF = 1024


def _ffn_specs(l, tm, h_spec):
    return [
        h_spec,
        pl.BlockSpec((1, D_MODEL), lambda i, f: (0, 0)),
        pl.BlockSpec((None, D_MODEL, FFN_TF), lambda i, f: (l, 0, f)),
        pl.BlockSpec((None, FFN_TF, D_MODEL), lambda i, f: (l, f, 0)),
        pl.BlockSpec((1, D_MODEL), lambda i, f: (0, 0)),
    ], [pltpu.VMEM((tm, D_MODEL), BF16), pltpu.VMEM((tm, D_MODEL), F32)]


def _ffn(h2, g1, w1_all, w2_all, g2, l):
    m = h2.shape[0]
    tm = _pick(m, 768, 16)
    in_specs, scratch = _ffn_specs(l, tm, pl.BlockSpec((tm, D_MODEL), lambda i, f: (i, 0)))
    return pl.pallas_call(
        _ffn_kernel,
        grid=(m // tm, D_FF // FFN_TF),
        in_specs=in_specs,
        out_specs=pl.BlockSpec((tm, D_MODEL), lambda i, f: (i, 0)),
        out_shape=jax.ShapeDtypeStruct((m, D_MODEL), F32),
        scratch_shapes=scratch,
        input_output_aliases={0: 0},
        compiler_params=_params(("arbitrary", "arbitrary")),
        name="ffn",
    )(h2, g1, w1_all, w2_all, g2)


def _ffn_final(h2, g1, w1_all, w2_all, g2, l, bsz, lp):
    seq = lp - PAD - N_META
    tm = _pick(seq, 512, 16)
    per_seq = seq // tm
    h_spec = pl.BlockSpec((pl.Element(tm), pl.Element(D_MODEL)),
                          lambda i, f: (pl.multiple_of((i // per_seq) * lp + PAD + N_META + (i % per_seq) * tm,
                                                       SUBLANES), 0))
    in_specs, scratch = _ffn_specs(l, tm, h_spec)
    return pl.pallas_call(
        _ffn_kernel,
        grid=(bsz * per_seq, D_FF // FFN_TF),
        in_specs=in_specs,
        out_specs=pl.BlockSpec((tm, D_MODEL), lambda i, f: (i, 0)),
        out_shape=jax.ShapeDtypeStruct((bsz * seq, D_MODEL), F32),
        scratch_shapes=scratch,
        compiler_params=_params(("arbitrary", "arbitrary")),
        name="ffn_final",
    )(h2, g1, w1_all, w2_all, g2)


IN_COLS = N_MAIN + 3 * HEADS
IN_COLS_PAD = -(-IN_COLS // LANES) * LANES
O_MI = 5 * GROUP_W
O_FQ = O_MI + 2 * HEADS
O_FF = O_FQ + 3 * GROUP_W
O_PU = O_FF + HEADS


def _split_w_in_kernel(w_ref, main_ref, gate_ref):
    rows = w_ref.shape[0]
    lane = lax.broadcasted_iota(jnp.int32, (rows, LANES), 1)
    main_ref[:, 0:O_MI] = w_ref[:, 0:O_MI].astype(BF16)

    def shifted_copy(dst0, src0, width):
        shift = src0 % LANES
        base = src0 - shift
        for c in range(0, width, LANES):
            lo = w_ref[:, base + c:base + c + LANES]
            hi = w_ref[:, base + c + LANES:base + c + 2 * LANES]
            mixed = jnp.where(lane >= shift, lo, hi)
            main_ref[:, dst0 + c:dst0 + c + LANES] = pltpu.roll(mixed, LANES - shift, axis=1).astype(BF16)

    shifted_copy(O_MI, O_FQ, 3 * GROUP_W)
    shifted_copy(O_MI + 3 * GROUP_W, O_PU, GROUP_W)
    g_ml = w_ref[:, O_MI:O_MI + LANES]
    ff0 = O_FF - O_FF % LANES
    g_fx = w_ref[:, ff0:ff0 + LANES]
    assert O_FF - ff0 == GC_FF
    gate = jnp.where(lane < GC_FF, g_ml, jnp.where(lane < GC_FF + HEADS, g_fx, 0.0))
    gate_ref[...] = gate.astype(BF16)


def _split_w_in(w_all, l):
    tr = 256
    return pl.pallas_call(
        _split_w_in_kernel,
        grid=(D_MODEL // tr,),
        in_specs=[pl.BlockSpec((None, tr, IN_COLS_PAD), lambda i: (l, i, 0))],
        out_specs=[pl.BlockSpec((tr, N_MAIN), lambda i: (i, 0)), pl.BlockSpec((tr, LANES), lambda i: (i, 0))],
        out_shape=[jax.ShapeDtypeStruct((D_MODEL, N_MAIN), BF16), jax.ShapeDtypeStruct((D_MODEL, LANES), BF16)],
        compiler_params=_params(("arbitrary",)),
        name="split_w_in",
    )(w_all)


def _s5_matrices(bbre, bbim, c_re, c_im):
    gh = SSM_G // 2
    mask = (jnp.arange(GROUP_W // 2)[:, None] // SSM_H == jnp.arange(S5_HALF)[None, :] // SSM_P).astype(F32)
    c_t = lambda c: c.transpose(0, 2, 1).reshape(SSM_N, SSM_H)
    bmats, cmats = [], []
    for k in range(2):
        st = slice(S5_HALF * k, S5_HALF * (k + 1))
        expand = lambda bb: jnp.tile(bb[:, st], (gh, 1)) * mask
        cont = lambda c: jnp.tile(c_t(c)[st, :], (1, gh)) * mask.T
        bmats.append(jnp.concatenate([expand(bbre), expand(bbim)], axis=1))
        cmats.append(jnp.concatenate([cont(c_re), -cont(c_im)], axis=0))
    return jnp.stack(bmats).astype(BF16), jnp.stack(cmats).astype(BF16)


def kernel(x, meta_tokens, g_pre_mix, g_post_mix, g_pre_ffn, g_post_ffn, w_in, ml_gate_bias, fx_gate_bias, ssm_lam_re, ssm_lam_im, ssm_log_dt, ssm_b_re, ssm_b_im, ssm_c_re, ssm_c_im, ssm_d, ssm_glu_w, ssm_glu_b, ml_conv_w, ml_norm_g, pool_w, pool_scale, w_out, mlp_w1, mlp_w2):
    bsz, seq, d = x.shape
    depth = w_in.shape[0]
    lp = PAD + N_META + seq
    m = bsz * lp
    assert d == D_MODEL and lp % BLOCK == 0
    meta = jnp.broadcast_to(meta_tokens[None].astype(x.dtype), (bsz, N_META, d))
    h = jnp.concatenate([jnp.zeros((bsz, PAD, d), x.dtype), meta, x], axis=1).reshape(m, d)
    row = lambda a: a.reshape(1, -1).astype(F32)
    colscale = jnp.ones((1, N_MAIN), F32).at[:, COL_FQ * GROUP_W:(COL_FQ + 1) * GROUP_W].set(DH ** -0.5)
    w_out16, w1_16, w2_16 = w_out.astype(BF16), mlp_w1.astype(BF16), mlp_w2.astype(BF16)

    for l in range(depth):
        w_main, w_gate = _split_w_in(w_in, l)
        proj, graw = _inproj(h, row(g_pre_mix[l]), w_main, w_gate, colscale)
        proj3 = proj.reshape(bsz, lp, N_MAIN)

        gbias = jnp.pad(jnp.concatenate([ml_gate_bias[l], fx_gate_bias[l]]).astype(F32), (0, LANES - 3 * HEADS))
        gcol, grow = _gates(graw, gbias.reshape(1, LANES), bsz, lp)

        consts, bbre, bbim = _s5_prep(ssm_lam_re[l], ssm_lam_im[l], ssm_log_dt[l], ssm_b_re[l], ssm_b_im[l], bsz)
        bmat, cmat = _s5_matrices(bbre, bbim, ssm_c_re[l], ssm_c_im[l])
        y_ssm = _s5(proj3, bmat, cmat, row(ssm_d[l]), consts, ssm_glu_w[l].astype(BF16), row(ssm_glu_b[l]))
        y_ml = _mlstm(proj3, gcol, grow, ml_conv_w[l].astype(F32), row(ml_norm_g[l]))
        y_fx = _fox(proj3, gcol)
        y_pool = _pool(proj, pool_w[l].astype(BF16), row(pool_scale[l]), bsz, lp)

        h = _outproj(y_ssm.reshape(m, GROUP_W), y_ml.reshape(m, GROUP_W), y_fx, y_pool,
                     w_out16, row(g_post_mix[l]), h, lp, l)
        if l + 1 < depth:
            h = _ffn(h, row(g_pre_ffn[l]), w1_16, w2_16, row(g_post_ffn[l]), l)
        else:
            h = _ffn_final(h, row(g_pre_ffn[l]), w1_16, w2_16, row(g_post_ffn[l]), l, bsz, lp)

    return h.reshape(bsz, seq, d)
```

```python
import functools
import math

import jax
import jax.numpy as jnp
from jax import lax
from jax.experimental import pallas as pl
from jax.experimental.pallas import tpu as pltpu

F32 = jnp.float32
BF16 = jnp.bfloat16

D_MODEL = 2048
N_META = 16
BLOCK = 128
PAD = BLOCK - N_META
GROUP_W = 512
SSM_H = 16
SSM_G = GROUP_W // SSM_H
SSM_P = 64
SSM_N = SSM_G * SSM_P
HEADS = 4
DH = GROUP_W // HEADS
ML_CONV = 4
POOL_WINDOWS = (2, 4, 8, 16)
POOL_MAXW = 16
D_FF = 4 * D_MODEL
EPS = 1e-6
NEG = -1e30
N_MAIN = 9 * GROUP_W
LANES = 128
SUBLANES = 8
VMEM_LIMIT = 60 * 1024 * 1024

COL_SU, COL_MQ, COL_MK, COL_MV, COL_MO, COL_FQ, COL_FK, COL_FV, COL_PU = range(9)
GC_MI, GC_MF, GC_FF, GC_RM = 0, HEADS, 2 * HEADS, 3 * HEADS


def _pick(n, target, mult):
    best = None
    for d in range(mult, min(n, target) + 1, mult):
        if n % d == 0:
            best = d
    assert best is not None, (n, target, mult)
    return best


def _params(sem):
    return pltpu.CompilerParams(dimension_semantics=sem, vmem_limit_bytes=VMEM_LIMIT)


def _rms(x, g):
    return x * lax.rsqrt(jnp.mean(x * x, axis=-1, keepdims=True) + EPS) * g


def _inproj_kernel(x_ref, g_ref, w_ref, wg_ref, cs_ref, o_ref, og_ref, xn_ref):
    @pl.when(pl.program_id(1) == 0)
    def _():
        xn = _rms(x_ref[...], g_ref[...]).astype(BF16)
        xn_ref[...] = xn
        og_ref[...] = jnp.dot(xn, wg_ref[...], preferred_element_type=F32)

    acc = jnp.dot(xn_ref[...], w_ref[...], preferred_element_type=F32)
    o_ref[...] = (acc * cs_ref[...]).astype(BF16)


def _inproj(h2, g, w_main, w_gate, colscale):
    m = h2.shape[0]
    tm = _pick(m, 1056, 16)
    tn = 1536
    return pl.pallas_call(
        _inproj_kernel,
        grid=(m // tm, N_MAIN // tn),
        in_specs=[
            pl.BlockSpec((tm, D_MODEL), lambda i, j: (i, 0)),
            pl.BlockSpec((1, D_MODEL), lambda i, j: (0, 0)),
            pl.BlockSpec((D_MODEL, tn), lambda i, j: (0, j)),
            pl.BlockSpec((D_MODEL, LANES), lambda i, j: (0, 0)),
            pl.BlockSpec((1, tn), lambda i, j: (0, j)),
        ],
        out_specs=[
            pl.BlockSpec((tm, tn), lambda i, j: (i, j)),
            pl.BlockSpec((tm, LANES), lambda i, j: (i, 0)),
        ],
        out_shape=[jax.ShapeDtypeStruct((m, N_MAIN), BF16), jax.ShapeDtypeStruct((m, LANES), F32)],
        scratch_shapes=[pltpu.VMEM((tm, D_MODEL), BF16)],
        compiler_params=_params(("arbitrary", "arbitrary")),
        name="inproj",
    )(h2, g, w_main, w_gate, colscale)


def _inproj_first_kernel(x_ref, meta_ref, g_ref, w_ref, wg_ref, cs_ref, o_ref, og_ref, h_ref, xn_ref,
                         *, tm, tiles_per_seq):
    head = PAD + N_META
    first_n = pl.program_id(1) == 0
    seq_start = pl.program_id(0) % tiles_per_seq == 0

    @pl.when(first_n & seq_start)
    def _():
        h_ref[0:PAD, :] = jnp.zeros((PAD, D_MODEL), F32)
        h_ref[PAD:head, :] = meta_ref[...]
        h_ref[head:tm, :] = x_ref[0:tm - head, :]

    @pl.when(first_n & jnp.logical_not(seq_start))
    def _():
        h_ref[...] = x_ref[...]

    @pl.when(first_n)
    def _():
        xn = _rms(h_ref[...], g_ref[...]).astype(BF16)
        xn_ref[...] = xn
        og_ref[...] = jnp.dot(xn, wg_ref[...], preferred_element_type=F32)

    acc = jnp.dot(xn_ref[...], w_ref[...], preferred_element_type=F32)
    o_ref[...] = (acc * cs_ref[...]).astype(BF16)


def _inproj_first(x2, meta, g, w_main, w_gate, colscale, bsz, lp):
    seq = lp - PAD - N_META
    tm = _pick(lp, min(528, lp // 2), 16)
    assert tm > PAD + N_META
    per_seq = lp // tm
    tn = 1536
    m = bsz * lp

    def x_index(i, j):
        b, t = i // per_seq, i % per_seq
        return pl.multiple_of(b * seq + jnp.maximum(t * tm - (PAD + N_META), 0), SUBLANES), 0

    return pl.pallas_call(
        functools.partial(_inproj_first_kernel, tm=tm, tiles_per_seq=per_seq),
        grid=(m // tm, N_MAIN // tn),
        in_specs=[
            pl.BlockSpec((pl.Element(tm), pl.Element(D_MODEL)), x_index),
            pl.BlockSpec((N_META, D_MODEL), lambda i, j: (0, 0)),
            pl.BlockSpec((1, D_MODEL), lambda i, j: (0, 0)),
            pl.BlockSpec((D_MODEL, tn), lambda i, j: (0, j)),
            pl.BlockSpec((D_MODEL, LANES), lambda i, j: (0, 0)),
            pl.BlockSpec((1, tn), lambda i, j: (0, j)),
        ],
        out_specs=[
            pl.BlockSpec((tm, tn), lambda i, j: (i, j)),
            pl.BlockSpec((tm, LANES), lambda i, j: (i, 0)),
            pl.BlockSpec((tm, D_MODEL), lambda i, j: (i, 0)),
        ],
        out_shape=[jax.ShapeDtypeStruct((m, N_MAIN), BF16), jax.ShapeDtypeStruct((m, LANES), F32),
                   jax.ShapeDtypeStruct((m, D_MODEL), F32)],
        scratch_shapes=[pltpu.VMEM((tm, D_MODEL), BF16)],
        compiler_params=_params(("arbitrary", "arbitrary")),
        name="inproj_first",
    )(x2, meta, g, w_main, w_gate, colscale)


def _gates_kernel(raw_ref, bias_ref, col_ref, row_ref, carry_ref):
    c = pl.program_id(0)

    @pl.when(c == 0)
    def _():
        carry_ref[...] = jnp.zeros_like(carry_ref)

    r_io = lax.broadcasted_iota(jnp.int32, (BLOCK, LANES), 0)
    c_io = lax.broadcasted_iota(jnp.int32, (BLOCK, LANES), 1)
    valid = (r_io + c * BLOCK) >= PAD
    tri = (r_io >= c_io).astype(F32)
    for b in range(raw_ref.shape[0]):
        g = raw_ref[b] + bias_ref[...]
        logsig = jnp.minimum(g, 0.0) - jnp.log(1.0 + jnp.exp(-jnp.abs(g)))
        x = jnp.where(valid, logsig, 0.0)
        x = jnp.where(c_io >= GC_MF, jnp.where(c_io < GC_FF + HEADS, x, 0.0), 0.0)
        cum = jnp.dot(tri, x, precision=lax.Precision.HIGHEST, preferred_element_type=F32)
        cum = cum + jnp.where(c_io >= GC_FF, carry_ref[b], 0.0)
        carry_ref[b] = cum[BLOCK - 1:BLOCK, :]
        log_i = jnp.where(valid, g, NEG)
        r = log_i - pltpu.roll(cum, LANES - GC_MF, axis=1)
        rmax = r
        d = 1
        while d < BLOCK:
            rmax = jnp.maximum(rmax, jnp.where(r_io >= d, pltpu.roll(rmax, d, axis=0), NEG))
            d *= 2
        out = jnp.where(c_io < GC_MF, r, jnp.where(c_io < GC_RM, cum, jnp.where(
            c_io < GC_RM + HEADS, pltpu.roll(rmax, GC_RM, axis=1), 0.0)))
        col_ref[b] = out
        row_ref[b, 0] = out.T[0:2 * SUBLANES, :]


def _gates(raw, bias, bsz, lp):
    nc = lp // BLOCK
    return pl.pallas_call(
        _gates_kernel,
        grid=(nc,),
        in_specs=[
            pl.BlockSpec((bsz, BLOCK, LANES), lambda c: (0, c, 0)),
            pl.BlockSpec((1, LANES), lambda c: (0, 0)),
        ],
        out_specs=[
            pl.BlockSpec((bsz, BLOCK, LANES), lambda c: (0, c, 0)),
            pl.BlockSpec((bsz, 1, 2 * SUBLANES, BLOCK), lambda c: (0, c, 0, 0)),
        ],
        out_shape=[
            jax.ShapeDtypeStruct((bsz, lp, LANES), F32),
            jax.ShapeDtypeStruct((bsz, nc, 2 * SUBLANES, BLOCK), F32),
        ],
        scratch_shapes=[pltpu.VMEM((bsz, 1, LANES), F32)],
        compiler_params=_params(("arbitrary",)),
        name="gates",
    )(raw.reshape(bsz, lp, LANES), bias)


S5_CONST_ROWS = 64
S5_STRIPS = SSM_N // LANES
S5_STRIPS_PER_LOOP = 4
S5_HALF = SSM_N // 2


def _s5_hs_steps(bsz):
    return [d for d in (1, 2, 4) if d < SUBLANES // bsz]


def _s5_prep_kernel(lre_ref, lim_ref, ldt_ref, bre_ref, bim_ref, consts_ref, bbre_ref, bbim_ref, *, bsz):
    lre = lre_ref[...]
    lim = lim_ref[...]
    dt = jnp.exp(ldt_ref[...])
    r_io = lax.broadcasted_iota(jnp.int32, (SUBLANES, SSM_N), 0)
    kk = (r_io + 1).astype(F32)
    mag = jnp.exp(kk * (lre * dt))
    ang = kk * (lim * dt)
    pw_re = mag * jnp.cos(ang)
    pw_im = mag * jnp.sin(ang)
    consts_ref[...] = jnp.zeros_like(consts_ref)
    steps = _s5_hs_steps(bsz)
    for i, d in enumerate(steps):
        keep = r_io >= d * bsz
        consts_ref[16 * i:16 * i + 8, :] = jnp.where(keep, pw_re[d - 1:d, :], 0.0)
        consts_ref[16 * i + 8:16 * i + 16, :] = jnp.where(keep, pw_im[d - 1:d, :], 0.0)
    t_in_vreg = lax.shift_right_logical(r_io, bsz.bit_length() - 1)
    cw_re = jnp.zeros((SUBLANES, SSM_N), F32)
    cw_im = jnp.zeros((SUBLANES, SSM_N), F32)
    for j in range(SUBLANES // bsz):
        cw_re = jnp.where(t_in_vreg == j, pw_re[j:j + 1, :], cw_re)
        cw_im = jnp.where(t_in_vreg == j, pw_im[j:j + 1, :], cw_im)
    base = 16 * len(steps)
    consts_ref[base:base + 8, :] = cw_re
    consts_ref[base + 8:base + 16, :] = cw_im
    nr = pw_re[0:1, :] - 1.0
    ni = pw_im[0:1, :]
    den = lre * lre + lim * lim
    cr = (nr * lre + ni * lim) / den
    ci = (ni * lre - nr * lim) / den
    bre = bre_ref[...]
    bim = bim_ref[...]
    bbre_ref[...] = cr * bre - ci * bim
    bbim_ref[...] = cr * bim + ci * bre


def _s5_prep(lam_re, lam_im, log_dt, b_re, b_im, bsz):
    flat = lambda a: a.reshape(1, SSM_N)
    ldt = jnp.repeat(log_dt, SSM_P).reshape(1, SSM_N)
    bt = lambda a: a.transpose(2, 0, 1).reshape(SSM_H, SSM_N)
    return pl.pallas_call(
        functools.partial(_s5_prep_kernel, bsz=bsz),
        out_shape=[
            jax.ShapeDtypeStruct((S5_CONST_ROWS, SSM_N), F32),
            jax.ShapeDtypeStruct((SSM_H, SSM_N), F32),
            jax.ShapeDtypeStruct((SSM_H, SSM_N), F32),
        ],
        name="s5_prep",
    )(flat(lam_re), flat(lam_im), ldt, bt(b_re), bt(b_im))


def _s5_kernel(u_ref, bmat_ref, cmat_ref, d_ref, consts_ref, gw_ref, gb_ref, y_ref,
               uil_ref, oil_ref, x_ref, carry_ref, *, tb, bsz):
    rows = bsz * tb
    n_us = GROUP_W // LANES

    @pl.when(pl.program_id(0) == 0)
    def _():
        carry_ref[...] = jnp.zeros_like(carry_ref)

    for b in range(bsz):
        ub = u_ref[b].astype(F32)
        for s in range(n_us):
            uil_ref[s, pl.ds(b, tb, stride=bsz), :] = ub[:, s * LANES:(s + 1) * LANES]
    u = jnp.concatenate([uil_ref[s] for s in range(n_us)], axis=1)
    u16 = u.astype(BF16)
    half_w = GROUP_W // 2
    for k in range(2):
        x_ref[:, 2 * S5_HALF * k:2 * S5_HALF * (k + 1)] = jnp.dot(
            u16[:, half_w * k:half_w * (k + 1)], bmat_ref[k], preferred_element_type=F32)

    def cmul_add(xr, xi, ar, ai, sr, si):
        return xr + ar * sr - ai * si, xi + ar * si + ai * sr

    row8 = lax.broadcasted_iota(jnp.int32, (SUBLANES, LANES), 0)
    steps = _s5_hs_steps(bsz)
    cbase = 16 * len(steps)

    def last_step_tile(c):
        span = SUBLANES
        while span > bsz:
            half = span // 2
            c = jnp.where((row8 & (span - 1)) < half, pltpu.roll(c, half, axis=0), c)
            span = half
        return c

    def lanes_of(s):
        k, j = divmod(s, S5_STRIPS // 2)
        re0 = 2 * S5_HALF * k + j * LANES
        return slice(re0, re0 + LANES), slice(re0 + S5_HALF, re0 + S5_HALF + LANES), slice(s * LANES, (s + 1) * LANES)

    for s0 in range(0, S5_STRIPS, S5_STRIPS_PER_LOOP):
        strips = range(s0, s0 + S5_STRIPS_PER_LOOP)

        def body(r, carry, strips=strips):
            r8 = pl.ds(pl.multiple_of(r * SUBLANES, SUBLANES), SUBLANES)
            out = []
            for n, s in enumerate(strips):
                lr, li, lc = lanes_of(s)
                xr = x_ref[r8, lr]
                xi = x_ref[r8, li]
                for i, d in enumerate(steps):
                    xr, xi = cmul_add(xr, xi, consts_ref[16 * i:16 * i + 8, lc], consts_ref[16 * i + 8:16 * i + 16, lc],
                                      pltpu.roll(xr, d * bsz, axis=0), pltpu.roll(xi, d * bsz, axis=0))
                xr, xi = cmul_add(xr, xi, consts_ref[cbase:cbase + 8, lc], consts_ref[cbase + 8:cbase + 16, lc],
                                  carry[2 * n], carry[2 * n + 1])
                x_ref[r8, lr] = xr
                x_ref[r8, li] = xi
                out.append(last_step_tile(xr))
                out.append(last_step_tile(xi))
            return tuple(out)

        init = []
        for s in strips:
            lr, li, _ = lanes_of(s)
            init += [carry_ref[:, lr], carry_ref[:, li]]
        fin = lax.fori_loop(0, rows // SUBLANES, body, tuple(init))
        for n, s in enumerate(strips):
            lr, li, _ = lanes_of(s)
            carry_ref[:, lr] = fin[2 * n]
            carry_ref[:, li] = fin[2 * n + 1]

    y = jnp.concatenate(
        [jnp.dot(x_ref[:, 2 * S5_HALF * k:2 * S5_HALF * (k + 1)].astype(BF16), cmat_ref[k],
                 preferred_element_type=F32) for k in range(2)], axis=1)
    y = jax.nn.gelu(y + d_ref[...] * u, approximate=True)
    z = jnp.dot(y.astype(BF16), gw_ref[...], preferred_element_type=F32) + gb_ref[...]
    out = z[:, :GROUP_W] * jax.nn.sigmoid(z[:, GROUP_W:])
    for s in range(n_us):
        oil_ref[s] = out[:, s * LANES:(s + 1) * LANES]
    for b in range(bsz):
        y_ref[b] = jnp.concatenate([oil_ref[s, pl.ds(b, tb, stride=bsz), :] for s in range(n_us)],
                                   axis=1).astype(BF16)


def _s5(proj3, bmat, cmat, dvec, consts, glu_w, glu_b):
    bsz, lp, _ = proj3.shape
    assert SUBLANES % bsz == 0
    tb = _pick(lp, 176, 16)
    rows = bsz * tb
    const = lambda shape: pl.BlockSpec(shape, lambda t: (0,) * len(shape))
    return pl.pallas_call(
        functools.partial(_s5_kernel, tb=tb, bsz=bsz),
        grid=(lp // tb,),
        in_specs=[
            pl.BlockSpec((bsz, tb, GROUP_W), lambda t: (0, t, COL_SU)),
            const((2, GROUP_W // 2, 2 * S5_HALF)),
            const((2, 2 * S5_HALF, GROUP_W // 2)),
            const((1, GROUP_W)),
            const((S5_CONST_ROWS, SSM_N)),
            const((GROUP_W, 2 * GROUP_W)),
            const((1, 2 * GROUP_W)),
        ],
        out_specs=pl.BlockSpec((bsz, tb, GROUP_W), lambda t: (0, t, 0)),
        out_shape=jax.ShapeDtypeStruct((bsz, lp, GROUP_W), BF16),
        scratch_shapes=[
            pltpu.VMEM((GROUP_W // LANES, rows, LANES), F32),
            pltpu.VMEM((GROUP_W // LANES, rows, LANES), F32),
            pltpu.VMEM((rows, 2 * SSM_N), F32),
            pltpu.VMEM((SUBLANES, 2 * SSM_N), F32),
        ],
        compiler_params=_params(("arbitrary",)),
        name="s5",
    )(proj3, bmat, cmat, dvec, consts, glu_w, glu_b)


def _mlstm_kernel(q_ref, k_ref, v_ref, o_ref, gcol_ref, grow_ref, cw_ref, ng_ref, y_ref,
                  buf_ref, cn_ref, m_ref):
    bsz = q_ref.shape[0]

    @pl.when(pl.program_id(0) == 0)
    def _():
        buf_ref[:, 0:SUBLANES, :] = jnp.zeros((bsz, SUBLANES, 2 * GROUP_W), F32)
        cn_ref[...] = jnp.zeros_like(cn_ref)
        m_ref[...] = jnp.zeros_like(m_ref)

    @pl.when(pl.program_id(0) > 0)
    def _():
        buf_ref[:, 0:SUBLANES, :] = buf_ref[:, BLOCK:BLOCK + SUBLANES, :]

    s_io = lax.broadcasted_iota(jnp.int32, (BLOCK, BLOCK), 0)
    t_io = lax.broadcasted_iota(jnp.int32, (BLOCK, BLOCK), 1)
    causal = s_io <= t_io
    ones_row = jnp.where(s_io == 0, 1.0, 0.0)
    nt_dims = (((1,), (1,)), ((), ()))

    for b in range(bsz):
        buf_ref[b, SUBLANES:SUBLANES + BLOCK, 0:GROUP_W] = q_ref[b].astype(F32)
        buf_ref[b, SUBLANES:SUBLANES + BLOCK, GROUP_W:] = k_ref[b].astype(F32)
        conv = cw_ref[ML_CONV - 1:ML_CONV, :] * buf_ref[b, SUBLANES:SUBLANES + BLOCK, :]
        for j in range(1, ML_CONV):
            conv = conv + cw_ref[ML_CONV - 1 - j:ML_CONV - j, :] * buf_ref[b, SUBLANES - j:SUBLANES - j + BLOCK, :]
        qk = conv * jax.nn.sigmoid(conv)
        gcol = gcol_ref[b]
        grow = grow_ref[b, 0]

        for h in range(HEADS):
            idx = b * HEADS + h
            cols = slice(h * DH, (h + 1) * DH)
            qh = qk[:, cols].astype(BF16)
            kh = (qk[:, GROUP_W + h * DH:GROUP_W + (h + 1) * DH] * (DH ** -0.5)).astype(BF16)
            v_t = jnp.concatenate([v_ref[b, :, cols].astype(F32).T, ones_row], axis=0)
            r_col = gcol[:, GC_MI + h:GC_MI + h + 1]
            r_row = grow[GC_MI + h:GC_MI + h + 1, :]
            b_row = grow[GC_MF + h:GC_MF + h + 1, :]
            rm_row = grow[GC_RM + h:GC_RM + h + 1, :]
            g = b_row[:, BLOCK - 1:BLOCK]
            r_max = rm_row[:, BLOCK - 1:BLOCK]
            cn_prev = cn_ref[idx]
            m_prev = m_ref[idx][:, 0:1]

            big_m = jnp.maximum(m_prev, rm_row)
            decay_t = jnp.where(causal, jnp.exp(r_col - big_m), 0.0)
            qk_t = lax.dot_general(kh, qh, nt_dims, preferred_element_type=F32)
            p_t = (qk_t * decay_t).astype(BF16)
            inter_w = jnp.exp(m_prev - big_m)
            tot = (jnp.dot(v_t.astype(BF16), p_t, preferred_element_type=F32)
                   + inter_w * lax.dot_general(cn_prev.astype(BF16), qh, nt_dims, preferred_element_type=F32))
            num = tot[0:DH, :]
            inv = 1.0 / jnp.maximum(jnp.abs(tot[DH:DH + 1, :]), jnp.exp(-(b_row + big_m)))
            scale = inv * lax.rsqrt(inv * inv * jnp.mean(num * num, axis=0, keepdims=True) + EPS)
            hn = (num * scale).T * ng_ref[:, cols]
            y_ref[b, :, cols] = (hn * jax.nn.sigmoid(o_ref[b, :, cols].astype(F32))).astype(BF16)

            vw_t = (v_t * jnp.exp(r_row - r_max)).astype(BF16)
            cn_loc = jnp.dot(vw_t, kh, preferred_element_type=F32)
            m_keep = jnp.maximum(m_prev, r_max)
            cn_ref[idx] = jnp.exp(m_prev - m_keep) * cn_prev + jnp.exp(r_max - m_keep) * cn_loc
            m_ref[idx] = jnp.broadcast_to(g + m_keep, (1, LANES))


def _mlstm(proj3, gcol, grow, conv_w, norm_g):
    bsz, lp, _ = proj3.shape
    blk = lambda col: pl.BlockSpec((bsz, BLOCK, GROUP_W), lambda c: (0, c, col))
    return pl.pallas_call(
        _mlstm_kernel,
        grid=(lp // BLOCK,),
        in_specs=[
            blk(COL_MQ), blk(COL_MK), blk(COL_MV), blk(COL_MO),
            pl.BlockSpec((bsz, BLOCK, LANES), lambda c: (0, c, 0)),
            pl.BlockSpec((bsz, 1, 2 * SUBLANES, BLOCK), lambda c: (0, c, 0, 0)),
            pl.BlockSpec((ML_CONV, 2 * GROUP_W), lambda c: (0, 0)),
            pl.BlockSpec((1, GROUP_W), lambda c: (0, 0)),
        ],
        out_specs=pl.BlockSpec((bsz, BLOCK, GROUP_W), lambda c: (0, c, 0)),
        out_shape=jax.ShapeDtypeStruct((bsz, lp, GROUP_W), BF16),
        scratch_shapes=[
            pltpu.VMEM((bsz, SUBLANES + BLOCK, 2 * GROUP_W), F32),
            pltpu.VMEM((bsz * HEADS, 2 * DH, DH), F32),
            pltpu.VMEM((bsz * HEADS, 1, LANES), F32),
        ],
        compiler_params=_params(("arbitrary",)),
        name="mlstm",
    )(proj3, proj3, proj3, proj3, gcol, grow, conv_w, norm_g)


FOX_ROWS = 32
FOX_BIG = 1e30


def _split3(c):
    hi = c.astype(BF16).astype(F32)
    r = c - hi
    mid = r.astype(BF16).astype(F32)
    return hi, mid, r - mid


def _fox_kernel(q_ref, k_ref, v_ref, gcol_ref, y_ref, kx_ref, vx_ref, qx_ref, s_ref, p_ref, acc_ref,
                m_ref, al_ref, *, tq, lp):
    qi = pl.program_id(1)
    nq = lp // tq
    lane = lax.broadcasted_iota(jnp.int32, (tq, LANES), 1)
    nt_dims = (((1,), (1,)), ((), ()))

    @pl.when(qi == 0)
    def _():
        def build(cb, carry):
            r0 = pl.multiple_of(cb * tq, tq)
            rows = pl.ds(r0, tq)
            is_token = (r0 + lax.broadcasted_iota(jnp.int32, (tq, 1), 0)) >= PAD
            g = gcol_ref[rows, :]
            ones_col = jnp.where(lane == 0, 1.0, 0.0).astype(BF16)
            for h in range(HEADS):
                cols = slice(h * DH, (h + 1) * DH)
                hi, mid, lo = _split3(jnp.where(is_token, g[:, GC_FF + h:GC_FF + h + 1], FOX_BIG))
                ext = jnp.where(lane < 3, 1.0, jnp.where(lane == 3, -hi, jnp.where(lane == 4, -mid,
                                jnp.where(lane == 5, -lo, 0.0))))
                kx_ref[h, rows, 0:DH] = k_ref[rows, cols]
                kx_ref[h, rows, DH:] = ext.astype(BF16)
                vx_ref[h, rows, 0:DH] = v_ref[rows, cols]
                vx_ref[h, rows, DH:] = ones_col
            return carry

        lax.fori_loop(0, nq, build, 0)

    gq = gcol_ref[pl.ds(pl.multiple_of(qi * tq, tq), tq), :]
    row_io = lax.broadcasted_iota(jnp.int32, (FOX_ROWS, tq), 0)
    col_io = lax.broadcasted_iota(jnp.int32, (FOX_ROWS, tq), 1)

    for h in range(HEADS):
        hi, mid, lo = _split3(gq[:, GC_FF + h:GC_FF + h + 1])
        ext = jnp.where(lane == 0, hi, jnp.where(lane == 1, mid, jnp.where(lane == 2, lo,
                        jnp.where(lane < 6, 1.0, 0.0))))
        qx_ref[h, :, 0:DH] = q_ref[:, h * DH:(h + 1) * DH]
        qx_ref[h, :, DH:] = ext.astype(BF16)
    m_ref[...] = jnp.full(m_ref.shape, NEG, F32)
    acc_ref[...] = jnp.zeros_like(acc_ref)

    def scores(kb, h):
        krows = pl.ds(pl.multiple_of(kb * tq, tq), tq)
        s_ref[h] = lax.dot_general(qx_ref[h], kx_ref[h, krows, :], nt_dims, preferred_element_type=F32)

    def step(kb, diagonal):
        krows = pl.ds(pl.multiple_of(kb * tq, tq), tq)
        for h in range(HEADS):
            for r0 in range(0, tq, FOX_ROWS):
                rr = slice(r0, r0 + FOX_ROWS)
                sc = s_ref[h, rr, :]
                if diagonal:
                    sc = jnp.where(col_io <= row_io + r0, sc, NEG)
                m_old = m_ref[h, rr, :]
                m_new = jnp.maximum(m_old, jnp.max(sc, axis=-1, keepdims=True))
                al_ref[h, rr, :] = jnp.exp(m_old - m_new)
                m_ref[h, rr, :] = m_new
                p_ref[h, rr, :] = jnp.exp(sc - m_new).astype(BF16)
            if not diagonal:
                scores(kb + 1, h)
            pv = jnp.dot(p_ref[h], vx_ref[h, krows, :], preferred_element_type=F32)
            acc_ref[h] = al_ref[h] * acc_ref[h] + pv

    def off_diagonal(kb, carry):
        step(kb, False)
        return carry

    for h in range(HEADS):
        scores(0, h)
    lax.fori_loop(0, qi, off_diagonal, 0)
    step(qi, True)
    for h in range(HEADS):
        acc = acc_ref[h]
        y_ref[:, h * DH:(h + 1) * DH] = (acc[:, 0:DH] / acc[:, DH:DH + 1]).astype(BF16)


def _fox(proj3, gcol):
    bsz, lp, _ = proj3.shape
    tq = _pick(lp, 384, BLOCK)
    nq = lp // tq
    return pl.pallas_call(
        functools.partial(_fox_kernel, tq=tq, lp=lp),
        grid=(bsz, nq),
        in_specs=[
            pl.BlockSpec((None, tq, GROUP_W), lambda b, i: (b, i, COL_FQ)),
            pl.BlockSpec((None, lp, GROUP_W), lambda b, i: (b, 0, COL_FK)),
            pl.BlockSpec((None, lp, GROUP_W), lambda b, i: (b, 0, COL_FV)),
            pl.BlockSpec((None, lp, LANES), lambda b, i: (b, 0, 0)),
        ],
        out_specs=pl.BlockSpec((tq, GROUP_W), lambda b, i: (b * nq + i, 0)),
        out_shape=jax.ShapeDtypeStruct((bsz * lp, GROUP_W), BF16),
        scratch_shapes=[
            pltpu.VMEM((HEADS, lp, 2 * DH), BF16),
            pltpu.VMEM((HEADS, lp, 2 * DH), BF16),
            pltpu.VMEM((HEADS, tq, 2 * DH), BF16),
            pltpu.VMEM((HEADS, tq, tq), F32),
            pltpu.VMEM((HEADS, tq, tq), BF16),
            pltpu.VMEM((HEADS, tq, 2 * DH), F32),
            pltpu.VMEM((HEADS, tq, 1), F32),
            pltpu.VMEM((HEADS, tq, 1), F32),
        ],
        compiler_params=_params(("arbitrary", "arbitrary")),
        name="fox",
    )(proj3, proj3, proj3, gcol)


def _pool_kernel(u_ref, w_ref, sc_ref, y_ref, buf_ref, *, tb):
    t = pl.program_id(1)

    @pl.when(t == 0)
    def _():
        buf_ref[0:POOL_MAXW, :] = jnp.zeros((POOL_MAXW, GROUP_W), F32)

    @pl.when(t > 0)
    def _():
        buf_ref[0:POOL_MAXW, :] = buf_ref[tb:tb + POOL_MAXW, :]

    buf_ref[POOL_MAXW:POOL_MAXW + tb, :] = u_ref[...].astype(F32)
    pos = t * tb + lax.broadcasted_iota(jnp.int32, (tb, 1), 0) - (PAD - 1)
    posf = jnp.maximum(pos, 1).astype(F32)
    gw = GROUP_W // len(POOL_WINDOWS)
    for gi, w in enumerate(POOL_WINDOWS):
        cols = slice(gi * gw, (gi + 1) * gw)
        x = buf_ref[POOL_MAXW:POOL_MAXW + tb, cols]
        s = x
        for j in range(1, w):
            s = s + buf_ref[POOL_MAXW - j:POOL_MAXW - j + tb, cols]
        pooled = s / jnp.minimum(posf, float(w)) - x
        mixed = jnp.dot(pooled.astype(BF16), w_ref[gi], preferred_element_type=F32)
        y_ref[:, cols] = (mixed * sc_ref[:, cols]).astype(BF16)


def _pool(proj, pool_w, pool_scale, bsz, lp):
    tb = _pick(lp, 528, 16)
    nt = lp // tb
    gw = GROUP_W // len(POOL_WINDOWS)
    return pl.pallas_call(
        functools.partial(_pool_kernel, tb=tb),
        grid=(bsz, nt),
        in_specs=[
            pl.BlockSpec((tb, GROUP_W), lambda b, t: (b * nt + t, COL_PU)),
            pl.BlockSpec((len(POOL_WINDOWS), gw, gw), lambda b, t: (0, 0, 0)),
            pl.BlockSpec((1, GROUP_W), lambda b, t: (0, 0)),
        ],
        out_specs=pl.BlockSpec((tb, GROUP_W), lambda b, t: (b * nt + t, 0)),
        out_shape=jax.ShapeDtypeStruct((bsz * lp, GROUP_W), BF16),
        scratch_shapes=[pltpu.VMEM((POOL_MAXW + tb, GROUP_W), F32)],
        compiler_params=_params(("arbitrary", "arbitrary")),
        name="pool",
    )(proj, pool_w, pool_scale)


def _outproj_kernel(ys_ref, ym_ref, yf_ref, yp_ref, w_ref, g_ref, h_ref, o_ref, *, tm, blocks_per_seq):
    mix = jnp.dot(ys_ref[...], w_ref[0:GROUP_W, :], preferred_element_type=F32)
    for n, y_ref in enumerate((ym_ref, yf_ref, yp_ref), start=1):
        mix = mix + jnp.dot(y_ref[...], w_ref[n * GROUP_W:(n + 1) * GROUP_W, :], preferred_element_type=F32)
    off = (pl.program_id(0) % blocks_per_seq) * tm
    valid = (off + lax.broadcasted_iota(jnp.int32, (tm, 1), 0)) >= PAD
    o_ref[...] = jnp.where(valid, h_ref[...] + _rms(mix, g_ref[...]), 0.0)


def _outproj(ys, ym, yf, yp, w_out_all, g, h2, lp, l):
    m = h2.shape[0]
    tm = _pick(lp, 528, 16)
    yspec = pl.BlockSpec((tm, GROUP_W), lambda i: (i, 0))
    return pl.pallas_call(
        functools.partial(_outproj_kernel, tm=tm, blocks_per_seq=lp // tm),
        grid=(m // tm,),
        in_specs=[
            yspec, yspec, yspec, yspec,
            pl.BlockSpec((None, D_MODEL, D_MODEL), lambda i: (l, 0, 0)),
            pl.BlockSpec((1, D_MODEL), lambda i: (0, 0)),
            pl.BlockSpec((tm, D_MODEL), lambda i: (i, 0)),
        ],
        out_specs=pl.BlockSpec((tm, D_MODEL), lambda i: (i, 0)),
        out_shape=jax.ShapeDtypeStruct((m, D_MODEL), F32),
        input_output_aliases={6: 0},
        compiler_params=_params(("arbitrary",)),
        name="outproj",
    )(ys, ym, yf, yp, w_out_all, g, h2)


def _ffn_kernel(h_ref, g1_ref, w1_ref, w2_ref, g2_ref, o_ref, hn_ref, acc_ref):
    f = pl.program_id(1)
    last = pl.num_programs(1) - 1
    tm = h_ref.shape[0]
    halves = (slice(0, tm // 2), slice(tm // 2, tm))

    def update(rows):
        a = jnp.maximum(jnp.dot(hn_ref[rows, :], w1_ref[...], preferred_element_type=F32), 0.0)
        return jnp.dot((a * a).astype(BF16), w2_ref[...], preferred_element_type=F32)

    @pl.when(f == 0)
    def _():
        for rows in halves:
            hn_ref[rows, :] = _rms(h_ref[rows, :], g1_ref[...]).astype(BF16)
            acc_ref[rows, :] = update(rows)

    @pl.when((f > 0) & (f < last))
    def _():
        acc_ref[...] += update(slice(0, tm))

    @pl.when(f == last)
    def _():
        for rows in halves:
            o_ref[rows, :] = h_ref[rows, :] + _rms(acc_ref[rows, :] + update(rows), g2_ref[...])


FFN_TF = 1024


def _ffn_specs(l, tm, h_spec):
    return [
        h_spec,
        pl.BlockSpec((1, D_MODEL), lambda i, f: (0, 0)),
        pl.BlockSpec((None, D_MODEL, FFN_TF), lambda i, f: (l, 0, f)),
        pl.BlockSpec((None, FFN_TF, D_MODEL), lambda i, f: (l, f, 0)),
        pl.BlockSpec((1, D_MODEL), lambda i, f: (0, 0)),
    ], [pltpu.VMEM((tm, D_MODEL), BF16), pltpu.VMEM((tm, D_MODEL), F32)]


def _ffn(h2, g1, w1_all, w2_all, g2, l):
    m = h2.shape[0]
    tm = _pick(m, 768, 16)
    in_specs, scratch = _ffn_specs(l, tm, pl.BlockSpec((tm, D_MODEL), lambda i, f: (i, 0)))
    return pl.pallas_call(
        _ffn_kernel,
        grid=(m // tm, D_FF // FFN_TF),
        in_specs=in_specs,
        out_specs=pl.BlockSpec((tm, D_MODEL), lambda i, f: (i, 0)),
        out_shape=jax.ShapeDtypeStruct((m, D_MODEL), F32),
        scratch_shapes=scratch,
        input_output_aliases={0: 0},
        compiler_params=_params(("arbitrary", "arbitrary")),
        name="ffn",
    )(h2, g1, w1_all, w2_all, g2)


def _ffn_final(h2, g1, w1_all, w2_all, g2, l, bsz, lp):
    seq = lp - PAD - N_META
    tm = _pick(seq, 512, 16)
    per_seq = seq // tm
    h_spec = pl.BlockSpec((pl.Element(tm), pl.Element(D_MODEL)),
                          lambda i, f: (pl.multiple_of((i // per_seq) * lp + PAD + N_META + (i % per_seq) * tm,
                                                       SUBLANES), 0))
    in_specs, scratch = _ffn_specs(l, tm, h_spec)
    return pl.pallas_call(
        _ffn_kernel,
        grid=(bsz * per_seq, D_FF // FFN_TF),
        in_specs=in_specs,
        out_specs=pl.BlockSpec((tm, D_MODEL), lambda i, f: (i, 0)),
        out_shape=jax.ShapeDtypeStruct((bsz * seq, D_MODEL), F32),
        scratch_shapes=scratch,
        compiler_params=_params(("arbitrary", "arbitrary")),
        name="ffn_final",
    )(h2, g1, w1_all, w2_all, g2)


IN_COLS = N_MAIN + 3 * HEADS
IN_COLS_PAD = -(-IN_COLS // LANES) * LANES
O_MI = 5 * GROUP_W
O_FQ = O_MI + 2 * HEADS
O_FF = O_FQ + 3 * GROUP_W
O_PU = O_FF + HEADS


def _split_w_in_kernel(w_ref, main_ref, gate_ref):
    rows = w_ref.shape[0]
    lane = lax.broadcasted_iota(jnp.int32, (rows, LANES), 1)
    main_ref[:, 0:O_MI] = w_ref[:, 0:O_MI].astype(BF16)

    def shifted_copy(dst0, src0, width):
        shift = src0 % LANES
        base = src0 - shift
        for c in range(0, width, LANES):
            lo = w_ref[:, base + c:base + c + LANES]
            hi = w_ref[:, base + c + LANES:base + c + 2 * LANES]
            mixed = jnp.where(lane >= shift, lo, hi)
            main_ref[:, dst0 + c:dst0 + c + LANES] = pltpu.roll(mixed, LANES - shift, axis=1).astype(BF16)

    shifted_copy(O_MI, O_FQ, 3 * GROUP_W)
    shifted_copy(O_MI + 3 * GROUP_W, O_PU, GROUP_W)
    g_ml = w_ref[:, O_MI:O_MI + LANES]
    ff0 = O_FF - O_FF % LANES
    g_fx = w_ref[:, ff0:ff0 + LANES]
    assert O_FF - ff0 == GC_FF
    gate = jnp.where(lane < GC_FF, g_ml, jnp.where(lane < GC_FF + HEADS, g_fx, 0.0))
    gate_ref[...] = gate.astype(BF16)


def _split_w_in(w_all, l):
    tr = 256
    return pl.pallas_call(
        _split_w_in_kernel,
        grid=(D_MODEL // tr,),
        in_specs=[pl.BlockSpec((None, tr, IN_COLS_PAD), lambda i: (l, i, 0))],
        out_specs=[pl.BlockSpec((tr, N_MAIN), lambda i: (i, 0)), pl.BlockSpec((tr, LANES), lambda i: (i, 0))],
        out_shape=[jax.ShapeDtypeStruct((D_MODEL, N_MAIN), BF16), jax.ShapeDtypeStruct((D_MODEL, LANES), BF16)],
        compiler_params=_params(("arbitrary",)),
        name="split_w_in",
    )(w_all)


def _s5_matrices(bbre, bbim, c_re, c_im):
    gh = SSM_G // 2
    mask = (jnp.arange(GROUP_W // 2)[:, None] // SSM_H == jnp.arange(S5_HALF)[None, :] // SSM_P).astype(F32)
    c_t = lambda c: c.transpose(0, 2, 1).reshape(SSM_N, SSM_H)
    bmats, cmats = [], []
    for k in range(2):
        st = slice(S5_HALF * k, S5_HALF * (k + 1))
        expand = lambda bb: jnp.tile(bb[:, st], (gh, 1)) * mask
        cont = lambda c: jnp.tile(c_t(c)[st, :], (1, gh)) * mask.T
        bmats.append(jnp.concatenate([expand(bbre), expand(bbim)], axis=1))
        cmats.append(jnp.concatenate([cont(c_re), -cont(c_im)], axis=0))
    return jnp.stack(bmats).astype(BF16), jnp.stack(cmats).astype(BF16)


def kernel(x, meta_tokens, g_pre_mix, g_post_mix, g_pre_ffn, g_post_ffn, w_in, ml_gate_bias, fx_gate_bias, ssm_lam_re, ssm_lam_im, ssm_log_dt, ssm_b_re, ssm_b_im, ssm_c_re, ssm_c_im, ssm_d, ssm_glu_w, ssm_glu_b, ml_conv_w, ml_norm_g, pool_w, pool_scale, w_out, mlp_w1, mlp_w2):
    bsz, seq, d = x.shape
    depth = w_in.shape[0]
    lp = PAD + N_META + seq
    m = bsz * lp
    assert d == D_MODEL and lp % BLOCK == 0
    row = lambda a: a.reshape(1, -1).astype(F32)
    colscale = jnp.ones((1, N_MAIN), F32).at[:, COL_FQ * GROUP_W:(COL_FQ + 1) * GROUP_W].set(DH ** -0.5)
    w_out16, w1_16, w2_16 = w_out.astype(BF16), mlp_w1.astype(BF16), mlp_w2.astype(BF16)

    for l in range(depth):
        w_main, w_gate = _split_w_in(w_in, l)
        if l == 0:
            proj, graw, h = _inproj_first(x.reshape(bsz * seq, d).astype(F32), meta_tokens.astype(F32),
                                          row(g_pre_mix[l]), w_main, w_gate, colscale, bsz, lp)
        else:
            proj, graw = _inproj(h, row(g_pre_mix[l]), w_main, w_gate, colscale)
        proj3 = proj.reshape(bsz, lp, N_MAIN)

        gbias = jnp.pad(jnp.concatenate([ml_gate_bias[l], fx_gate_bias[l]]).astype(F32), (0, LANES - 3 * HEADS))
        gcol, grow = _gates(graw, gbias.reshape(1, LANES), bsz, lp)

        consts, bbre, bbim = _s5_prep(ssm_lam_re[l], ssm_lam_im[l], ssm_log_dt[l], ssm_b_re[l], ssm_b_im[l], bsz)
        bmat, cmat = _s5_matrices(bbre, bbim, ssm_c_re[l], ssm_c_im[l])
        y_ssm = _s5(proj3, bmat, cmat, row(ssm_d[l]), consts, ssm_glu_w[l].astype(BF16), row(ssm_glu_b[l]))
        y_ml = _mlstm(proj3, gcol, grow, ml_conv_w[l].astype(F32), row(ml_norm_g[l]))
        y_fx = _fox(proj3, gcol)
        y_pool = _pool(proj, pool_w[l].astype(BF16), row(pool_scale[l]), bsz, lp)

        h = _outproj(y_ssm.reshape(m, GROUP_W), y_ml.reshape(m, GROUP_W), y_fx, y_pool,
                     w_out16, row(g_post_mix[l]), h, lp, l)
        if l + 1 < depth:
            h = _ffn(h, row(g_pre_ffn[l]), w1_16, w2_16, row(g_post_ffn[l]), l)
        else:
            h = _ffn_final(h, row(g_pre_ffn[l]), w1_16, w2_16, row(g_post_ffn[l]), l, bsz, lp)

    return h.reshape(bsz, seq, d)
```

```python
import functools
import math

import jax
import jax.numpy as jnp
from jax import lax
from jax.experimental import pallas as pl
from jax.experimental.pallas import tpu as pltpu

F32 = jnp.float32
BF16 = jnp.bfloat16

D_MODEL = 2048
N_META = 16
BLOCK = 128
PAD = BLOCK - N_META
GROUP_W = 512
SSM_H = 16
SSM_G = GROUP_W // SSM_H
SSM_P = 64
SSM_N = SSM_G * SSM_P
HEADS = 4
DH = GROUP_W // HEADS
ML_CONV = 4
POOL_WINDOWS = (2, 4, 8, 16)
POOL_MAXW = 16
D_FF = 4 * D_MODEL
EPS = 1e-6
NEG = -1e30
N_MAIN = 9 * GROUP_W
LANES = 128
SUBLANES = 8
VMEM_LIMIT = 60 * 1024 * 1024

COL_SU, COL_MQ, COL_MK, COL_MV, COL_MO, COL_FQ, COL_FK, COL_FV, COL_PU = range(9)
GC_MI, GC_MF, GC_FF, GC_RM = 0, HEADS, 2 * HEADS, 3 * HEADS


def _pick(n, target, mult):
    best = None
    for d in range(mult, min(n, target) + 1, mult):
        if n % d == 0:
            best = d
    assert best is not None, (n, target, mult)
    return best


def _params(sem):
    return pltpu.CompilerParams(dimension_semantics=sem, vmem_limit_bytes=VMEM_LIMIT)


def _rms(x, g):
    return x * lax.rsqrt(jnp.mean(x * x, axis=-1, keepdims=True) + EPS) * g


def _inproj_kernel(x_ref, g_ref, w_ref, wg_ref, cs_ref, o_ref, og_ref, xn_ref):
    @pl.when(pl.program_id(1) == 0)
    def _():
        xn = _rms(x_ref[...], g_ref[...]).astype(BF16)
        xn_ref[...] = xn
        og_ref[...] = jnp.dot(xn, wg_ref[...], preferred_element_type=F32)

    acc = jnp.dot(xn_ref[...], w_ref[...], preferred_element_type=F32)
    o_ref[...] = (acc * cs_ref[...]).astype(BF16)


def _inproj(h2, g, w_main, w_gate, colscale, l):
    m = h2.shape[0]
    tm = _pick(m, 1056, 16)
    tn = 1536
    return pl.pallas_call(
        _inproj_kernel,
        grid=(m // tm, N_MAIN // tn),
        in_specs=[
            pl.BlockSpec((tm, D_MODEL), lambda i, j: (i, 0)),
            pl.BlockSpec((1, D_MODEL), lambda i, j: (0, 0)),
            pl.BlockSpec((None, D_MODEL, tn), lambda i, j: (l, 0, j)),
            pl.BlockSpec((None, D_MODEL, LANES), lambda i, j: (l, 0, 0)),
            pl.BlockSpec((1, tn), lambda i, j: (0, j)),
        ],
        out_specs=[
            pl.BlockSpec((tm, tn), lambda i, j: (i, j)),
            pl.BlockSpec((tm, LANES), lambda i, j: (i, 0)),
        ],
        out_shape=[jax.ShapeDtypeStruct((m, N_MAIN), BF16), jax.ShapeDtypeStruct((m, LANES), F32)],
        scratch_shapes=[pltpu.VMEM((tm, D_MODEL), BF16)],
        compiler_params=_params(("arbitrary", "arbitrary")),
        name="inproj",
    )(h2, g, w_main, w_gate, colscale)


def _inproj_first_kernel(x_ref, meta_ref, g_ref, w_ref, wg_ref, cs_ref, o_ref, og_ref, h_ref, xn_ref,
                         *, tm, tiles_per_seq):
    head = PAD + N_META
    first_n = pl.program_id(1) == 0
    seq_start = pl.program_id(0) % tiles_per_seq == 0

    @pl.when(first_n & seq_start)
    def _():
        h_ref[0:PAD, :] = jnp.zeros((PAD, D_MODEL), F32)
        h_ref[PAD:head, :] = meta_ref[...]
        h_ref[head:tm, :] = x_ref[0:tm - head, :]

    @pl.when(first_n & jnp.logical_not(seq_start))
    def _():
        h_ref[...] = x_ref[...]

    @pl.when(first_n)
    def _():
        xn = _rms(h_ref[...], g_ref[...]).astype(BF16)
        xn_ref[...] = xn
        og_ref[...] = jnp.dot(xn, wg_ref[...], preferred_element_type=F32)

    acc = jnp.dot(xn_ref[...], w_ref[...], preferred_element_type=F32)
    o_ref[...] = (acc * cs_ref[...]).astype(BF16)


def _inproj_first(x2, meta, g, w_main, w_gate, colscale, bsz, lp, l=0):
    seq = lp - PAD - N_META
    tm = _pick(lp, min(1056, lp // 2), 16)
    assert tm > PAD + N_META
    per_seq = lp // tm
    tn = 768
    m = bsz * lp

    def x_index(i, j):
        b, t = i // per_seq, i % per_seq
        return pl.multiple_of(b * seq + jnp.maximum(t * tm - (PAD + N_META), 0), SUBLANES), 0

    return pl.pallas_call(
        functools.partial(_inproj_first_kernel, tm=tm, tiles_per_seq=per_seq),
        grid=(m // tm, N_MAIN // tn),
        in_specs=[
            pl.BlockSpec((pl.Element(tm), pl.Element(D_MODEL)), x_index),
            pl.BlockSpec((N_META, D_MODEL), lambda i, j: (0, 0)),
            pl.BlockSpec((1, D_MODEL), lambda i, j: (0, 0)),
            pl.BlockSpec((None, D_MODEL, tn), lambda i, j: (l, 0, j)),
            pl.BlockSpec((None, D_MODEL, LANES), lambda i, j: (l, 0, 0)),
            pl.BlockSpec((1, tn), lambda i, j: (0, j)),
        ],
        out_specs=[
            pl.BlockSpec((tm, tn), lambda i, j: (i, j)),
            pl.BlockSpec((tm, LANES), lambda i, j: (i, 0)),
            pl.BlockSpec((tm, D_MODEL), lambda i, j: (i, 0)),
        ],
        out_shape=[jax.ShapeDtypeStruct((m, N_MAIN), BF16), jax.ShapeDtypeStruct((m, LANES), F32),
                   jax.ShapeDtypeStruct((m, D_MODEL), F32)],
        scratch_shapes=[pltpu.VMEM((tm, D_MODEL), BF16)],
        compiler_params=_params(("arbitrary", "arbitrary")),
        name="inproj_first",
    )(x2, meta, g, w_main, w_gate, colscale)


def _gates_kernel(raw_ref, bias_ref, col_ref, row_ref, carry_ref):
    c = pl.program_id(0)

    @pl.when(c == 0)
    def _():
        carry_ref[...] = jnp.zeros_like(carry_ref)

    r_io = lax.broadcasted_iota(jnp.int32, (BLOCK, LANES), 0)
    c_io = lax.broadcasted_iota(jnp.int32, (BLOCK, LANES), 1)
    valid = (r_io + c * BLOCK) >= PAD
    tri = (r_io >= c_io).astype(F32)
    for b in range(raw_ref.shape[0]):
        g = raw_ref[b] + bias_ref[...]
        logsig = jnp.minimum(g, 0.0) - jnp.log(1.0 + jnp.exp(-jnp.abs(g)))
        x = jnp.where(valid, logsig, 0.0)
        x = jnp.where(c_io >= GC_MF, jnp.where(c_io < GC_FF + HEADS, x, 0.0), 0.0)
        cum = jnp.dot(tri, x, precision=lax.Precision.HIGHEST, preferred_element_type=F32)
        cum = cum + jnp.where(c_io >= GC_FF, carry_ref[b], 0.0)
        carry_ref[b] = cum[BLOCK - 1:BLOCK, :]
        log_i = jnp.where(valid, g, NEG)
        r = log_i - pltpu.roll(cum, LANES - GC_MF, axis=1)
        rmax = r
        d = 1
        while d < BLOCK:
            rmax = jnp.maximum(rmax, jnp.where(r_io >= d, pltpu.roll(rmax, d, axis=0), NEG))
            d *= 2
        out = jnp.where(c_io < GC_MF, r, jnp.where(c_io < GC_RM, cum, jnp.where(
            c_io < GC_RM + HEADS, pltpu.roll(rmax, GC_RM, axis=1), 0.0)))
        col_ref[b] = out
        row_ref[b, 0] = out.T[0:2 * SUBLANES, :]


def _gates(raw, bias, bsz, lp):
    nc = lp // BLOCK
    return pl.pallas_call(
        _gates_kernel,
        grid=(nc,),
        in_specs=[
            pl.BlockSpec((bsz, BLOCK, LANES), lambda c: (0, c, 0)),
            pl.BlockSpec((1, LANES), lambda c: (0, 0)),
        ],
        out_specs=[
            pl.BlockSpec((bsz, BLOCK, LANES), lambda c: (0, c, 0)),
            pl.BlockSpec((bsz, 1, 2 * SUBLANES, BLOCK), lambda c: (0, c, 0, 0)),
        ],
        out_shape=[
            jax.ShapeDtypeStruct((bsz, lp, LANES), F32),
            jax.ShapeDtypeStruct((bsz, nc, 2 * SUBLANES, BLOCK), F32),
        ],
        scratch_shapes=[pltpu.VMEM((bsz, 1, LANES), F32)],
        compiler_params=_params(("arbitrary",)),
        name="gates",
    )(raw.reshape(bsz, lp, LANES), bias)


S5_CONST_ROWS = 64
S5_STRIPS = SSM_N // LANES
S5_STRIPS_PER_LOOP = 4
S5_HALF = SSM_N // 2


def _s5_hs_steps(bsz):
    return [d for d in (1, 2, 4) if d < SUBLANES // bsz]


def _s5_prep_kernel(lre_ref, lim_ref, ldt_ref, bre_ref, bim_ref, consts_ref, bbre_ref, bbim_ref, *, bsz):
    lre = lre_ref[...]
    lim = lim_ref[...]
    dt = jnp.exp(ldt_ref[...])
    r_io = lax.broadcasted_iota(jnp.int32, (SUBLANES, SSM_N), 0)
    kk = (r_io + 1).astype(F32)
    mag = jnp.exp(kk * (lre * dt))
    ang = kk * (lim * dt)
    pw_re = mag * jnp.cos(ang)
    pw_im = mag * jnp.sin(ang)
    consts_ref[...] = jnp.zeros_like(consts_ref)
    steps = _s5_hs_steps(bsz)
    for i, d in enumerate(steps):
        keep = r_io >= d * bsz
        consts_ref[16 * i:16 * i + 8, :] = jnp.where(keep, pw_re[d - 1:d, :], 0.0)
        consts_ref[16 * i + 8:16 * i + 16, :] = jnp.where(keep, pw_im[d - 1:d, :], 0.0)
    t_in_vreg = lax.shift_right_logical(r_io, bsz.bit_length() - 1)
    cw_re = jnp.zeros((SUBLANES, SSM_N), F32)
    cw_im = jnp.zeros((SUBLANES, SSM_N), F32)
    for j in range(SUBLANES // bsz):
        cw_re = jnp.where(t_in_vreg == j, pw_re[j:j + 1, :], cw_re)
        cw_im = jnp.where(t_in_vreg == j, pw_im[j:j + 1, :], cw_im)
    base = 16 * len(steps)
    consts_ref[base:base + 8, :] = cw_re
    consts_ref[base + 8:base + 16, :] = cw_im
    nr = pw_re[0:1, :] - 1.0
    ni = pw_im[0:1, :]
    den = lre * lre + lim * lim
    cr = (nr * lre + ni * lim) / den
    ci = (ni * lre - nr * lim) / den
    bre = bre_ref[...]
    bim = bim_ref[...]
    bbre_ref[...] = cr * bre - ci * bim
    bbim_ref[...] = cr * bim + ci * bre


def _s5_prep(lam_re, lam_im, log_dt, b_re, b_im, bsz):
    flat = lambda a: a.reshape(1, SSM_N)
    ldt = jnp.repeat(log_dt, SSM_P).reshape(1, SSM_N)
    bt = lambda a: a.transpose(2, 0, 1).reshape(SSM_H, SSM_N)
    return pl.pallas_call(
        functools.partial(_s5_prep_kernel, bsz=bsz),
        out_shape=[
            jax.ShapeDtypeStruct((S5_CONST_ROWS, SSM_N), F32),
            jax.ShapeDtypeStruct((SSM_H, SSM_N), F32),
            jax.ShapeDtypeStruct((SSM_H, SSM_N), F32),
        ],
        name="s5_prep",
    )(flat(lam_re), flat(lam_im), ldt, bt(b_re), bt(b_im))


def _s5_kernel(u_ref, bmat_ref, cmat_ref, d_ref, consts_ref, gw_ref, gb_ref, y_ref,
               uil_ref, oil_ref, x_ref, carry_ref, *, tb, bsz):
    rows = bsz * tb
    n_us = GROUP_W // LANES

    @pl.when(pl.program_id(0) == 0)
    def _():
        carry_ref[...] = jnp.zeros_like(carry_ref)

    for b in range(bsz):
        ub = u_ref[b].astype(F32)
        for s in range(n_us):
            uil_ref[s, pl.ds(b, tb, stride=bsz), :] = ub[:, s * LANES:(s + 1) * LANES]
    u = jnp.concatenate([uil_ref[s] for s in range(n_us)], axis=1)
    u16 = u.astype(BF16)
    half_w = GROUP_W // 2
    for k in range(2):
        x_ref[:, 2 * S5_HALF * k:2 * S5_HALF * (k + 1)] = jnp.dot(
            u16[:, half_w * k:half_w * (k + 1)], bmat_ref[k], preferred_element_type=F32)

    def cmul_add(xr, xi, ar, ai, sr, si):
        return xr + ar * sr - ai * si, xi + ar * si + ai * sr

    row8 = lax.broadcasted_iota(jnp.int32, (SUBLANES, LANES), 0)
    steps = _s5_hs_steps(bsz)
    cbase = 16 * len(steps)

    def last_step_tile(c):
        span = SUBLANES
        while span > bsz:
            half = span // 2
            c = jnp.where((row8 & (span - 1)) < half, pltpu.roll(c, half, axis=0), c)
            span = half
        return c

    def lanes_of(s):
        k, j = divmod(s, S5_STRIPS // 2)
        re0 = 2 * S5_HALF * k + j * LANES
        return slice(re0, re0 + LANES), slice(re0 + S5_HALF, re0 + S5_HALF + LANES), slice(s * LANES, (s + 1) * LANES)

    for s0 in range(0, S5_STRIPS, S5_STRIPS_PER_LOOP):
        strips = range(s0, s0 + S5_STRIPS_PER_LOOP)

        def body(r, carry, strips=strips):
            r8 = pl.ds(pl.multiple_of(r * SUBLANES, SUBLANES), SUBLANES)
            out = []
            for n, s in enumerate(strips):
                lr, li, lc = lanes_of(s)
                xr = x_ref[r8, lr]
                xi = x_ref[r8, li]
                for i, d in enumerate(steps):
                    xr, xi = cmul_add(xr, xi, consts_ref[16 * i:16 * i + 8, lc], consts_ref[16 * i + 8:16 * i + 16, lc],
                                      pltpu.roll(xr, d * bsz, axis=0), pltpu.roll(xi, d * bsz, axis=0))
                xr, xi = cmul_add(xr, xi, consts_ref[cbase:cbase + 8, lc], consts_ref[cbase + 8:cbase + 16, lc],
                                  carry[2 * n], carry[2 * n + 1])
                x_ref[r8, lr] = xr
                x_ref[r8, li] = xi
                out.append(last_step_tile(xr))
                out.append(last_step_tile(xi))
            return tuple(out)

        init = []
        for s in strips:
            lr, li, _ = lanes_of(s)
            init += [carry_ref[:, lr], carry_ref[:, li]]
        fin = lax.fori_loop(0, rows // SUBLANES, body, tuple(init))
        for n, s in enumerate(strips):
            lr, li, _ = lanes_of(s)
            carry_ref[:, lr] = fin[2 * n]
            carry_ref[:, li] = fin[2 * n + 1]

    y = jnp.concatenate(
        [jnp.dot(x_ref[:, 2 * S5_HALF * k:2 * S5_HALF * (k + 1)].astype(BF16), cmat_ref[k],
                 preferred_element_type=F32) for k in range(2)], axis=1)
    y = jax.nn.gelu(y + d_ref[...] * u, approximate=True)
    z = jnp.dot(y.astype(BF16), gw_ref[...], preferred_element_type=F32) + gb_ref[...]
    out = z[:, :GROUP_W] * jax.nn.sigmoid(z[:, GROUP_W:])
    for s in range(n_us):
        oil_ref[s] = out[:, s * LANES:(s + 1) * LANES]
    for b in range(bsz):
        y_ref[b] = jnp.concatenate([oil_ref[s, pl.ds(b, tb, stride=bsz), :] for s in range(n_us)],
                                   axis=1).astype(BF16)


def _s5(proj3, bmat, cmat, dvec, consts, glu_w, glu_b):
    bsz, lp, _ = proj3.shape
    assert SUBLANES % bsz == 0
    tb = _pick(lp, 176, 16)
    rows = bsz * tb
    const = lambda shape: pl.BlockSpec(shape, lambda t: (0,) * len(shape))
    return pl.pallas_call(
        functools.partial(_s5_kernel, tb=tb, bsz=bsz),
        grid=(lp // tb,),
        in_specs=[
            pl.BlockSpec((bsz, tb, GROUP_W), lambda t: (0, t, COL_SU)),
            const((2, GROUP_W // 2, 2 * S5_HALF)),
            const((2, 2 * S5_HALF, GROUP_W // 2)),
            const((1, GROUP_W)),
            const((S5_CONST_ROWS, SSM_N)),
            const((GROUP_W, 2 * GROUP_W)),
            const((1, 2 * GROUP_W)),
        ],
        out_specs=pl.BlockSpec((bsz, tb, GROUP_W), lambda t: (0, t, 0)),
        out_shape=jax.ShapeDtypeStruct((bsz, lp, GROUP_W), BF16),
        scratch_shapes=[
            pltpu.VMEM((GROUP_W // LANES, rows, LANES), F32),
            pltpu.VMEM((GROUP_W // LANES, rows, LANES), F32),
            pltpu.VMEM((rows, 2 * SSM_N), F32),
            pltpu.VMEM((SUBLANES, 2 * SSM_N), F32),
        ],
        compiler_params=_params(("arbitrary",)),
        name="s5",
    )(proj3, bmat, cmat, dvec, consts, glu_w, glu_b)


def _mlstm_kernel(q_ref, k_ref, v_ref, o_ref, gcol_ref, grow_ref, cw_ref, ng_ref, y_ref,
                  buf_ref, cn_ref, m_ref):
    bsz = q_ref.shape[0]

    @pl.when(pl.program_id(0) == 0)
    def _():
        buf_ref[:, 0:SUBLANES, :] = jnp.zeros((bsz, SUBLANES, 2 * GROUP_W), F32)
        cn_ref[...] = jnp.zeros_like(cn_ref)
        m_ref[...] = jnp.zeros_like(m_ref)

    @pl.when(pl.program_id(0) > 0)
    def _():
        buf_ref[:, 0:SUBLANES, :] = buf_ref[:, BLOCK:BLOCK + SUBLANES, :]

    s_io = lax.broadcasted_iota(jnp.int32, (BLOCK, BLOCK), 0)
    t_io = lax.broadcasted_iota(jnp.int32, (BLOCK, BLOCK), 1)
    causal = s_io <= t_io
    ones_row = jnp.where(s_io == 0, 1.0, 0.0)
    nt_dims = (((1,), (1,)), ((), ()))

    for b in range(bsz):
        buf_ref[b, SUBLANES:SUBLANES + BLOCK, 0:GROUP_W] = q_ref[b].astype(F32)
        buf_ref[b, SUBLANES:SUBLANES + BLOCK, GROUP_W:] = k_ref[b].astype(F32)
        conv = cw_ref[ML_CONV - 1:ML_CONV, :] * buf_ref[b, SUBLANES:SUBLANES + BLOCK, :]
        for j in range(1, ML_CONV):
            conv = conv + cw_ref[ML_CONV - 1 - j:ML_CONV - j, :] * buf_ref[b, SUBLANES - j:SUBLANES - j + BLOCK, :]
        qk = conv * jax.nn.sigmoid(conv)
        gcol = gcol_ref[b]
        grow = grow_ref[b, 0]

        for h in range(HEADS):
            idx = b * HEADS + h
            cols = slice(h * DH, (h + 1) * DH)
            qh = qk[:, cols].astype(BF16)
            kh = (qk[:, GROUP_W + h * DH:GROUP_W + (h + 1) * DH] * (DH ** -0.5)).astype(BF16)
            v_t = jnp.concatenate([v_ref[b, :, cols].astype(F32).T, ones_row], axis=0)
            r_col = gcol[:, GC_MI + h:GC_MI + h + 1]
            r_row = grow[GC_MI + h:GC_MI + h + 1, :]
            b_row = grow[GC_MF + h:GC_MF + h + 1, :]
            rm_row = grow[GC_RM + h:GC_RM + h + 1, :]
            g = b_row[:, BLOCK - 1:BLOCK]
            r_max = rm_row[:, BLOCK - 1:BLOCK]
            cn_prev = cn_ref[idx]
            m_prev = m_ref[idx][:, 0:1]

            big_m = jnp.maximum(m_prev, rm_row)
            decay_t = jnp.where(causal, jnp.exp(r_col - big_m), 0.0)
            qk_t = lax.dot_general(kh, qh, nt_dims, preferred_element_type=F32)
            p_t = (qk_t * decay_t).astype(BF16)
            inter_w = jnp.exp(m_prev - big_m)
            tot = (jnp.dot(v_t.astype(BF16), p_t, preferred_element_type=F32)
                   + inter_w * lax.dot_general(cn_prev.astype(BF16), qh, nt_dims, preferred_element_type=F32))
            num = tot[0:DH, :]
            inv = 1.0 / jnp.maximum(jnp.abs(tot[DH:DH + 1, :]), jnp.exp(-(b_row + big_m)))
            scale = inv * lax.rsqrt(inv * inv * jnp.mean(num * num, axis=0, keepdims=True) + EPS)
            hn = (num * scale).T * ng_ref[:, cols]
            y_ref[b, :, cols] = (hn * jax.nn.sigmoid(o_ref[b, :, cols].astype(F32))).astype(BF16)

            vw_t = (v_t * jnp.exp(r_row - r_max)).astype(BF16)
            cn_loc = jnp.dot(vw_t, kh, preferred_element_type=F32)
            m_keep = jnp.maximum(m_prev, r_max)
            cn_ref[idx] = jnp.exp(m_prev - m_keep) * cn_prev + jnp.exp(r_max - m_keep) * cn_loc
            m_ref[idx] = jnp.broadcast_to(g + m_keep, (1, LANES))


def _mlstm(proj3, gcol, grow, conv_w, norm_g):
    bsz, lp, _ = proj3.shape
    blk = lambda col: pl.BlockSpec((bsz, BLOCK, GROUP_W), lambda c: (0, c, col))
    return pl.pallas_call(
        _mlstm_kernel,
        grid=(lp // BLOCK,),
        in_specs=[
            blk(COL_MQ), blk(COL_MK), blk(COL_MV), blk(COL_MO),
            pl.BlockSpec((bsz, BLOCK, LANES), lambda c: (0, c, 0)),
            pl.BlockSpec((bsz, 1, 2 * SUBLANES, BLOCK), lambda c: (0, c, 0, 0)),
            pl.BlockSpec((ML_CONV, 2 * GROUP_W), lambda c: (0, 0)),
            pl.BlockSpec((1, GROUP_W), lambda c: (0, 0)),
        ],
        out_specs=pl.BlockSpec((bsz, BLOCK, GROUP_W), lambda c: (0, c, 0)),
        out_shape=jax.ShapeDtypeStruct((bsz, lp, GROUP_W), BF16),
        scratch_shapes=[
            pltpu.VMEM((bsz, SUBLANES + BLOCK, 2 * GROUP_W), F32),
            pltpu.VMEM((bsz * HEADS, 2 * DH, DH), F32),
            pltpu.VMEM((bsz * HEADS, 1, LANES), F32),
        ],
        compiler_params=_params(("arbitrary",)),
        name="mlstm",
    )(proj3, proj3, proj3, proj3, gcol, grow, conv_w, norm_g)


FOX_ROWS = 32
FOX_BIG = 1e30


def _split3(c):
    hi = c.astype(BF16).astype(F32)
    r = c - hi
    mid = r.astype(BF16).astype(F32)
    return hi, mid, r - mid


def _fox_kernel(q_ref, k_ref, v_ref, gcol_ref, y_ref, kx_ref, vx_ref, qx_ref, s_ref, p_ref, acc_ref,
                m_ref, al_ref, *, tq, lp):
    qi = pl.program_id(1)
    nq = lp // tq
    lane = lax.broadcasted_iota(jnp.int32, (tq, LANES), 1)
    nt_dims = (((1,), (1,)), ((), ()))

    @pl.when(qi == 0)
    def _():
        def build(cb, carry):
            r0 = pl.multiple_of(cb * tq, tq)
            rows = pl.ds(r0, tq)
            is_token = (r0 + lax.broadcasted_iota(jnp.int32, (tq, 1), 0)) >= PAD
            g = gcol_ref[rows, :]
            ones_col = jnp.where(lane == 0, 1.0, 0.0).astype(BF16)
            for h in range(HEADS):
                cols = slice(h * DH, (h + 1) * DH)
                hi, mid, lo = _split3(jnp.where(is_token, g[:, GC_FF + h:GC_FF + h + 1], FOX_BIG))
                ext = jnp.where(lane < 3, 1.0, jnp.where(lane == 3, -hi, jnp.where(lane == 4, -mid,
                                jnp.where(lane == 5, -lo, 0.0))))
                kx_ref[h, rows, 0:DH] = k_ref[rows, cols]
                kx_ref[h, rows, DH:] = ext.astype(BF16)
                vx_ref[h, rows, 0:DH] = v_ref[rows, cols]
                vx_ref[h, rows, DH:] = ones_col
            return carry

        lax.fori_loop(0, nq, build, 0)

    gq = gcol_ref[pl.ds(pl.multiple_of(qi * tq, tq), tq), :]
    row_io = lax.broadcasted_iota(jnp.int32, (FOX_ROWS, tq), 0)
    col_io = lax.broadcasted_iota(jnp.int32, (FOX_ROWS, tq), 1)

    for h in range(HEADS):
        hi, mid, lo = _split3(gq[:, GC_FF + h:GC_FF + h + 1])
        ext = jnp.where(lane == 0, hi, jnp.where(lane == 1, mid, jnp.where(lane == 2, lo,
                        jnp.where(lane < 6, 1.0, 0.0))))
        qx_ref[h, :, 0:DH] = q_ref[:, h * DH:(h + 1) * DH]
        qx_ref[h, :, DH:] = ext.astype(BF16)
    m_ref[...] = jnp.full(m_ref.shape, NEG, F32)
    acc_ref[...] = jnp.zeros_like(acc_ref)

    def scores(kb, h):
        krows = pl.ds(pl.multiple_of(kb * tq, tq), tq)
        s_ref[h] = lax.dot_general(qx_ref[h], kx_ref[h, krows, :], nt_dims, preferred_element_type=F32)

    def step(kb, diagonal):
        krows = pl.ds(pl.multiple_of(kb * tq, tq), tq)
        for h in range(HEADS):
            for r0 in range(0, tq, FOX_ROWS):
                rr = slice(r0, r0 + FOX_ROWS)
                sc = s_ref[h, rr, :]
                if diagonal:
                    sc = jnp.where(col_io <= row_io + r0, sc, NEG)
                m_old = m_ref[h, rr, :]
                m_new = jnp.maximum(m_old, jnp.max(sc, axis=-1, keepdims=True))
                al_ref[h, rr, :] = jnp.exp(m_old - m_new)
                m_ref[h, rr, :] = m_new
                p_ref[h, rr, :] = jnp.exp(sc - m_new).astype(BF16)
            if not diagonal:
                scores(kb + 1, h)
            pv = jnp.dot(p_ref[h], vx_ref[h, krows, :], preferred_element_type=F32)
            acc_ref[h] = al_ref[h] * acc_ref[h] + pv

    def off_diagonal(kb, carry):
        step(kb, False)
        return carry

    for h in range(HEADS):
        scores(0, h)
    lax.fori_loop(0, qi, off_diagonal, 0)
    step(qi, True)
    for h in range(HEADS):
        acc = acc_ref[h]
        y_ref[:, h * DH:(h + 1) * DH] = (acc[:, 0:DH] / acc[:, DH:DH + 1]).astype(BF16)


def _fox(proj3, gcol):
    bsz, lp, _ = proj3.shape
    tq = _pick(lp, 384, BLOCK)
    nq = lp // tq
    return pl.pallas_call(
        functools.partial(_fox_kernel, tq=tq, lp=lp),
        grid=(bsz, nq),
        in_specs=[
            pl.BlockSpec((None, tq, GROUP_W), lambda b, i: (b, i, COL_FQ)),
            pl.BlockSpec((None, lp, GROUP_W), lambda b, i: (b, 0, COL_FK)),
            pl.BlockSpec((None, lp, GROUP_W), lambda b, i: (b, 0, COL_FV)),
            pl.BlockSpec((None, lp, LANES), lambda b, i: (b, 0, 0)),
        ],
        out_specs=pl.BlockSpec((tq, GROUP_W), lambda b, i: (b * nq + i, 0)),
        out_shape=jax.ShapeDtypeStruct((bsz * lp, GROUP_W), BF16),
        scratch_shapes=[
            pltpu.VMEM((HEADS, lp, 2 * DH), BF16),
            pltpu.VMEM((HEADS, lp, 2 * DH), BF16),
            pltpu.VMEM((HEADS, tq, 2 * DH), BF16),
            pltpu.VMEM((HEADS, tq, tq), F32),
            pltpu.VMEM((HEADS, tq, tq), BF16),
            pltpu.VMEM((HEADS, tq, 2 * DH), F32),
            pltpu.VMEM((HEADS, tq, 1), F32),
            pltpu.VMEM((HEADS, tq, 1), F32),
        ],
        compiler_params=_params(("arbitrary", "arbitrary")),
        name="fox",
    )(proj3, proj3, proj3, gcol)


def _pool_kernel(u_ref, w_ref, sc_ref, y_ref, buf_ref, *, tb):
    t = pl.program_id(1)

    @pl.when(t == 0)
    def _():
        buf_ref[0:POOL_MAXW, :] = jnp.zeros((POOL_MAXW, GROUP_W), F32)

    @pl.when(t > 0)
    def _():
        buf_ref[0:POOL_MAXW, :] = buf_ref[tb:tb + POOL_MAXW, :]

    buf_ref[POOL_MAXW:POOL_MAXW + tb, :] = u_ref[...].astype(F32)
    pos = t * tb + lax.broadcasted_iota(jnp.int32, (tb, 1), 0) - (PAD - 1)
    posf = jnp.maximum(pos, 1).astype(F32)
    gw = GROUP_W // len(POOL_WINDOWS)
    for gi, w in enumerate(POOL_WINDOWS):
        cols = slice(gi * gw, (gi + 1) * gw)
        x = buf_ref[POOL_MAXW:POOL_MAXW + tb, cols]
        s = x
        for j in range(1, w):
            s = s + buf_ref[POOL_MAXW - j:POOL_MAXW - j + tb, cols]
        pooled = s / jnp.minimum(posf, float(w)) - x
        mixed = jnp.dot(pooled.astype(BF16), w_ref[gi], preferred_element_type=F32)
        y_ref[:, cols] = (mixed * sc_ref[:, cols]).astype(BF16)


def _pool(proj, pool_w, pool_scale, bsz, lp):
    tb = _pick(lp, 528, 16)
    nt = lp // tb
    gw = GROUP_W // len(POOL_WINDOWS)
    return pl.pallas_call(
        functools.partial(_pool_kernel, tb=tb),
        grid=(bsz, nt),
        in_specs=[
            pl.BlockSpec((tb, GROUP_W), lambda b, t: (b * nt + t, COL_PU)),
            pl.BlockSpec((len(POOL_WINDOWS), gw, gw), lambda b, t: (0, 0, 0)),
            pl.BlockSpec((1, GROUP_W), lambda b, t: (0, 0)),
        ],
        out_specs=pl.BlockSpec((tb, GROUP_W), lambda b, t: (b * nt + t, 0)),
        out_shape=jax.ShapeDtypeStruct((bsz * lp, GROUP_W), BF16),
        scratch_shapes=[pltpu.VMEM((POOL_MAXW + tb, GROUP_W), F32)],
        compiler_params=_params(("arbitrary", "arbitrary")),
        name="pool",
    )(proj, pool_w, pool_scale)


def _outproj_kernel(ys_ref, ym_ref, yf_ref, yp_ref, w_ref, g_ref, h_ref, o_ref, *, tm, blocks_per_seq):
    mix = jnp.dot(ys_ref[...], w_ref[0:GROUP_W, :], preferred_element_type=F32)
    for n, y_ref in enumerate((ym_ref, yf_ref, yp_ref), start=1):
        mix = mix + jnp.dot(y_ref[...], w_ref[n * GROUP_W:(n + 1) * GROUP_W, :], preferred_element_type=F32)
    off = (pl.program_id(0) % blocks_per_seq) * tm
    valid = (off + lax.broadcasted_iota(jnp.int32, (tm, 1), 0)) >= PAD
    o_ref[...] = jnp.where(valid, h_ref[...] + _rms(mix, g_ref[...]), 0.0)


def _outproj(ys, ym, yf, yp, w_out_all, g, h2, lp, l):
    m = h2.shape[0]
    tm = _pick(lp, 528, 16)
    yspec = pl.BlockSpec((tm, GROUP_W), lambda i: (i, 0))
    return pl.pallas_call(
        functools.partial(_outproj_kernel, tm=tm, blocks_per_seq=lp // tm),
        grid=(m // tm,),
        in_specs=[
            yspec, yspec, yspec, yspec,
            pl.BlockSpec((None, D_MODEL, D_MODEL), lambda i: (l, 0, 0)),
            pl.BlockSpec((1, D_MODEL), lambda i: (0, 0)),
            pl.BlockSpec((tm, D_MODEL), lambda i: (i, 0)),
        ],
        out_specs=pl.BlockSpec((tm, D_MODEL), lambda i: (i, 0)),
        out_shape=jax.ShapeDtypeStruct((m, D_MODEL), F32),
        input_output_aliases={6: 0},
        compiler_params=_params(("arbitrary",)),
        name="outproj",
    )(ys, ym, yf, yp, w_out_all, g, h2)


def _ffn_kernel(h_ref, g1_ref, w1_ref, w2_ref, g2_ref, o_ref, hn_ref, acc_ref):
    f = pl.program_id(1)
    last = pl.num_programs(1) - 1
    tm = h_ref.shape[0]
    halves = (slice(0, tm // 2), slice(tm // 2, tm))

    def update(rows):
        a = jnp.maximum(jnp.dot(hn_ref[rows, :], w1_ref[...], preferred_element_type=F32), 0.0)
        return jnp.dot((a * a).astype(BF16), w2_ref[...], preferred_element_type=F32)

    @pl.when(f == 0)
    def _():
        for rows in halves:
            hn_ref[rows, :] = _rms(h_ref[rows, :], g1_ref[...]).astype(BF16)
            acc_ref[rows, :] = update(rows)

    @pl.when((f > 0) & (f < last))
    def _():
        acc_ref[...] += update(slice(0, tm))

    @pl.when(f == last)
    def _():
        for rows in halves:
            o_ref[rows, :] = h_ref[rows, :] + _rms(acc_ref[rows, :] + update(rows), g2_ref[...])


FFN_TF = 1024


def _ffn_specs(l, tm, h_spec):
    return [
        h_spec,
        pl.BlockSpec((1, D_MODEL), lambda i, f: (0, 0)),
        pl.BlockSpec((None, D_MODEL, FFN_TF), lambda i, f: (l, 0, f)),
        pl.BlockSpec((None, FFN_TF, D_MODEL), lambda i, f: (l, f, 0)),
        pl.BlockSpec((1, D_MODEL), lambda i, f: (0, 0)),
    ], [pltpu.VMEM((tm, D_MODEL), BF16), pltpu.VMEM((tm, D_MODEL), F32)]


def _ffn(h2, g1, w1_all, w2_all, g2, l):
    m = h2.shape[0]
    tm = _pick(m, 768, 16)
    in_specs, scratch = _ffn_specs(l, tm, pl.BlockSpec((tm, D_MODEL), lambda i, f: (i, 0)))
    return pl.pallas_call(
        _ffn_kernel,
        grid=(m // tm, D_FF // FFN_TF),
        in_specs=in_specs,
        out_specs=pl.BlockSpec((tm, D_MODEL), lambda i, f: (i, 0)),
        out_shape=jax.ShapeDtypeStruct((m, D_MODEL), F32),
        scratch_shapes=scratch,
        input_output_aliases={0: 0},
        compiler_params=_params(("arbitrary", "arbitrary")),
        name="ffn",
    )(h2, g1, w1_all, w2_all, g2)


def _ffn_final(h2, g1, w1_all, w2_all, g2, l, bsz, lp):
    seq = lp - PAD - N_META
    tm = _pick(seq, 512, 16)
    per_seq = seq // tm
    h_spec = pl.BlockSpec((pl.Element(tm), pl.Element(D_MODEL)),
                          lambda i, f: (pl.multiple_of((i // per_seq) * lp + PAD + N_META + (i % per_seq) * tm,
                                                       SUBLANES), 0))
    in_specs, scratch = _ffn_specs(l, tm, h_spec)
    return pl.pallas_call(
        _ffn_kernel,
        grid=(bsz * per_seq, D_FF // FFN_TF),
        in_specs=in_specs,
        out_specs=pl.BlockSpec((tm, D_MODEL), lambda i, f: (i, 0)),
        out_shape=jax.ShapeDtypeStruct((bsz * seq, D_MODEL), F32),
        scratch_shapes=scratch,
        compiler_params=_params(("arbitrary", "arbitrary")),
        name="ffn_final",
    )(h2, g1, w1_all, w2_all, g2)


IN_COLS = N_MAIN + 3 * HEADS
IN_COLS_PAD = -(-IN_COLS // LANES) * LANES
O_MI = 5 * GROUP_W
O_FQ = O_MI + 2 * HEADS
O_FF = O_FQ + 3 * GROUP_W
O_PU = O_FF + HEADS


SPLIT_BLOCKS = N_MAIN // LANES
assert O_MI % LANES == 0 and (O_FQ - O_MI) + 3 * GROUP_W == (O_FF - O_MI) and O_FF % LANES == GC_FF


def _split_w_in_kernel(a_ref, b_ref, main_ref, gate_ref):
    d = pl.program_id(0)
    depth = a_ref.shape[1]
    col = lax.broadcasted_iota(jnp.int32, (LANES, D_MODEL), 0)

    def emit(shift):
        for l in range(depth):
            a = a_ref[:, l, :]
            if shift:
                a = pltpu.roll(jnp.where(col >= shift, a, b_ref[:, l, :]), LANES - shift, axis=0)
            main_ref[l] = a.T.astype(BF16)

    wide_before_ml_gates = O_MI // LANES
    wide_before_fx_gate = (O_FF - (O_FQ - O_MI)) // LANES

    @pl.when(d < wide_before_ml_gates)
    def _():
        emit(0)

    @pl.when((d >= wide_before_ml_gates) & (d < wide_before_fx_gate))
    def _():
        emit(O_FQ - O_MI)

    @pl.when(d >= wide_before_fx_gate)
    def _():
        emit(O_PU - O_MI - 3 * GROUP_W)

    @pl.when(d == 0)
    def _():
        gate_ref[...] = jnp.zeros_like(gate_ref)

    @pl.when(d == wide_before_ml_gates)
    def _():
        for l in range(depth):
            gate_ref[l] += jnp.where(col < GC_FF, a_ref[:, l, :], 0.0).T

    @pl.when(d == O_FF // LANES)
    def _():
        for l in range(depth):
            g_fx = jnp.where(col >= GC_FF, jnp.where(col < GC_FF + HEADS, a_ref[:, l, :], 0.0), 0.0)
            gate_ref[l] += g_fx.T


def _split_w_in(w_all):
    depth = w_all.shape[0]
    w_t = jnp.transpose(w_all, (2, 0, 1))
    blk = lambda off: pl.BlockSpec((LANES, depth, D_MODEL), lambda d: (d + off, 0, 0))
    return pl.pallas_call(
        _split_w_in_kernel,
        grid=(SPLIT_BLOCKS,),
        in_specs=[blk(0), blk(1)],
        out_specs=[pl.BlockSpec((depth, D_MODEL, LANES), lambda d: (0, 0, d)),
                   pl.BlockSpec((depth, D_MODEL, LANES), lambda d: (0, 0, 0))],
        out_shape=[jax.ShapeDtypeStruct((depth, D_MODEL, N_MAIN), BF16),
                   jax.ShapeDtypeStruct((depth, D_MODEL, LANES), F32)],
        compiler_params=_params(("arbitrary",)),
        name="split_w_in",
    )(w_t, w_t)


def _s5_matrices(bbre, bbim, c_re, c_im):
    gh = SSM_G // 2
    mask = (jnp.arange(GROUP_W // 2)[:, None] // SSM_H == jnp.arange(S5_HALF)[None, :] // SSM_P).astype(F32)
    c_t = lambda c: c.transpose(0, 2, 1).reshape(SSM_N, SSM_H)
    bmats, cmats = [], []
    for k in range(2):
        st = slice(S5_HALF * k, S5_HALF * (k + 1))
        expand = lambda bb: jnp.tile(bb[:, st], (gh, 1)) * mask
        cont = lambda c: jnp.tile(c_t(c)[st, :], (1, gh)) * mask.T
        bmats.append(jnp.concatenate([expand(bbre), expand(bbim)], axis=1))
        cmats.append(jnp.concatenate([cont(c_re), -cont(c_im)], axis=0))
    return jnp.stack(bmats).astype(BF16), jnp.stack(cmats).astype(BF16)


def kernel(x, meta_tokens, g_pre_mix, g_post_mix, g_pre_ffn, g_post_ffn, w_in, ml_gate_bias, fx_gate_bias, ssm_lam_re, ssm_lam_im, ssm_log_dt, ssm_b_re, ssm_b_im, ssm_c_re, ssm_c_im, ssm_d, ssm_glu_w, ssm_glu_b, ml_conv_w, ml_norm_g, pool_w, pool_scale, w_out, mlp_w1, mlp_w2):
    bsz, seq, d = x.shape
    depth = w_in.shape[0]
    lp = PAD + N_META + seq
    m = bsz * lp
    assert d == D_MODEL and lp % BLOCK == 0
    row = lambda a: a.reshape(1, -1).astype(F32)
    colscale = jnp.ones((1, N_MAIN), F32).at[:, COL_FQ * GROUP_W:(COL_FQ + 1) * GROUP_W].set(DH ** -0.5)
    w_out16, w1_16, w2_16 = w_out.astype(BF16), mlp_w1.astype(BF16), mlp_w2.astype(BF16)

    w_main, w_gate = _split_w_in(w_in)
    w_gate = w_gate.astype(BF16)

    for l in range(depth):
        if l == 0:
            proj, graw, h = _inproj_first(x.reshape(bsz * seq, d).astype(F32), meta_tokens.astype(F32),
                                          row(g_pre_mix[l]), w_main, w_gate, colscale, bsz, lp)
        else:
            proj, graw = _inproj(h, row(g_pre_mix[l]), w_main, w_gate, colscale, l)
        proj3 = proj.reshape(bsz, lp, N_MAIN)

        gbias = jnp.pad(jnp.concatenate([ml_gate_bias[l], fx_gate_bias[l]]).astype(F32), (0, LANES - 3 * HEADS))
        gcol, grow = _gates(graw, gbias.reshape(1, LANES), bsz, lp)

        consts, bbre, bbim = _s5_prep(ssm_lam_re[l], ssm_lam_im[l], ssm_log_dt[l], ssm_b_re[l], ssm_b_im[l], bsz)
        bmat, cmat = _s5_matrices(bbre, bbim, ssm_c_re[l], ssm_c_im[l])
        y_ssm = _s5(proj3, bmat, cmat, row(ssm_d[l]), consts, ssm_glu_w[l].astype(BF16), row(ssm_glu_b[l]))
        y_ml = _mlstm(proj3, gcol, grow, ml_conv_w[l].astype(F32), row(ml_norm_g[l]))
        y_fx = _fox(proj3, gcol)
        y_pool = _pool(proj, pool_w[l].astype(BF16), row(pool_scale[l]), bsz, lp)

        h = _outproj(y_ssm.reshape(m, GROUP_W), y_ml.reshape(m, GROUP_W), y_fx, y_pool,
                     w_out16, row(g_post_mix[l]), h, lp, l)
        if l + 1 < depth:
            h = _ffn(h, row(g_pre_ffn[l]), w1_16, w2_16, row(g_post_ffn[l]), l)
        else:
            h = _ffn_final(h, row(g_pre_ffn[l]), w1_16, w2_16, row(g_post_ffn[l]), l, bsz, lp)

    return h.reshape(bsz, seq, d)
```

```python
import functools
import math

import jax
import jax.numpy as jnp
from jax import lax
from jax.experimental import pallas as pl
from jax.experimental.pallas import tpu as pltpu

F32 = jnp.float32
BF16 = jnp.bfloat16

D_MODEL = 2048
N_META = 16
BLOCK = 128
PAD = BLOCK - N_META
GROUP_W = 512
SSM_H = 16
SSM_G = GROUP_W // SSM_H
SSM_P = 64
SSM_N = SSM_G * SSM_P
HEADS = 4
DH = GROUP_W // HEADS
ML_CONV = 4
POOL_WINDOWS = (2, 4, 8, 16)
POOL_MAXW = 16
D_FF = 4 * D_MODEL
EPS = 1e-6
NEG = -1e30
N_MAIN = 9 * GROUP_W
LANES = 128
SUBLANES = 8
VMEM_LIMIT = 60 * 1024 * 1024

COL_SU, COL_MQ, COL_MK, COL_MV, COL_MO, COL_FQ, COL_FK, COL_FV, COL_PU = range(9)
GC_MI, GC_MF, GC_FF, GC_RM = 0, HEADS, 2 * HEADS, 3 * HEADS


def _pick(n, target, mult):
    best = None
    for d in range(mult, min(n, target) + 1, mult):
        if n % d == 0:
            best = d
    assert best is not None, (n, target, mult)
    return best


def _params(sem):
    return pltpu.CompilerParams(dimension_semantics=sem, vmem_limit_bytes=VMEM_LIMIT)


def _rms(x, g):
    return x * lax.rsqrt(jnp.mean(x * x, axis=-1, keepdims=True) + EPS) * g


def _inproj_steps(src_ref, g_ref, w_ref, wg_ref, cs_ref, o_ref, og_ref, xn_ref):
    tm = src_ref.shape[0]

    def project(rows):
        acc = jnp.dot(xn_ref[rows, :], w_ref[...], preferred_element_type=F32)
        o_ref[rows, :] = (acc * cs_ref[...]).astype(BF16)

    @pl.when(pl.program_id(1) == 0)
    def _():
        for rows in (slice(0, tm // 2), slice(tm // 2, tm)):
            xn = _rms(src_ref[rows, :], g_ref[...]).astype(BF16)
            xn_ref[rows, :] = xn
            og_ref[rows, :] = jnp.dot(xn, wg_ref[...], preferred_element_type=F32)
            project(rows)

    @pl.when(pl.program_id(1) > 0)
    def _():
        project(slice(0, tm))


def _inproj_kernel(x_ref, g_ref, w_ref, wg_ref, cs_ref, o_ref, og_ref, xn_ref):
    _inproj_steps(x_ref, g_ref, w_ref, wg_ref, cs_ref, o_ref, og_ref, xn_ref)


def _inproj(h2, g, w_main, w_gate, colscale, l):
    m = h2.shape[0]
    tm = _pick(m, 1056, 16)
    tn = 1536
    return pl.pallas_call(
        _inproj_kernel,
        grid=(m // tm, N_MAIN // tn),
        in_specs=[
            pl.BlockSpec((tm, D_MODEL), lambda i, j: (i, 0)),
            pl.BlockSpec((1, D_MODEL), lambda i, j: (0, 0)),
            pl.BlockSpec((None, D_MODEL, tn), lambda i, j: (l, 0, j)),
            pl.BlockSpec((None, D_MODEL, LANES), lambda i, j: (l, 0, 0)),
            pl.BlockSpec((1, tn), lambda i, j: (0, j)),
        ],
        out_specs=[
            pl.BlockSpec((tm, tn), lambda i, j: (i, j)),
            pl.BlockSpec((tm, LANES), lambda i, j: (i, 0)),
        ],
        out_shape=[jax.ShapeDtypeStruct((m, N_MAIN), BF16), jax.ShapeDtypeStruct((m, LANES), F32)],
        scratch_shapes=[pltpu.VMEM((tm, D_MODEL), BF16)],
        compiler_params=_params(("arbitrary", "arbitrary")),
        name="inproj",
    )(h2, g, w_main, w_gate, colscale)


def _inproj_first_kernel(x_ref, meta_ref, g_ref, w_ref, wg_ref, cs_ref, o_ref, og_ref, h_ref, xn_ref,
                         *, tm, tiles_per_seq):
    head = PAD + N_META
    first_n = pl.program_id(1) == 0
    seq_start = pl.program_id(0) % tiles_per_seq == 0

    @pl.when(first_n & seq_start)
    def _():
        h_ref[0:PAD, :] = jnp.zeros((PAD, D_MODEL), F32)
        h_ref[PAD:head, :] = meta_ref[...]
        h_ref[head:tm, :] = x_ref[0:tm - head, :]

    @pl.when(first_n & jnp.logical_not(seq_start))
    def _():
        h_ref[...] = x_ref[...]

    _inproj_steps(h_ref, g_ref, w_ref, wg_ref, cs_ref, o_ref, og_ref, xn_ref)


def _inproj_first(x2, meta, g, w_main, w_gate, colscale, bsz, lp, l=0):
    seq = lp - PAD - N_META
    tm = _pick(lp, min(704, lp // 2), 16)
    assert tm > PAD + N_META
    per_seq = lp // tm
    tn = 1536
    m = bsz * lp

    def x_index(i, j):
        b, t = i // per_seq, i % per_seq
        return pl.multiple_of(b * seq + jnp.maximum(t * tm - (PAD + N_META), 0), SUBLANES), 0

    return pl.pallas_call(
        functools.partial(_inproj_first_kernel, tm=tm, tiles_per_seq=per_seq),
        grid=(m // tm, N_MAIN // tn),
        in_specs=[
            pl.BlockSpec((pl.Element(tm), pl.Element(D_MODEL)), x_index),
            pl.BlockSpec((N_META, D_MODEL), lambda i, j: (0, 0)),
            pl.BlockSpec((1, D_MODEL), lambda i, j: (0, 0)),
            pl.BlockSpec((None, D_MODEL, tn), lambda i, j: (l, 0, j)),
            pl.BlockSpec((None, D_MODEL, LANES), lambda i, j: (l, 0, 0)),
            pl.BlockSpec((1, tn), lambda i, j: (0, j)),
        ],
        out_specs=[
            pl.BlockSpec((tm, tn), lambda i, j: (i, j)),
            pl.BlockSpec((tm, LANES), lambda i, j: (i, 0)),
            pl.BlockSpec((tm, D_MODEL), lambda i, j: (i, 0)),
        ],
        out_shape=[jax.ShapeDtypeStruct((m, N_MAIN), BF16), jax.ShapeDtypeStruct((m, LANES), F32),
                   jax.ShapeDtypeStruct((m, D_MODEL), F32)],
        scratch_shapes=[pltpu.VMEM((tm, D_MODEL), BF16)],
        compiler_params=_params(("arbitrary", "arbitrary")),
        name="inproj_first",
    )(x2, meta, g, w_main, w_gate, colscale)


def _gates_kernel(raw_ref, bias_ref, col_ref, row_ref, carry_ref):
    c = pl.program_id(0)

    @pl.when(c == 0)
    def _():
        carry_ref[...] = jnp.zeros_like(carry_ref)

    r_io = lax.broadcasted_iota(jnp.int32, (BLOCK, LANES), 0)
    c_io = lax.broadcasted_iota(jnp.int32, (BLOCK, LANES), 1)
    valid = (r_io + c * BLOCK) >= PAD
    tri = (r_io >= c_io).astype(F32)
    for b in range(raw_ref.shape[0]):
        g = raw_ref[b] + bias_ref[...]
        logsig = jnp.minimum(g, 0.0) - jnp.log(1.0 + jnp.exp(-jnp.abs(g)))
        x = jnp.where(valid, logsig, 0.0)
        x = jnp.where(c_io >= GC_MF, jnp.where(c_io < GC_FF + HEADS, x, 0.0), 0.0)
        cum = jnp.dot(tri, x, precision=lax.Precision.HIGHEST, preferred_element_type=F32)
        cum = cum + jnp.where(c_io >= GC_FF, carry_ref[b], 0.0)
        carry_ref[b] = cum[BLOCK - 1:BLOCK, :]
        log_i = jnp.where(valid, g, NEG)
        r = log_i - pltpu.roll(cum, LANES - GC_MF, axis=1)
        rmax = r
        d = 1
        while d < BLOCK:
            rmax = jnp.maximum(rmax, jnp.where(r_io >= d, pltpu.roll(rmax, d, axis=0), NEG))
            d *= 2
        out = jnp.where(c_io < GC_MF, r, jnp.where(c_io < GC_RM, cum, jnp.where(
            c_io < GC_RM + HEADS, pltpu.roll(rmax, GC_RM, axis=1), 0.0)))
        col_ref[b] = out
        row_ref[b, 0] = out.T[0:2 * SUBLANES, :]


def _gates(raw, bias, bsz, lp):
    nc = lp // BLOCK
    return pl.pallas_call(
        _gates_kernel,
        grid=(nc,),
        in_specs=[
            pl.BlockSpec((bsz, BLOCK, LANES), lambda c: (0, c, 0)),
            pl.BlockSpec((1, LANES), lambda c: (0, 0)),
        ],
        out_specs=[
            pl.BlockSpec((bsz, BLOCK, LANES), lambda c: (0, c, 0)),
            pl.BlockSpec((bsz, 1, 2 * SUBLANES, BLOCK), lambda c: (0, c, 0, 0)),
        ],
        out_shape=[
            jax.ShapeDtypeStruct((bsz, lp, LANES), F32),
            jax.ShapeDtypeStruct((bsz, nc, 2 * SUBLANES, BLOCK), F32),
        ],
        scratch_shapes=[pltpu.VMEM((bsz, 1, LANES), F32)],
        compiler_params=_params(("arbitrary",)),
        name="gates",
    )(raw.reshape(bsz, lp, LANES), bias)


S5_CONST_ROWS = 64
S5_STRIPS = SSM_N // LANES
S5_STRIPS_PER_LOOP = 4
S5_HALF = SSM_N // 2


def _s5_hs_steps(bsz):
    return [d for d in (1, 2, 4) if d < SUBLANES // bsz]


def _s5_prep_kernel(lre_ref, lim_ref, ldt_ref, bre_ref, bim_ref, consts_ref, bbre_ref, bbim_ref, *, bsz):
    lre = lre_ref[...]
    lim = lim_ref[...]
    dt = jnp.exp(ldt_ref[...])
    r_io = lax.broadcasted_iota(jnp.int32, (SUBLANES, SSM_N), 0)
    kk = (r_io + 1).astype(F32)
    mag = jnp.exp(kk * (lre * dt))
    ang = kk * (lim * dt)
    pw_re = mag * jnp.cos(ang)
    pw_im = mag * jnp.sin(ang)
    consts_ref[...] = jnp.zeros_like(consts_ref)
    steps = _s5_hs_steps(bsz)
    for i, d in enumerate(steps):
        keep = r_io >= d * bsz
        consts_ref[16 * i:16 * i + 8, :] = jnp.where(keep, pw_re[d - 1:d, :], 0.0)
        consts_ref[16 * i + 8:16 * i + 16, :] = jnp.where(keep, pw_im[d - 1:d, :], 0.0)
    t_in_vreg = lax.shift_right_logical(r_io, bsz.bit_length() - 1)
    cw_re = jnp.zeros((SUBLANES, SSM_N), F32)
    cw_im = jnp.zeros((SUBLANES, SSM_N), F32)
    for j in range(SUBLANES // bsz):
        cw_re = jnp.where(t_in_vreg == j, pw_re[j:j + 1, :], cw_re)
        cw_im = jnp.where(t_in_vreg == j, pw_im[j:j + 1, :], cw_im)
    base = 16 * len(steps)
    consts_ref[base:base + 8, :] = cw_re
    consts_ref[base + 8:base + 16, :] = cw_im
    nr = pw_re[0:1, :] - 1.0
    ni = pw_im[0:1, :]
    den = lre * lre + lim * lim
    cr = (nr * lre + ni * lim) / den
    ci = (ni * lre - nr * lim) / den
    bre = bre_ref[...]
    bim = bim_ref[...]
    bbre_ref[...] = cr * bre - ci * bim
    bbim_ref[...] = cr * bim + ci * bre


def _s5_prep(lam_re, lam_im, log_dt, b_re, b_im, bsz):
    flat = lambda a: a.reshape(1, SSM_N)
    ldt = jnp.repeat(log_dt, SSM_P).reshape(1, SSM_N)
    bt = lambda a: a.transpose(2, 0, 1).reshape(SSM_H, SSM_N)
    return pl.pallas_call(
        functools.partial(_s5_prep_kernel, bsz=bsz),
        out_shape=[
            jax.ShapeDtypeStruct((S5_CONST_ROWS, SSM_N), F32),
            jax.ShapeDtypeStruct((SSM_H, SSM_N), F32),
            jax.ShapeDtypeStruct((SSM_H, SSM_N), F32),
        ],
        name="s5_prep",
    )(flat(lam_re), flat(lam_im), ldt, bt(b_re), bt(b_im))


def _s5_kernel(u_ref, bmat_ref, cmat_ref, d_ref, consts_ref, gw_ref, gb_ref, y_ref,
               uil_ref, oil_ref, x_ref, carry_ref, *, tb, bsz):
    rows = bsz * tb
    n_us = GROUP_W // LANES

    @pl.when(pl.program_id(0) == 0)
    def _():
        carry_ref[...] = jnp.zeros_like(carry_ref)

    for b in range(bsz):
        ub = u_ref[b].astype(F32)
        for s in range(n_us):
            uil_ref[s, pl.ds(b, tb, stride=bsz), :] = ub[:, s * LANES:(s + 1) * LANES]
    u = jnp.concatenate([uil_ref[s] for s in range(n_us)], axis=1)
    u16 = u.astype(BF16)
    half_w = GROUP_W // 2
    for k in range(2):
        x_ref[:, 2 * S5_HALF * k:2 * S5_HALF * (k + 1)] = jnp.dot(
            u16[:, half_w * k:half_w * (k + 1)], bmat_ref[k], preferred_element_type=F32)

    def cmul_add(xr, xi, ar, ai, sr, si):
        return xr + ar * sr - ai * si, xi + ar * si + ai * sr

    row8 = lax.broadcasted_iota(jnp.int32, (SUBLANES, LANES), 0)
    steps = _s5_hs_steps(bsz)
    cbase = 16 * len(steps)

    def last_step_tile(c):
        span = SUBLANES
        while span > bsz:
            half = span // 2
            c = jnp.where((row8 & (span - 1)) < half, pltpu.roll(c, half, axis=0), c)
            span = half
        return c

    def lanes_of(s):
        k, j = divmod(s, S5_STRIPS // 2)
        re0 = 2 * S5_HALF * k + j * LANES
        return slice(re0, re0 + LANES), slice(re0 + S5_HALF, re0 + S5_HALF + LANES), slice(s * LANES, (s + 1) * LANES)

    for s0 in range(0, S5_STRIPS, S5_STRIPS_PER_LOOP):
        strips = range(s0, s0 + S5_STRIPS_PER_LOOP)

        def body(r, carry, strips=strips):
            r8 = pl.ds(pl.multiple_of(r * SUBLANES, SUBLANES), SUBLANES)
            out = []
            for n, s in enumerate(strips):
                lr, li, lc = lanes_of(s)
                xr = x_ref[r8, lr]
                xi = x_ref[r8, li]
                for i, d in enumerate(steps):
                    xr, xi = cmul_add(xr, xi, consts_ref[16 * i:16 * i + 8, lc], consts_ref[16 * i + 8:16 * i + 16, lc],
                                      pltpu.roll(xr, d * bsz, axis=0), pltpu.roll(xi, d * bsz, axis=0))
                xr, xi = cmul_add(xr, xi, consts_ref[cbase:cbase + 8, lc], consts_ref[cbase + 8:cbase + 16, lc],
                                  carry[2 * n], carry[2 * n + 1])
                x_ref[r8, lr] = xr
                x_ref[r8, li] = xi
                out.append(last_step_tile(xr))
                out.append(last_step_tile(xi))
            return tuple(out)

        init = []
        for s in strips:
            lr, li, _ = lanes_of(s)
            init += [carry_ref[:, lr], carry_ref[:, li]]
        fin = lax.fori_loop(0, rows // SUBLANES, body, tuple(init))
        for n, s in enumerate(strips):
            lr, li, _ = lanes_of(s)
            carry_ref[:, lr] = fin[2 * n]
            carry_ref[:, li] = fin[2 * n + 1]

    y = jnp.concatenate(
        [jnp.dot(x_ref[:, 2 * S5_HALF * k:2 * S5_HALF * (k + 1)].astype(BF16), cmat_ref[k],
                 preferred_element_type=F32) for k in range(2)], axis=1)
    y = jax.nn.gelu(y + d_ref[...] * u, approximate=True)
    z = jnp.dot(y.astype(BF16), gw_ref[...], preferred_element_type=F32) + gb_ref[...]
    out = z[:, :GROUP_W] * jax.nn.sigmoid(z[:, GROUP_W:])
    for s in range(n_us):
        oil_ref[s] = out[:, s * LANES:(s + 1) * LANES]
    for b in range(bsz):
        y_ref[b] = jnp.concatenate([oil_ref[s, pl.ds(b, tb, stride=bsz), :] for s in range(n_us)],
                                   axis=1).astype(BF16)


def _s5(proj3, bmat, cmat, dvec, consts, glu_w, glu_b):
    bsz, lp, _ = proj3.shape
    assert SUBLANES % bsz == 0
    tb = _pick(lp, 176, 16)
    rows = bsz * tb
    const = lambda shape: pl.BlockSpec(shape, lambda t: (0,) * len(shape))
    return pl.pallas_call(
        functools.partial(_s5_kernel, tb=tb, bsz=bsz),
        grid=(lp // tb,),
        in_specs=[
            pl.BlockSpec((bsz, tb, GROUP_W), lambda t: (0, t, COL_SU)),
            const((2, GROUP_W // 2, 2 * S5_HALF)),
            const((2, 2 * S5_HALF, GROUP_W // 2)),
            const((1, GROUP_W)),
            const((S5_CONST_ROWS, SSM_N)),
            const((GROUP_W, 2 * GROUP_W)),
            const((1, 2 * GROUP_W)),
        ],
        out_specs=pl.BlockSpec((bsz, tb, GROUP_W), lambda t: (0, t, 0)),
        out_shape=jax.ShapeDtypeStruct((bsz, lp, GROUP_W), BF16),
        scratch_shapes=[
            pltpu.VMEM((GROUP_W // LANES, rows, LANES), F32),
            pltpu.VMEM((GROUP_W // LANES, rows, LANES), F32),
            pltpu.VMEM((rows, 2 * SSM_N), F32),
            pltpu.VMEM((SUBLANES, 2 * SSM_N), F32),
        ],
        compiler_params=_params(("arbitrary",)),
        name="s5",
    )(proj3, bmat, cmat, dvec, consts, glu_w, glu_b)


def _mlstm_kernel(q_ref, k_ref, v_ref, o_ref, gcol_ref, grow_ref, cw_ref, ng_ref, y_ref,
                  buf_ref, cn_ref, m_ref):
    bsz = q_ref.shape[0]

    @pl.when(pl.program_id(0) == 0)
    def _():
        buf_ref[:, 0:SUBLANES, :] = jnp.zeros((bsz, SUBLANES, 2 * GROUP_W), F32)
        cn_ref[...] = jnp.zeros_like(cn_ref)
        m_ref[...] = jnp.zeros_like(m_ref)

    @pl.when(pl.program_id(0) > 0)
    def _():
        buf_ref[:, 0:SUBLANES, :] = buf_ref[:, BLOCK:BLOCK + SUBLANES, :]

    s_io = lax.broadcasted_iota(jnp.int32, (BLOCK, BLOCK), 0)
    t_io = lax.broadcasted_iota(jnp.int32, (BLOCK, BLOCK), 1)
    causal = s_io <= t_io
    ones_row = jnp.where(s_io == 0, 1.0, 0.0)
    nt_dims = (((1,), (1,)), ((), ()))

    for b in range(bsz):
        buf_ref[b, SUBLANES:SUBLANES + BLOCK, 0:GROUP_W] = q_ref[b].astype(F32)
        buf_ref[b, SUBLANES:SUBLANES + BLOCK, GROUP_W:] = k_ref[b].astype(F32)
        conv = cw_ref[ML_CONV - 1:ML_CONV, :] * buf_ref[b, SUBLANES:SUBLANES + BLOCK, :]
        for j in range(1, ML_CONV):
            conv = conv + cw_ref[ML_CONV - 1 - j:ML_CONV - j, :] * buf_ref[b, SUBLANES - j:SUBLANES - j + BLOCK, :]
        qk = conv * jax.nn.sigmoid(conv)
        gcol = gcol_ref[b]
        grow = grow_ref[b, 0]

        for h in range(HEADS):
            idx = b * HEADS + h
            cols = slice(h * DH, (h + 1) * DH)
            qh = qk[:, cols].astype(BF16)
            kh = (qk[:, GROUP_W + h * DH:GROUP_W + (h + 1) * DH] * (DH ** -0.5)).astype(BF16)
            v_t = jnp.concatenate([v_ref[b, :, cols].astype(F32).T, ones_row], axis=0)
            r_col = gcol[:, GC_MI + h:GC_MI + h + 1]
            r_row = grow[GC_MI + h:GC_MI + h + 1, :]
            b_row = grow[GC_MF + h:GC_MF + h + 1, :]
            rm_row = grow[GC_RM + h:GC_RM + h + 1, :]
            g = b_row[:, BLOCK - 1:BLOCK]
            r_max = rm_row[:, BLOCK - 1:BLOCK]
            cn_prev = cn_ref[idx]
            m_prev = m_ref[idx][:, 0:1]

            big_m = jnp.maximum(m_prev, rm_row)
            decay_t = jnp.where(causal, jnp.exp(r_col - big_m), 0.0)
            qk_t = lax.dot_general(kh, qh, nt_dims, preferred_element_type=F32)
            p_t = (qk_t * decay_t).astype(BF16)
            inter_w = jnp.exp(m_prev - big_m)
            tot = (jnp.dot(v_t.astype(BF16), p_t, preferred_element_type=F32)
                   + inter_w * lax.dot_general(cn_prev.astype(BF16), qh, nt_dims, preferred_element_type=F32))
            num = tot[0:DH, :]
            inv = 1.0 / jnp.maximum(jnp.abs(tot[DH:DH + 1, :]), jnp.exp(-(b_row + big_m)))
            scale = inv * lax.rsqrt(inv * inv * jnp.mean(num * num, axis=0, keepdims=True) + EPS)
            hn = (num * scale).T * ng_ref[:, cols]
            y_ref[b, :, cols] = (hn * jax.nn.sigmoid(o_ref[b, :, cols].astype(F32))).astype(BF16)

            vw_t = (v_t * jnp.exp(r_row - r_max)).astype(BF16)
            cn_loc = jnp.dot(vw_t, kh, preferred_element_type=F32)
            m_keep = jnp.maximum(m_prev, r_max)
            cn_ref[idx] = jnp.exp(m_prev - m_keep) * cn_prev + jnp.exp(r_max - m_keep) * cn_loc
            m_ref[idx] = jnp.broadcast_to(g + m_keep, (1, LANES))


def _mlstm(proj3, gcol, grow, conv_w, norm_g):
    bsz, lp, _ = proj3.shape
    blk = lambda col: pl.BlockSpec((bsz, BLOCK, GROUP_W), lambda c: (0, c, col))
    return pl.pallas_call(
        _mlstm_kernel,
        grid=(lp // BLOCK,),
        in_specs=[
            blk(COL_MQ), blk(COL_MK), blk(COL_MV), blk(COL_MO),
            pl.BlockSpec((bsz, BLOCK, LANES), lambda c: (0, c, 0)),
            pl.BlockSpec((bsz, 1, 2 * SUBLANES, BLOCK), lambda c: (0, c, 0, 0)),
            pl.BlockSpec((ML_CONV, 2 * GROUP_W), lambda c: (0, 0)),
            pl.BlockSpec((1, GROUP_W), lambda c: (0, 0)),
        ],
        out_specs=pl.BlockSpec((bsz, BLOCK, GROUP_W), lambda c: (0, c, 0)),
        out_shape=jax.ShapeDtypeStruct((bsz, lp, GROUP_W), BF16),
        scratch_shapes=[
            pltpu.VMEM((bsz, SUBLANES + BLOCK, 2 * GROUP_W), F32),
            pltpu.VMEM((bsz * HEADS, 2 * DH, DH), F32),
            pltpu.VMEM((bsz * HEADS, 1, LANES), F32),
        ],
        compiler_params=_params(("arbitrary",)),
        name="mlstm",
    )(proj3, proj3, proj3, proj3, gcol, grow, conv_w, norm_g)


FOX_ROWS = 32
FOX_BIG = 1e30


def _split3(c):
    hi = c.astype(BF16).astype(F32)
    r = c - hi
    mid = r.astype(BF16).astype(F32)
    return hi, mid, r - mid


def _fox_kernel(q_ref, k_ref, v_ref, gcol_ref, y_ref, kx_ref, vx_ref, qx_ref, s_ref, p_ref, acc_ref,
                m_ref, al_ref, *, tq, lp):
    qi = pl.program_id(1)
    nq = lp // tq
    lane = lax.broadcasted_iota(jnp.int32, (tq, LANES), 1)
    nt_dims = (((1,), (1,)), ((), ()))

    @pl.when(qi == 0)
    def _():
        def build(cb, carry):
            r0 = pl.multiple_of(cb * tq, tq)
            rows = pl.ds(r0, tq)
            is_token = (r0 + lax.broadcasted_iota(jnp.int32, (tq, 1), 0)) >= PAD
            g = gcol_ref[rows, :]
            ones_col = jnp.where(lane == 0, 1.0, 0.0).astype(BF16)
            for h in range(HEADS):
                cols = slice(h * DH, (h + 1) * DH)
                hi, mid, lo = _split3(jnp.where(is_token, g[:, GC_FF + h:GC_FF + h + 1], FOX_BIG))
                ext = jnp.where(lane < 3, 1.0, jnp.where(lane == 3, -hi, jnp.where(lane == 4, -mid,
                                jnp.where(lane == 5, -lo, 0.0))))
                kx_ref[h, rows, 0:DH] = k_ref[rows, cols]
                kx_ref[h, rows, DH:] = ext.astype(BF16)
                vx_ref[h, rows, 0:DH] = v_ref[rows, cols]
                vx_ref[h, rows, DH:] = ones_col
            return carry

        lax.fori_loop(0, nq, build, 0)

    gq = gcol_ref[pl.ds(pl.multiple_of(qi * tq, tq), tq), :]
    row_io = lax.broadcasted_iota(jnp.int32, (FOX_ROWS, tq), 0)
    col_io = lax.broadcasted_iota(jnp.int32, (FOX_ROWS, tq), 1)

    for h in range(HEADS):
        hi, mid, lo = _split3(gq[:, GC_FF + h:GC_FF + h + 1])
        ext = jnp.where(lane == 0, hi, jnp.where(lane == 1, mid, jnp.where(lane == 2, lo,
                        jnp.where(lane < 6, 1.0, 0.0))))
        qx_ref[h, :, 0:DH] = q_ref[:, h * DH:(h + 1) * DH]
        qx_ref[h, :, DH:] = ext.astype(BF16)
    m_ref[...] = jnp.full(m_ref.shape, NEG, F32)
    acc_ref[...] = jnp.zeros_like(acc_ref)

    def scores(kb, h):
        krows = pl.ds(pl.multiple_of(kb * tq, tq), tq)
        s_ref[h] = lax.dot_general(qx_ref[h], kx_ref[h, krows, :], nt_dims, preferred_element_type=F32)

    def step(kb, diagonal):
        krows = pl.ds(pl.multiple_of(kb * tq, tq), tq)
        for h in range(HEADS):
            for r0 in range(0, tq, FOX_ROWS):
                rr = slice(r0, r0 + FOX_ROWS)
                sc = s_ref[h, rr, :]
                if diagonal:
                    sc = jnp.where(col_io <= row_io + r0, sc, NEG)
                m_old = m_ref[h, rr, :]
                m_new = jnp.maximum(m_old, jnp.max(sc, axis=-1, keepdims=True))
                al_ref[h, rr, :] = jnp.exp(m_old - m_new)
                m_ref[h, rr, :] = m_new
                p_ref[h, rr, :] = jnp.exp(sc - m_new).astype(BF16)
            if not diagonal:
                scores(kb + 1, h)
            pv = jnp.dot(p_ref[h], vx_ref[h, krows, :], preferred_element_type=F32)
            acc_ref[h] = al_ref[h] * acc_ref[h] + pv

    def off_diagonal(kb, carry):
        step(kb, False)
        return carry

    for h in range(HEADS):
        scores(0, h)
    lax.fori_loop(0, qi, off_diagonal, 0)
    step(qi, True)
    for h in range(HEADS):
        acc = acc_ref[h]
        y_ref[:, h * DH:(h + 1) * DH] = (acc[:, 0:DH] / acc[:, DH:DH + 1]).astype(BF16)


def _fox(proj3, gcol):
    bsz, lp, _ = proj3.shape
    tq = _pick(lp, 384, BLOCK)
    nq = lp // tq
    return pl.pallas_call(
        functools.partial(_fox_kernel, tq=tq, lp=lp),
        grid=(bsz, nq),
        in_specs=[
            pl.BlockSpec((None, tq, GROUP_W), lambda b, i: (b, i, COL_FQ)),
            pl.BlockSpec((None, lp, GROUP_W), lambda b, i: (b, 0, COL_FK)),
            pl.BlockSpec((None, lp, GROUP_W), lambda b, i: (b, 0, COL_FV)),
            pl.BlockSpec((None, lp, LANES), lambda b, i: (b, 0, 0)),
        ],
        out_specs=pl.BlockSpec((tq, GROUP_W), lambda b, i: (b * nq + i, 0)),
        out_shape=jax.ShapeDtypeStruct((bsz * lp, GROUP_W), BF16),
        scratch_shapes=[
            pltpu.VMEM((HEADS, lp, 2 * DH), BF16),
            pltpu.VMEM((HEADS, lp, 2 * DH), BF16),
            pltpu.VMEM((HEADS, tq, 2 * DH), BF16),
            pltpu.VMEM((HEADS, tq, tq), F32),
            pltpu.VMEM((HEADS, tq, tq), BF16),
            pltpu.VMEM((HEADS, tq, 2 * DH), F32),
            pltpu.VMEM((HEADS, tq, 1), F32),
            pltpu.VMEM((HEADS, tq, 1), F32),
        ],
        compiler_params=_params(("arbitrary", "arbitrary")),
        name="fox",
    )(proj3, proj3, proj3, gcol)


def _pool_rows(u_ref, pw_ref, ps_ref, buf_ref, t, tm):
    @pl.when(t == 0)
    def _():
        buf_ref[0:POOL_MAXW, :] = jnp.zeros((POOL_MAXW, GROUP_W), F32)

    @pl.when(t > 0)
    def _():
        buf_ref[0:POOL_MAXW, :] = buf_ref[tm:tm + POOL_MAXW, :]

    buf_ref[POOL_MAXW:POOL_MAXW + tm, :] = u_ref[...].astype(F32)
    pos = t * tm + lax.broadcasted_iota(jnp.int32, (tm, 1), 0) - (PAD - 1)
    posf = jnp.maximum(pos, 1).astype(F32)
    gw = GROUP_W // len(POOL_WINDOWS)
    out = []
    for gi, w in enumerate(POOL_WINDOWS):
        cols = slice(gi * gw, (gi + 1) * gw)
        x = buf_ref[POOL_MAXW:POOL_MAXW + tm, cols]
        s = x
        for j in range(1, w):
            s = s + buf_ref[POOL_MAXW - j:POOL_MAXW - j + tm, cols]
        pooled = s / jnp.minimum(posf, float(w)) - x
        mixed = jnp.dot(pooled.astype(BF16), pw_ref[gi], preferred_element_type=F32)
        out.append((mixed * ps_ref[:, cols]).astype(BF16))
    return jnp.concatenate(out, axis=1)


def _outproj_kernel(ys_ref, ym_ref, yf_ref, pu_ref, pw_ref, ps_ref, w_ref, g_ref, h_ref, o_ref, buf_ref,
                    *, tm, blocks_per_seq):
    t = pl.program_id(0) % blocks_per_seq
    y_pool = _pool_rows(pu_ref, pw_ref, ps_ref, buf_ref, t, tm)
    mix = jnp.dot(ys_ref[...], w_ref[0:GROUP_W, :], preferred_element_type=F32)
    for n, y_ref in enumerate((ym_ref, yf_ref), start=1):
        mix = mix + jnp.dot(y_ref[...], w_ref[n * GROUP_W:(n + 1) * GROUP_W, :], preferred_element_type=F32)
    mix = mix + jnp.dot(y_pool, w_ref[3 * GROUP_W:4 * GROUP_W, :], preferred_element_type=F32)
    valid = (t * tm + lax.broadcasted_iota(jnp.int32, (tm, 1), 0)) >= PAD
    o_ref[...] = jnp.where(valid, h_ref[...] + _rms(mix, g_ref[...]), 0.0)


def _outproj(ys, ym, yf, proj, pool_w, pool_scale, w_out_all, g, h2, lp, l):
    m = h2.shape[0]
    tm = _pick(lp, 528, 16)
    gw = GROUP_W // len(POOL_WINDOWS)
    yspec = pl.BlockSpec((tm, GROUP_W), lambda i: (i, 0))
    return pl.pallas_call(
        functools.partial(_outproj_kernel, tm=tm, blocks_per_seq=lp // tm),
        grid=(m // tm,),
        in_specs=[
            yspec, yspec, yspec,
            pl.BlockSpec((tm, GROUP_W), lambda i: (i, COL_PU)),
            pl.BlockSpec((len(POOL_WINDOWS), gw, gw), lambda i: (0, 0, 0)),
            pl.BlockSpec((1, GROUP_W), lambda i: (0, 0)),
            pl.BlockSpec((None, D_MODEL, D_MODEL), lambda i: (l, 0, 0)),
            pl.BlockSpec((1, D_MODEL), lambda i: (0, 0)),
            pl.BlockSpec((tm, D_MODEL), lambda i: (i, 0)),
        ],
        out_specs=pl.BlockSpec((tm, D_MODEL), lambda i: (i, 0)),
        out_shape=jax.ShapeDtypeStruct((m, D_MODEL), F32),
        scratch_shapes=[pltpu.VMEM((POOL_MAXW + tm, GROUP_W), F32)],
        input_output_aliases={8: 0},
        compiler_params=_params(("arbitrary",)),
        name="outproj",
    )(ys, ym, yf, proj, pool_w, pool_scale, w_out_all, g, h2)


def _ffn_kernel(h_ref, g1_ref, w1_ref, w2_ref, g2_ref, o_ref, hn_ref, acc_ref):
    f = pl.program_id(1)
    last = pl.num_programs(1) - 1
    tm = h_ref.shape[0]
    halves = (slice(0, tm // 2), slice(tm // 2, tm))

    def update(rows):
        a = jnp.maximum(jnp.dot(hn_ref[rows, :], w1_ref[...], preferred_element_type=F32), 0.0)
        return jnp.dot((a * a).astype(BF16), w2_ref[...], preferred_element_type=F32)

    @pl.when(f == 0)
    def _():
        for rows in halves:
            hn_ref[rows, :] = _rms(h_ref[rows, :], g1_ref[...]).astype(BF16)
            acc_ref[rows, :] = update(rows)

    @pl.when((f > 0) & (f < last))
    def _():
        acc_ref[...] += update(slice(0, tm))

    @pl.when(f == last)
    def _():
        for rows in halves:
            o_ref[rows, :] = h_ref[rows, :] + _rms(acc_ref[rows, :] + update(rows), g2_ref[...])


FFN_TF = 1024


def _ffn_specs(l, tm, h_spec):
    return [
        h_spec,
        pl.BlockSpec((1, D_MODEL), lambda i, f: (0, 0)),
        pl.BlockSpec((None, D_MODEL, FFN_TF), lambda i, f: (l, 0, f)),
        pl.BlockSpec((None, FFN_TF, D_MODEL), lambda i, f: (l, f, 0)),
        pl.BlockSpec((1, D_MODEL), lambda i, f: (0, 0)),
    ], [pltpu.VMEM((tm, D_MODEL), BF16), pltpu.VMEM((tm, D_MODEL), F32)]


def _ffn(h2, g1, w1_all, w2_all, g2, l):
    m = h2.shape[0]
    tm = _pick(m, 768, 16)
    in_specs, scratch = _ffn_specs(l, tm, pl.BlockSpec((tm, D_MODEL), lambda i, f: (i, 0)))
    return pl.pallas_call(
        _ffn_kernel,
        grid=(m // tm, D_FF // FFN_TF),
        in_specs=in_specs,
        out_specs=pl.BlockSpec((tm, D_MODEL), lambda i, f: (i, 0)),
        out_shape=jax.ShapeDtypeStruct((m, D_MODEL), F32),
        scratch_shapes=scratch,
        input_output_aliases={0: 0},
        compiler_params=_params(("arbitrary", "arbitrary")),
        name="ffn",
    )(h2, g1, w1_all, w2_all, g2)


def _ffn_final(h2, g1, w1_all, w2_all, g2, l, bsz, lp):
    seq = lp - PAD - N_META
    tm = _pick(seq, 512, 16)
    per_seq = seq // tm
    h_spec = pl.BlockSpec((pl.Element(tm), pl.Element(D_MODEL)),
                          lambda i, f: (pl.multiple_of((i // per_seq) * lp + PAD + N_META + (i % per_seq) * tm,
                                                       SUBLANES), 0))
    in_specs, scratch = _ffn_specs(l, tm, h_spec)
    return pl.pallas_call(
        _ffn_kernel,
        grid=(bsz * per_seq, D_FF // FFN_TF),
        in_specs=in_specs,
        out_specs=pl.BlockSpec((tm, D_MODEL), lambda i, f: (i, 0)),
        out_shape=jax.ShapeDtypeStruct((bsz * seq, D_MODEL), F32),
        scratch_shapes=scratch,
        compiler_params=_params(("arbitrary", "arbitrary")),
        name="ffn_final",
    )(h2, g1, w1_all, w2_all, g2)


IN_COLS = N_MAIN + 3 * HEADS
IN_COLS_PAD = -(-IN_COLS // LANES) * LANES
O_MI = 5 * GROUP_W
O_FQ = O_MI + 2 * HEADS
O_FF = O_FQ + 3 * GROUP_W
O_PU = O_FF + HEADS


SPLIT_BLOCKS = N_MAIN // LANES
assert O_MI % LANES == 0 and (O_FQ - O_MI) + 3 * GROUP_W == (O_FF - O_MI) and O_FF % LANES == GC_FF


def _split_w_in_kernel(a_ref, b_ref, main_ref, gate_ref):
    d = pl.program_id(0)
    depth = a_ref.shape[1]
    col = lax.broadcasted_iota(jnp.int32, (LANES, D_MODEL), 0)

    def emit(shift):
        for l in range(depth):
            a = a_ref[:, l, :]
            if shift:
                a = pltpu.roll(jnp.where(col >= shift, a, b_ref[:, l, :]), LANES - shift, axis=0)
            main_ref[l] = a.T.astype(BF16)

    wide_before_ml_gates = O_MI // LANES
    wide_before_fx_gate = (O_FF - (O_FQ - O_MI)) // LANES

    @pl.when(d < wide_before_ml_gates)
    def _():
        emit(0)

    @pl.when((d >= wide_before_ml_gates) & (d < wide_before_fx_gate))
    def _():
        emit(O_FQ - O_MI)

    @pl.when(d >= wide_before_fx_gate)
    def _():
        emit(O_PU - O_MI - 3 * GROUP_W)

    @pl.when(d == 0)
    def _():
        gate_ref[...] = jnp.zeros_like(gate_ref)

    @pl.when(d == wide_before_ml_gates)
    def _():
        for l in range(depth):
            gate_ref[l] += jnp.where(col < GC_FF, a_ref[:, l, :], 0.0).T

    @pl.when(d == O_FF // LANES)
    def _():
        for l in range(depth):
            g_fx = jnp.where(col >= GC_FF, jnp.where(col < GC_FF + HEADS, a_ref[:, l, :], 0.0), 0.0)
            gate_ref[l] += g_fx.T


def _split_w_in(w_all):
    depth = w_all.shape[0]
    w_t = jnp.transpose(w_all, (2, 0, 1))
    blk = lambda off: pl.BlockSpec((LANES, depth, D_MODEL), lambda d: (d + off, 0, 0))
    return pl.pallas_call(
        _split_w_in_kernel,
        grid=(SPLIT_BLOCKS,),
        in_specs=[blk(0), blk(1)],
        out_specs=[pl.BlockSpec((depth, D_MODEL, LANES), lambda d: (0, 0, d)),
                   pl.BlockSpec((depth, D_MODEL, LANES), lambda d: (0, 0, 0))],
        out_shape=[jax.ShapeDtypeStruct((depth, D_MODEL, N_MAIN), BF16),
                   jax.ShapeDtypeStruct((depth, D_MODEL, LANES), F32)],
        compiler_params=_params(("arbitrary",)),
        name="split_w_in",
    )(w_t, w_t)


def _s5_matrices(bbre, bbim, c_re, c_im):
    gh = SSM_G // 2
    mask = (jnp.arange(GROUP_W // 2)[:, None] // SSM_H == jnp.arange(S5_HALF)[None, :] // SSM_P).astype(F32)
    c_t = lambda c: c.transpose(0, 2, 1).reshape(SSM_N, SSM_H)
    bmats, cmats = [], []
    for k in range(2):
        st = slice(S5_HALF * k, S5_HALF * (k + 1))
        expand = lambda bb: jnp.tile(bb[:, st], (gh, 1)) * mask
        cont = lambda c: jnp.tile(c_t(c)[st, :], (1, gh)) * mask.T
        bmats.append(jnp.concatenate([expand(bbre), expand(bbim)], axis=1))
        cmats.append(jnp.concatenate([cont(c_re), -cont(c_im)], axis=0))
    return jnp.stack(bmats).astype(BF16), jnp.stack(cmats).astype(BF16)


def kernel(x, meta_tokens, g_pre_mix, g_post_mix, g_pre_ffn, g_post_ffn, w_in, ml_gate_bias, fx_gate_bias, ssm_lam_re, ssm_lam_im, ssm_log_dt, ssm_b_re, ssm_b_im, ssm_c_re, ssm_c_im, ssm_d, ssm_glu_w, ssm_glu_b, ml_conv_w, ml_norm_g, pool_w, pool_scale, w_out, mlp_w1, mlp_w2):
    bsz, seq, d = x.shape
    depth = w_in.shape[0]
    lp = PAD + N_META + seq
    m = bsz * lp
    assert d == D_MODEL and lp % BLOCK == 0
    row = lambda a: a.reshape(1, -1).astype(F32)
    colscale = jnp.ones((1, N_MAIN), F32).at[:, COL_FQ * GROUP_W:(COL_FQ + 1) * GROUP_W].set(DH ** -0.5)
    w_out16, w1_16, w2_16 = w_out.astype(BF16), mlp_w1.astype(BF16), mlp_w2.astype(BF16)

    w_main, w_gate = _split_w_in(w_in)
    w_gate = w_gate.astype(BF16)

    for l in range(depth):
        if l == 0:
            proj, graw, h = _inproj_first(x.reshape(bsz * seq, d).astype(F32), meta_tokens.astype(F32),
                                          row(g_pre_mix[l]), w_main, w_gate, colscale, bsz, lp)
        else:
            proj, graw = _inproj(h, row(g_pre_mix[l]), w_main, w_gate, colscale, l)
        proj3 = proj.reshape(bsz, lp, N_MAIN)

        gbias = jnp.pad(jnp.concatenate([ml_gate_bias[l], fx_gate_bias[l]]).astype(F32), (0, LANES - 3 * HEADS))
        gcol, grow = _gates(graw, gbias.reshape(1, LANES), bsz, lp)

        consts, bbre, bbim = _s5_prep(ssm_lam_re[l], ssm_lam_im[l], ssm_log_dt[l], ssm_b_re[l], ssm_b_im[l], bsz)
        bmat, cmat = _s5_matrices(bbre, bbim, ssm_c_re[l], ssm_c_im[l])
        y_ssm = _s5(proj3, bmat, cmat, row(ssm_d[l]), consts, ssm_glu_w[l].astype(BF16), row(ssm_glu_b[l]))
        y_ml = _mlstm(proj3, gcol, grow, ml_conv_w[l].astype(F32), row(ml_norm_g[l]))
        y_fx = _fox(proj3, gcol)

        h = _outproj(y_ssm.reshape(m, GROUP_W), y_ml.reshape(m, GROUP_W), y_fx,
                     proj, pool_w[l].astype(BF16), row(pool_scale[l]),
                     w_out16, row(g_post_mix[l]), h, lp, l)
        if l + 1 < depth:
            h = _ffn(h, row(g_pre_ffn[l]), w1_16, w2_16, row(g_post_ffn[l]), l)
        else:
            h = _ffn_final(h, row(g_pre_ffn[l]), w1_16, w2_16, row(g_post_ffn[l]), l, bsz, lp)

    return h.reshape(bsz, seq, d)
```

```python
import functools
import math

import jax
import jax.numpy as jnp
from jax import lax
from jax.experimental import pallas as pl
from jax.experimental.pallas import tpu as pltpu

F32 = jnp.float32
BF16 = jnp.bfloat16

D_MODEL = 2048
N_META = 16
BLOCK = 128
PAD = BLOCK - N_META
GROUP_W = 512
SSM_H = 16
SSM_G = GROUP_W // SSM_H
SSM_P = 64
SSM_N = SSM_G * SSM_P
HEADS = 4
DH = GROUP_W // HEADS
ML_CONV = 4
POOL_WINDOWS = (2, 4, 8, 16)
POOL_MAXW = 16
D_FF = 4 * D_MODEL
EPS = 1e-6
NEG = -1e30
N_MAIN = 9 * GROUP_W
LANES = 128
SUBLANES = 8
VMEM_LIMIT = 60 * 1024 * 1024

COL_SU, COL_MQ, COL_MK, COL_MV, COL_MO, COL_FQ, COL_FK, COL_FV, COL_PU = range(9)
GC_MI, GC_MF, GC_FF, GC_RM = 0, HEADS, 2 * HEADS, 3 * HEADS


def _pick(n, target, mult):
    best = None
    for d in range(mult, min(n, target) + 1, mult):
        if n % d == 0:
            best = d
    assert best is not None, (n, target, mult)
    return best


def _params(sem):
    return pltpu.CompilerParams(dimension_semantics=sem, vmem_limit_bytes=VMEM_LIMIT)


def _rms(x, g):
    return x * lax.rsqrt(jnp.mean(x * x, axis=-1, keepdims=True) + EPS) * g


def _inproj_steps(src_ref, g_ref, w_ref, wg_ref, cs_ref, o_ref, og_ref, xn_ref):
    tm = src_ref.shape[0]

    def project(rows):
        acc = jnp.dot(xn_ref[rows, :], w_ref[...], preferred_element_type=F32)
        o_ref[rows, :] = (acc * cs_ref[...]).astype(BF16)

    @pl.when(pl.program_id(1) == 0)
    def _():
        for rows in (slice(0, tm // 2), slice(tm // 2, tm)):
            xn = _rms(src_ref[rows, :], g_ref[...]).astype(BF16)
            xn_ref[rows, :] = xn
            og_ref[rows, :] = jnp.dot(xn, wg_ref[...], preferred_element_type=F32)
            project(rows)

    @pl.when(pl.program_id(1) > 0)
    def _():
        project(slice(0, tm))


def _inproj_kernel(x_ref, g_ref, w_ref, wg_ref, cs_ref, o_ref, og_ref, xn_ref):
    _inproj_steps(x_ref, g_ref, w_ref, wg_ref, cs_ref, o_ref, og_ref, xn_ref)


def _inproj(h2, g, w_main, w_gate, colscale, l):
    m = h2.shape[0]
    tm = _pick(m, 1056, 16)
    tn = 1536
    return pl.pallas_call(
        _inproj_kernel,
        grid=(m // tm, N_MAIN // tn),
        in_specs=[
            pl.BlockSpec((tm, D_MODEL), lambda i, j: (i, 0)),
            pl.BlockSpec((1, D_MODEL), lambda i, j: (0, 0)),
            pl.BlockSpec((None, D_MODEL, tn), lambda i, j: (l, 0, j)),
            pl.BlockSpec((None, D_MODEL, LANES), lambda i, j: (l, 0, 0)),
            pl.BlockSpec((1, tn), lambda i, j: (0, j)),
        ],
        out_specs=[
            pl.BlockSpec((tm, tn), lambda i, j: (i, j)),
            pl.BlockSpec((tm, LANES), lambda i, j: (i, 0)),
        ],
        out_shape=[jax.ShapeDtypeStruct((m, N_MAIN), BF16), jax.ShapeDtypeStruct((m, LANES), F32)],
        scratch_shapes=[pltpu.VMEM((tm, D_MODEL), BF16)],
        compiler_params=_params(("arbitrary", "arbitrary")),
        name="inproj",
    )(h2, g, w_main, w_gate, colscale)


def _inproj_first_kernel(x_ref, meta_ref, g_ref, w_ref, wg_ref, cs_ref, o_ref, og_ref, h_ref, xn_ref,
                         *, tm, tiles_per_seq):
    head = PAD + N_META
    first_n = pl.program_id(1) == 0
    seq_start = pl.program_id(0) % tiles_per_seq == 0

    @pl.when(first_n & seq_start)
    def _():
        h_ref[0:PAD, :] = jnp.zeros((PAD, D_MODEL), F32)
        h_ref[PAD:head, :] = meta_ref[...]
        h_ref[head:tm, :] = x_ref[0:tm - head, :]

    @pl.when(first_n & jnp.logical_not(seq_start))
    def _():
        h_ref[...] = x_ref[...]

    _inproj_steps(h_ref, g_ref, w_ref, wg_ref, cs_ref, o_ref, og_ref, xn_ref)


def _inproj_first(x2, meta, g, w_main, w_gate, colscale, bsz, lp, l=0):
    seq = lp - PAD - N_META
    tm = _pick(lp, min(704, lp // 2), 16)
    assert tm > PAD + N_META
    per_seq = lp // tm
    tn = 1536
    m = bsz * lp

    def x_index(i, j):
        b, t = i // per_seq, i % per_seq
        return pl.multiple_of(b * seq + jnp.maximum(t * tm - (PAD + N_META), 0), SUBLANES), 0

    return pl.pallas_call(
        functools.partial(_inproj_first_kernel, tm=tm, tiles_per_seq=per_seq),
        grid=(m // tm, N_MAIN // tn),
        in_specs=[
            pl.BlockSpec((pl.Element(tm), pl.Element(D_MODEL)), x_index),
            pl.BlockSpec((N_META, D_MODEL), lambda i, j: (0, 0)),
            pl.BlockSpec((1, D_MODEL), lambda i, j: (0, 0)),
            pl.BlockSpec((None, D_MODEL, tn), lambda i, j: (l, 0, j)),
            pl.BlockSpec((None, D_MODEL, LANES), lambda i, j: (l, 0, 0)),
            pl.BlockSpec((1, tn), lambda i, j: (0, j)),
        ],
        out_specs=[
            pl.BlockSpec((tm, tn), lambda i, j: (i, j)),
            pl.BlockSpec((tm, LANES), lambda i, j: (i, 0)),
            pl.BlockSpec((tm, D_MODEL), lambda i, j: (i, 0)),
        ],
        out_shape=[jax.ShapeDtypeStruct((m, N_MAIN), BF16), jax.ShapeDtypeStruct((m, LANES), F32),
                   jax.ShapeDtypeStruct((m, D_MODEL), F32)],
        scratch_shapes=[pltpu.VMEM((tm, D_MODEL), BF16)],
        compiler_params=_params(("arbitrary", "arbitrary")),
        name="inproj_first",
    )(x2, meta, g, w_main, w_gate, colscale)


def _gates_kernel(raw_ref, bias_ref, col_ref, row_ref, carry_ref):
    c = pl.program_id(0)

    @pl.when(c == 0)
    def _():
        carry_ref[...] = jnp.zeros_like(carry_ref)

    r_io = lax.broadcasted_iota(jnp.int32, (BLOCK, LANES), 0)
    c_io = lax.broadcasted_iota(jnp.int32, (BLOCK, LANES), 1)
    valid = (r_io + c * BLOCK) >= PAD
    tri = (r_io >= c_io).astype(F32)
    for b in range(raw_ref.shape[0]):
        g = raw_ref[b] + bias_ref[...]
        logsig = jnp.minimum(g, 0.0) - jnp.log(1.0 + jnp.exp(-jnp.abs(g)))
        x = jnp.where(valid, logsig, 0.0)
        x = jnp.where(c_io >= GC_MF, jnp.where(c_io < GC_FF + HEADS, x, 0.0), 0.0)
        cum = jnp.dot(tri, x, precision=lax.Precision.HIGHEST, preferred_element_type=F32)
        cum = cum + jnp.where(c_io >= GC_FF, carry_ref[b], 0.0)
        carry_ref[b] = cum[BLOCK - 1:BLOCK, :]
        log_i = jnp.where(valid, g, NEG)
        r = log_i - pltpu.roll(cum, LANES - GC_MF, axis=1)
        rmax = r
        d = 1
        while d < BLOCK:
            rmax = jnp.maximum(rmax, jnp.where(r_io >= d, pltpu.roll(rmax, d, axis=0), NEG))
            d *= 2
        out = jnp.where(c_io < GC_MF, r, jnp.where(c_io < GC_RM, cum, jnp.where(
            c_io < GC_RM + HEADS, pltpu.roll(rmax, GC_RM, axis=1), 0.0)))
        col_ref[b] = out
        row_ref[b, 0] = out.T[0:2 * SUBLANES, :]


def _gates(raw, bias, bsz, lp):
    nc = lp // BLOCK
    return pl.pallas_call(
        _gates_kernel,
        grid=(nc,),
        in_specs=[
            pl.BlockSpec((bsz, BLOCK, LANES), lambda c: (0, c, 0)),
            pl.BlockSpec((1, LANES), lambda c: (0, 0)),
        ],
        out_specs=[
            pl.BlockSpec((bsz, BLOCK, LANES), lambda c: (0, c, 0)),
            pl.BlockSpec((bsz, 1, 2 * SUBLANES, BLOCK), lambda c: (0, c, 0, 0)),
        ],
        out_shape=[
            jax.ShapeDtypeStruct((bsz, lp, LANES), F32),
            jax.ShapeDtypeStruct((bsz, nc, 2 * SUBLANES, BLOCK), F32),
        ],
        scratch_shapes=[pltpu.VMEM((bsz, 1, LANES), F32)],
        compiler_params=_params(("arbitrary",)),
        name="gates",
    )(raw.reshape(bsz, lp, LANES), bias)


S5_CONST_ROWS = 64
S5_STRIPS = SSM_N // LANES
S5_STRIPS_PER_LOOP = 2
S5_HALF = SSM_N // 2


def _s5_hs_steps(bsz):
    return [d for d in (1, 2, 4) if d < SUBLANES // bsz]


def _s5_prep_kernel(lre_ref, lim_ref, ldt_ref, bre_ref, bim_ref, consts_ref, bbre_ref, bbim_ref, *, bsz):
    lre = lre_ref[...]
    lim = lim_ref[...]
    dt = jnp.exp(ldt_ref[...])
    r_io = lax.broadcasted_iota(jnp.int32, (SUBLANES, SSM_N), 0)
    kk = (r_io + 1).astype(F32)
    mag = jnp.exp(kk * (lre * dt))
    ang = kk * (lim * dt)
    pw_re = mag * jnp.cos(ang)
    pw_im = mag * jnp.sin(ang)
    consts_ref[...] = jnp.zeros_like(consts_ref)
    steps = _s5_hs_steps(bsz)
    for i, d in enumerate(steps):
        keep = r_io >= d * bsz
        consts_ref[16 * i:16 * i + 8, :] = jnp.where(keep, pw_re[d - 1:d, :], 0.0)
        consts_ref[16 * i + 8:16 * i + 16, :] = jnp.where(keep, pw_im[d - 1:d, :], 0.0)
    t_in_vreg = lax.shift_right_logical(r_io, bsz.bit_length() - 1)
    cw_re = jnp.zeros((SUBLANES, SSM_N), F32)
    cw_im = jnp.zeros((SUBLANES, SSM_N), F32)
    for j in range(SUBLANES // bsz):
        cw_re = jnp.where(t_in_vreg == j, pw_re[j:j + 1, :], cw_re)
        cw_im = jnp.where(t_in_vreg == j, pw_im[j:j + 1, :], cw_im)
    base = 16 * len(steps)
    consts_ref[base:base + 8, :] = cw_re
    consts_ref[base + 8:base + 16, :] = cw_im
    nr = pw_re[0:1, :] - 1.0
    ni = pw_im[0:1, :]
    den = lre * lre + lim * lim
    cr = (nr * lre + ni * lim) / den
    ci = (ni * lre - nr * lim) / den
    bre = bre_ref[...]
    bim = bim_ref[...]
    bbre_ref[...] = cr * bre - ci * bim
    bbim_ref[...] = cr * bim + ci * bre


def _s5_prep(lam_re, lam_im, log_dt, b_re, b_im, bsz):
    flat = lambda a: a.reshape(1, SSM_N)
    ldt = jnp.repeat(log_dt, SSM_P).reshape(1, SSM_N)
    bt = lambda a: a.transpose(2, 0, 1).reshape(SSM_H, SSM_N)
    return pl.pallas_call(
        functools.partial(_s5_prep_kernel, bsz=bsz),
        out_shape=[
            jax.ShapeDtypeStruct((S5_CONST_ROWS, SSM_N), F32),
            jax.ShapeDtypeStruct((SSM_H, SSM_N), F32),
            jax.ShapeDtypeStruct((SSM_H, SSM_N), F32),
        ],
        name="s5_prep",
    )(flat(lam_re), flat(lam_im), ldt, bt(b_re), bt(b_im))


def _s5_kernel(u_ref, bmat_ref, cmat_ref, d_ref, consts_ref, gw_ref, gb_ref, y_ref,
               uil_ref, oil_ref, x_ref, carry_ref, *, tb, bsz):
    rows = bsz * tb
    n_us = GROUP_W // LANES

    @pl.when(pl.program_id(0) == 0)
    def _():
        carry_ref[...] = jnp.zeros_like(carry_ref)

    for b in range(bsz):
        ub = u_ref[b].astype(F32)
        for s in range(n_us):
            uil_ref[s, pl.ds(b, tb, stride=bsz), :] = ub[:, s * LANES:(s + 1) * LANES]
    u = jnp.concatenate([uil_ref[s] for s in range(n_us)], axis=1)
    u16 = u.astype(BF16)
    half_w = GROUP_W // 2
    n_groups = S5_STRIPS // S5_STRIPS_PER_LOOP
    gw_lanes = S5_STRIPS_PER_LOOP * LANES

    def group_lanes(g):
        k, j = divmod(g, n_groups // 2)
        re0 = j * gw_lanes
        return k, slice(re0, re0 + gw_lanes), slice(S5_HALF + re0, S5_HALF + re0 + gw_lanes)

    def expand(g):
        k, re, im = group_lanes(g)
        uk = u16[:, half_w * k:half_w * (k + 1)]
        base = 2 * S5_HALF * k
        for part in (re, im):
            x_ref[:, base + part.start:base + part.stop] = jnp.dot(
                uk, bmat_ref[k, :, part], preferred_element_type=F32)

    def contract(g):
        k, re, im = group_lanes(g)
        base = 2 * S5_HALF * k
        return sum(jnp.dot(x_ref[:, base + part.start:base + part.stop].astype(BF16), cmat_ref[k, part, :],
                           preferred_element_type=F32) for part in (re, im))

    def cmul_add(xr, xi, ar, ai, sr, si):
        return xr + ar * sr - ai * si, xi + ar * si + ai * sr

    row8 = lax.broadcasted_iota(jnp.int32, (SUBLANES, LANES), 0)
    steps = _s5_hs_steps(bsz)
    cbase = 16 * len(steps)

    def last_step_tile(c):
        span = SUBLANES
        while span > bsz:
            half = span // 2
            c = jnp.where((row8 & (span - 1)) < half, pltpu.roll(c, half, axis=0), c)
            span = half
        return c

    def lanes_of(s):
        k, j = divmod(s, S5_STRIPS // 2)
        re0 = 2 * S5_HALF * k + j * LANES
        return slice(re0, re0 + LANES), slice(re0 + S5_HALF, re0 + S5_HALF + LANES), slice(s * LANES, (s + 1) * LANES)

    def scan(g):
        strips = range(g * S5_STRIPS_PER_LOOP, (g + 1) * S5_STRIPS_PER_LOOP)
        carry = []
        for s in strips:
            lr, li, _ = lanes_of(s)
            carry += [carry_ref[:, lr], carry_ref[:, li]]
        for r in range(rows // SUBLANES):
            r8 = slice(r * SUBLANES, (r + 1) * SUBLANES)
            for n, s in enumerate(strips):
                lr, li, lc = lanes_of(s)
                xr = x_ref[r8, lr]
                xi = x_ref[r8, li]
                for i, d in enumerate(steps):
                    xr, xi = cmul_add(xr, xi, consts_ref[16 * i:16 * i + 8, lc], consts_ref[16 * i + 8:16 * i + 16, lc],
                                      pltpu.roll(xr, d * bsz, axis=0), pltpu.roll(xi, d * bsz, axis=0))
                xr, xi = cmul_add(xr, xi, consts_ref[cbase:cbase + 8, lc], consts_ref[cbase + 8:cbase + 16, lc],
                                  carry[2 * n], carry[2 * n + 1])
                x_ref[r8, lr] = xr
                x_ref[r8, li] = xi
                carry[2 * n] = last_step_tile(xr)
                carry[2 * n + 1] = last_step_tile(xi)
        for n, s in enumerate(strips):
            lr, li, _ = lanes_of(s)
            carry_ref[:, lr] = carry[2 * n]
            carry_ref[:, li] = carry[2 * n + 1]

    expand(0)
    y_half = [None, None]
    for g in range(n_groups):
        if g + 1 < n_groups:
            expand(g + 1)
        scan(g)
        k = group_lanes(g)[0]
        y_half[k] = contract(g) if y_half[k] is None else y_half[k] + contract(g)
    y = jnp.concatenate(y_half, axis=1)
    y = jax.nn.gelu(y + d_ref[...] * u, approximate=True)
    z = jnp.dot(y.astype(BF16), gw_ref[...], preferred_element_type=F32) + gb_ref[...]
    out = z[:, :GROUP_W] * jax.nn.sigmoid(z[:, GROUP_W:])
    for s in range(n_us):
        oil_ref[s] = out[:, s * LANES:(s + 1) * LANES]
    for b in range(bsz):
        y_ref[b] = jnp.concatenate([oil_ref[s, pl.ds(b, tb, stride=bsz), :] for s in range(n_us)],
                                   axis=1).astype(BF16)


def _s5(proj3, bmat, cmat, dvec, consts, glu_w, glu_b):
    bsz, lp, _ = proj3.shape
    assert SUBLANES % bsz == 0
    tb = _pick(lp, 176, 16)
    rows = bsz * tb
    const = lambda shape: pl.BlockSpec(shape, lambda t: (0,) * len(shape))
    return pl.pallas_call(
        functools.partial(_s5_kernel, tb=tb, bsz=bsz),
        grid=(lp // tb,),
        in_specs=[
            pl.BlockSpec((bsz, tb, GROUP_W), lambda t: (0, t, COL_SU)),
            const((2, GROUP_W // 2, 2 * S5_HALF)),
            const((2, 2 * S5_HALF, GROUP_W // 2)),
            const((1, GROUP_W)),
            const((S5_CONST_ROWS, SSM_N)),
            const((GROUP_W, 2 * GROUP_W)),
            const((1, 2 * GROUP_W)),
        ],
        out_specs=pl.BlockSpec((bsz, tb, GROUP_W), lambda t: (0, t, 0)),
        out_shape=jax.ShapeDtypeStruct((bsz, lp, GROUP_W), BF16),
        scratch_shapes=[
            pltpu.VMEM((GROUP_W // LANES, rows, LANES), F32),
            pltpu.VMEM((GROUP_W // LANES, rows, LANES), F32),
            pltpu.VMEM((rows, 2 * SSM_N), F32),
            pltpu.VMEM((SUBLANES, 2 * SSM_N), F32),
        ],
        compiler_params=_params(("arbitrary",)),
        name="s5",
    )(proj3, bmat, cmat, dvec, consts, glu_w, glu_b)


def _mlstm_kernel(q_ref, k_ref, v_ref, o_ref, gcol_ref, grow_ref, cw_ref, ng_ref, y_ref,
                  buf_ref, cn_ref, m_ref):
    bsz = q_ref.shape[0]

    @pl.when(pl.program_id(0) == 0)
    def _():
        buf_ref[:, 0:SUBLANES, :] = jnp.zeros((bsz, SUBLANES, 2 * GROUP_W), F32)
        cn_ref[...] = jnp.zeros_like(cn_ref)
        m_ref[...] = jnp.zeros_like(m_ref)

    @pl.when(pl.program_id(0) > 0)
    def _():
        buf_ref[:, 0:SUBLANES, :] = buf_ref[:, BLOCK:BLOCK + SUBLANES, :]

    s_io = lax.broadcasted_iota(jnp.int32, (BLOCK, BLOCK), 0)
    t_io = lax.broadcasted_iota(jnp.int32, (BLOCK, BLOCK), 1)
    causal = s_io <= t_io
    ones_row = jnp.where(s_io == 0, 1.0, 0.0)
    nt_dims = (((1,), (1,)), ((), ()))

    for b in range(bsz):
        buf_ref[b, SUBLANES:SUBLANES + BLOCK, 0:GROUP_W] = q_ref[b].astype(F32)
        buf_ref[b, SUBLANES:SUBLANES + BLOCK, GROUP_W:] = k_ref[b].astype(F32)
        conv = cw_ref[ML_CONV - 1:ML_CONV, :] * buf_ref[b, SUBLANES:SUBLANES + BLOCK, :]
        for j in range(1, ML_CONV):
            conv = conv + cw_ref[ML_CONV - 1 - j:ML_CONV - j, :] * buf_ref[b, SUBLANES - j:SUBLANES - j + BLOCK, :]
        qk = conv * jax.nn.sigmoid(conv)
        gcol = gcol_ref[b]
        grow = grow_ref[b, 0]

        for h in range(HEADS):
            idx = b * HEADS + h
            cols = slice(h * DH, (h + 1) * DH)
            qh = qk[:, cols].astype(BF16)
            kh = (qk[:, GROUP_W + h * DH:GROUP_W + (h + 1) * DH] * (DH ** -0.5)).astype(BF16)
            v_t = jnp.concatenate([v_ref[b, :, cols].astype(F32).T, ones_row], axis=0)
            r_col = gcol[:, GC_MI + h:GC_MI + h + 1]
            r_row = grow[GC_MI + h:GC_MI + h + 1, :]
            b_row = grow[GC_MF + h:GC_MF + h + 1, :]
            rm_row = grow[GC_RM + h:GC_RM + h + 1, :]
            g = b_row[:, BLOCK - 1:BLOCK]
            r_max = rm_row[:, BLOCK - 1:BLOCK]
            cn_prev = cn_ref[idx]
            m_prev = m_ref[idx][:, 0:1]

            big_m = jnp.maximum(m_prev, rm_row)
            decay_t = jnp.where(causal, jnp.exp(r_col - big_m), 0.0)
            qk_t = lax.dot_general(kh, qh, nt_dims, preferred_element_type=F32)
            p_t = (qk_t * decay_t).astype(BF16)
            inter_w = jnp.exp(m_prev - big_m)
            tot = (jnp.dot(v_t.astype(BF16), p_t, preferred_element_type=F32)
                   + inter_w * lax.dot_general(cn_prev.astype(BF16), qh, nt_dims, preferred_element_type=F32))
            num = tot[0:DH, :]
            inv = 1.0 / jnp.maximum(jnp.abs(tot[DH:DH + 1, :]), jnp.exp(-(b_row + big_m)))
            scale = inv * lax.rsqrt(inv * inv * jnp.mean(num * num, axis=0, keepdims=True) + EPS)
            hn = (num * scale).T * ng_ref[:, cols]
            y_ref[b, :, cols] = (hn * jax.nn.sigmoid(o_ref[b, :, cols].astype(F32))).astype(BF16)

            vw_t = (v_t * jnp.exp(r_row - r_max)).astype(BF16)
            cn_loc = jnp.dot(vw_t, kh, preferred_element_type=F32)
            m_keep = jnp.maximum(m_prev, r_max)
            cn_ref[idx] = jnp.exp(m_prev - m_keep) * cn_prev + jnp.exp(r_max - m_keep) * cn_loc
            m_ref[idx] = jnp.broadcast_to(g + m_keep, (1, LANES))


def _mlstm(proj3, gcol, grow, conv_w, norm_g):
    bsz, lp, _ = proj3.shape
    blk = lambda col: pl.BlockSpec((bsz, BLOCK, GROUP_W), lambda c: (0, c, col))
    return pl.pallas_call(
        _mlstm_kernel,
        grid=(lp // BLOCK,),
        in_specs=[
            blk(COL_MQ), blk(COL_MK), blk(COL_MV), blk(COL_MO),
            pl.BlockSpec((bsz, BLOCK, LANES), lambda c: (0, c, 0)),
            pl.BlockSpec((bsz, 1, 2 * SUBLANES, BLOCK), lambda c: (0, c, 0, 0)),
            pl.BlockSpec((ML_CONV, 2 * GROUP_W), lambda c: (0, 0)),
            pl.BlockSpec((1, GROUP_W), lambda c: (0, 0)),
        ],
        out_specs=pl.BlockSpec((bsz, BLOCK, GROUP_W), lambda c: (0, c, 0)),
        out_shape=jax.ShapeDtypeStruct((bsz, lp, GROUP_W), BF16),
        scratch_shapes=[
            pltpu.VMEM((bsz, SUBLANES + BLOCK, 2 * GROUP_W), F32),
            pltpu.VMEM((bsz * HEADS, 2 * DH, DH), F32),
            pltpu.VMEM((bsz * HEADS, 1, LANES), F32),
        ],
        compiler_params=_params(("arbitrary",)),
        name="mlstm",
    )(proj3, proj3, proj3, proj3, gcol, grow, conv_w, norm_g)


FOX_ROWS = 32
FOX_BIG = 1e30


def _split3(c):
    hi = c.astype(BF16).astype(F32)
    r = c - hi
    mid = r.astype(BF16).astype(F32)
    return hi, mid, r - mid


def _fox_kernel(q_ref, k_ref, v_ref, gcol_ref, y_ref, kx_ref, vx_ref, qx_ref, s_ref, p_ref, acc_ref,
                m_ref, al_ref, *, tq, lp):
    qi = pl.program_id(1)
    nq = lp // tq
    lane = lax.broadcasted_iota(jnp.int32, (tq, LANES), 1)
    nt_dims = (((1,), (1,)), ((), ()))

    @pl.when(qi == 0)
    def _():
        def build(cb, carry):
            r0 = pl.multiple_of(cb * tq, tq)
            rows = pl.ds(r0, tq)
            is_token = (r0 + lax.broadcasted_iota(jnp.int32, (tq, 1), 0)) >= PAD
            g = gcol_ref[rows, :]
            ones_col = jnp.where(lane == 0, 1.0, 0.0).astype(BF16)
            for h in range(HEADS):
                cols = slice(h * DH, (h + 1) * DH)
                hi, mid, lo = _split3(jnp.where(is_token, g[:, GC_FF + h:GC_FF + h + 1], FOX_BIG))
                ext = jnp.where(lane < 3, 1.0, jnp.where(lane == 3, -hi, jnp.where(lane == 4, -mid,
                                jnp.where(lane == 5, -lo, 0.0))))
                kx_ref[h, rows, 0:DH] = k_ref[rows, cols]
                kx_ref[h, rows, DH:] = ext.astype(BF16)
                vx_ref[h, rows, 0:DH] = v_ref[rows, cols]
                vx_ref[h, rows, DH:] = ones_col
            return carry

        lax.fori_loop(0, nq, build, 0)

    gq = gcol_ref[pl.ds(pl.multiple_of(qi * tq, tq), tq), :]
    row_io = lax.broadcasted_iota(jnp.int32, (FOX_ROWS, tq), 0)
    col_io = lax.broadcasted_iota(jnp.int32, (FOX_ROWS, tq), 1)

    for h in range(HEADS):
        hi, mid, lo = _split3(gq[:, GC_FF + h:GC_FF + h + 1])
        ext = jnp.where(lane == 0, hi, jnp.where(lane == 1, mid, jnp.where(lane == 2, lo,
                        jnp.where(lane < 6, 1.0, 0.0))))
        qx_ref[h, :, 0:DH] = q_ref[:, h * DH:(h + 1) * DH]
        qx_ref[h, :, DH:] = ext.astype(BF16)
    m_ref[...] = jnp.full(m_ref.shape, NEG, F32)
    acc_ref[...] = jnp.zeros_like(acc_ref)

    def scores(kb, h):
        krows = pl.ds(pl.multiple_of(kb * tq, tq), tq)
        s_ref[h] = lax.dot_general(qx_ref[h], kx_ref[h, krows, :], nt_dims, preferred_element_type=F32)

    def step(kb, diagonal):
        krows = pl.ds(pl.multiple_of(kb * tq, tq), tq)
        for h in range(HEADS):
            for r0 in range(0, tq, FOX_ROWS):
                rr = slice(r0, r0 + FOX_ROWS)
                sc = s_ref[h, rr, :]
                if diagonal:
                    sc = jnp.where(col_io <= row_io + r0, sc, NEG)
                m_old = m_ref[h, rr, :]
                m_new = jnp.maximum(m_old, jnp.max(sc, axis=-1, keepdims=True))
                al_ref[h, rr, :] = jnp.exp(m_old - m_new)
                m_ref[h, rr, :] = m_new
                p_ref[h, rr, :] = jnp.exp(sc - m_new).astype(BF16)
            if not diagonal:
                scores(kb + 1, h)
            pv = jnp.dot(p_ref[h], vx_ref[h, krows, :], preferred_element_type=F32)
            acc_ref[h] = al_ref[h] * acc_ref[h] + pv

    def off_diagonal(kb, carry):
        step(kb, False)
        return carry

    for h in range(HEADS):
        scores(0, h)
    lax.fori_loop(0, qi, off_diagonal, 0)
    step(qi, True)
    for h in range(HEADS):
        acc = acc_ref[h]
        y_ref[:, h * DH:(h + 1) * DH] = (acc[:, 0:DH] / acc[:, DH:DH + 1]).astype(BF16)


def _fox(proj3, gcol):
    bsz, lp, _ = proj3.shape
    tq = _pick(lp, 384, BLOCK)
    nq = lp // tq
    return pl.pallas_call(
        functools.partial(_fox_kernel, tq=tq, lp=lp),
        grid=(bsz, nq),
        in_specs=[
            pl.BlockSpec((None, tq, GROUP_W), lambda b, i: (b, i, COL_FQ)),
            pl.BlockSpec((None, lp, GROUP_W), lambda b, i: (b, 0, COL_FK)),
            pl.BlockSpec((None, lp, GROUP_W), lambda b, i: (b, 0, COL_FV)),
            pl.BlockSpec((None, lp, LANES), lambda b, i: (b, 0, 0)),
        ],
        out_specs=pl.BlockSpec((tq, GROUP_W), lambda b, i: (b * nq + i, 0)),
        out_shape=jax.ShapeDtypeStruct((bsz * lp, GROUP_W), BF16),
        scratch_shapes=[
            pltpu.VMEM((HEADS, lp, 2 * DH), BF16),
            pltpu.VMEM((HEADS, lp, 2 * DH), BF16),
            pltpu.VMEM((HEADS, tq, 2 * DH), BF16),
            pltpu.VMEM((HEADS, tq, tq), F32),
            pltpu.VMEM((HEADS, tq, tq), BF16),
            pltpu.VMEM((HEADS, tq, 2 * DH), F32),
            pltpu.VMEM((HEADS, tq, 1), F32),
            pltpu.VMEM((HEADS, tq, 1), F32),
        ],
        compiler_params=_params(("arbitrary", "arbitrary")),
        name="fox",
    )(proj3, proj3, proj3, gcol)


def _pool_rows(u_ref, pw_ref, ps_ref, buf_ref, t, tm):
    @pl.when(t == 0)
    def _():
        buf_ref[0:POOL_MAXW, :] = jnp.zeros((POOL_MAXW, GROUP_W), F32)

    @pl.when(t > 0)
    def _():
        buf_ref[0:POOL_MAXW, :] = buf_ref[tm:tm + POOL_MAXW, :]

    buf_ref[POOL_MAXW:POOL_MAXW + tm, :] = u_ref[...].astype(F32)
    pos = t * tm + lax.broadcasted_iota(jnp.int32, (tm, 1), 0) - (PAD - 1)
    posf = jnp.maximum(pos, 1).astype(F32)
    gw = GROUP_W // len(POOL_WINDOWS)
    out = []
    for gi, w in enumerate(POOL_WINDOWS):
        cols = slice(gi * gw, (gi + 1) * gw)
        x = buf_ref[POOL_MAXW:POOL_MAXW + tm, cols]
        s = x
        for j in range(1, w):
            s = s + buf_ref[POOL_MAXW - j:POOL_MAXW - j + tm, cols]
        pooled = s / jnp.minimum(posf, float(w)) - x
        mixed = jnp.dot(pooled.astype(BF16), pw_ref[gi], preferred_element_type=F32)
        out.append((mixed * ps_ref[:, cols]).astype(BF16))
    return jnp.concatenate(out, axis=1)


def _outproj_kernel(ys_ref, ym_ref, yf_ref, pu_ref, pw_ref, ps_ref, w_ref, g_ref, h_ref, o_ref, buf_ref,
                    *, tm, blocks_per_seq):
    t = pl.program_id(0) % blocks_per_seq
    y_pool = _pool_rows(pu_ref, pw_ref, ps_ref, buf_ref, t, tm)
    mix = jnp.dot(ys_ref[...], w_ref[0:GROUP_W, :], preferred_element_type=F32)
    for n, y_ref in enumerate((ym_ref, yf_ref), start=1):
        mix = mix + jnp.dot(y_ref[...], w_ref[n * GROUP_W:(n + 1) * GROUP_W, :], preferred_element_type=F32)
    mix = mix + jnp.dot(y_pool, w_ref[3 * GROUP_W:4 * GROUP_W, :], preferred_element_type=F32)
    valid = (t * tm + lax.broadcasted_iota(jnp.int32, (tm, 1), 0)) >= PAD
    o_ref[...] = jnp.where(valid, h_ref[...] + _rms(mix, g_ref[...]), 0.0)


def _outproj(ys, ym, yf, proj, pool_w, pool_scale, w_out_all, g, h2, lp, l):
    m = h2.shape[0]
    tm = _pick(lp, 528, 16)
    gw = GROUP_W // len(POOL_WINDOWS)
    yspec = pl.BlockSpec((tm, GROUP_W), lambda i: (i, 0))
    return pl.pallas_call(
        functools.partial(_outproj_kernel, tm=tm, blocks_per_seq=lp // tm),
        grid=(m // tm,),
        in_specs=[
            yspec, yspec, yspec,
            pl.BlockSpec((tm, GROUP_W), lambda i: (i, COL_PU)),
            pl.BlockSpec((len(POOL_WINDOWS), gw, gw), lambda i: (0, 0, 0)),
            pl.BlockSpec((1, GROUP_W), lambda i: (0, 0)),
            pl.BlockSpec((None, D_MODEL, D_MODEL), lambda i: (l, 0, 0)),
            pl.BlockSpec((1, D_MODEL), lambda i: (0, 0)),
            pl.BlockSpec((tm, D_MODEL), lambda i: (i, 0)),
        ],
        out_specs=pl.BlockSpec((tm, D_MODEL), lambda i: (i, 0)),
        out_shape=jax.ShapeDtypeStruct((m, D_MODEL), F32),
        scratch_shapes=[pltpu.VMEM((POOL_MAXW + tm, GROUP_W), F32)],
        input_output_aliases={8: 0},
        compiler_params=_params(("arbitrary",)),
        name="outproj",
    )(ys, ym, yf, proj, pool_w, pool_scale, w_out_all, g, h2)


def _ffn_kernel(h_ref, g1_ref, w1_ref, w2_ref, g2_ref, o_ref, hn_ref, acc_ref):
    f = pl.program_id(1)
    last = pl.num_programs(1) - 1
    tm = h_ref.shape[0]
    halves = (slice(0, tm // 2), slice(tm // 2, tm))

    def update(rows):
        a = jnp.maximum(jnp.dot(hn_ref[rows, :], w1_ref[...], preferred_element_type=F32), 0.0)
        return jnp.dot((a * a).astype(BF16), w2_ref[...], preferred_element_type=F32)

    @pl.when(f == 0)
    def _():
        for rows in halves:
            hn_ref[rows, :] = _rms(h_ref[rows, :], g1_ref[...]).astype(BF16)
            acc_ref[rows, :] = update(rows)

    @pl.when((f > 0) & (f < last))
    def _():
        acc_ref[...] += update(slice(0, tm))

    @pl.when(f == last)
    def _():
        for rows in halves:
            o_ref[rows, :] = h_ref[rows, :] + _rms(acc_ref[rows, :] + update(rows), g2_ref[...])


FFN_TF = 1024


def _ffn_specs(l, tm, h_spec):
    return [
        h_spec,
        pl.BlockSpec((1, D_MODEL), lambda i, f: (0, 0)),
        pl.BlockSpec((None, D_MODEL, FFN_TF), lambda i, f: (l, 0, f)),
        pl.BlockSpec((None, FFN_TF, D_MODEL), lambda i, f: (l, f, 0)),
        pl.BlockSpec((1, D_MODEL), lambda i, f: (0, 0)),
    ], [pltpu.VMEM((tm, D_MODEL), BF16), pltpu.VMEM((tm, D_MODEL), F32)]


def _ffn(h2, g1, w1_all, w2_all, g2, l):
    m = h2.shape[0]
    tm = _pick(m, 768, 16)
    in_specs, scratch = _ffn_specs(l, tm, pl.BlockSpec((tm, D_MODEL), lambda i, f: (i, 0)))
    return pl.pallas_call(
        _ffn_kernel,
        grid=(m // tm, D_FF // FFN_TF),
        in_specs=in_specs,
        out_specs=pl.BlockSpec((tm, D_MODEL), lambda i, f: (i, 0)),
        out_shape=jax.ShapeDtypeStruct((m, D_MODEL), F32),
        scratch_shapes=scratch,
        input_output_aliases={0: 0},
        compiler_params=_params(("arbitrary", "arbitrary")),
        name="ffn",
    )(h2, g1, w1_all, w2_all, g2)


def _ffn_final(h2, g1, w1_all, w2_all, g2, l, bsz, lp):
    seq = lp - PAD - N_META
    tm = _pick(seq, 512, 16)
    per_seq = seq // tm
    h_spec = pl.BlockSpec((pl.Element(tm), pl.Element(D_MODEL)),
                          lambda i, f: (pl.multiple_of((i // per_seq) * lp + PAD + N_META + (i % per_seq) * tm,
                                                       SUBLANES), 0))
    in_specs, scratch = _ffn_specs(l, tm, h_spec)
    return pl.pallas_call(
        _ffn_kernel,
        grid=(bsz * per_seq, D_FF // FFN_TF),
        in_specs=in_specs,
        out_specs=pl.BlockSpec((tm, D_MODEL), lambda i, f: (i, 0)),
        out_shape=jax.ShapeDtypeStruct((bsz * seq, D_MODEL), F32),
        scratch_shapes=scratch,
        compiler_params=_params(("arbitrary", "arbitrary")),
        name="ffn_final",
    )(h2, g1, w1_all, w2_all, g2)


IN_COLS = N_MAIN + 3 * HEADS
IN_COLS_PAD = -(-IN_COLS // LANES) * LANES
O_MI = 5 * GROUP_W
O_FQ = O_MI + 2 * HEADS
O_FF = O_FQ + 3 * GROUP_W
O_PU = O_FF + HEADS


SPLIT_BLOCKS = N_MAIN // LANES
assert O_MI % LANES == 0 and (O_FQ - O_MI) + 3 * GROUP_W == (O_FF - O_MI) and O_FF % LANES == GC_FF


def _split_w_in_kernel(a_ref, b_ref, main_ref, gate_ref):
    d = pl.program_id(0)
    depth = a_ref.shape[1]
    col = lax.broadcasted_iota(jnp.int32, (LANES, D_MODEL), 0)

    def emit(shift):
        for l in range(depth):
            a = a_ref[:, l, :]
            if shift:
                a = pltpu.roll(jnp.where(col >= shift, a, b_ref[:, l, :]), LANES - shift, axis=0)
            main_ref[l] = a.T.astype(BF16)

    wide_before_ml_gates = O_MI // LANES
    wide_before_fx_gate = (O_FF - (O_FQ - O_MI)) // LANES

    @pl.when(d < wide_before_ml_gates)
    def _():
        emit(0)

    @pl.when((d >= wide_before_ml_gates) & (d < wide_before_fx_gate))
    def _():
        emit(O_FQ - O_MI)

    @pl.when(d >= wide_before_fx_gate)
    def _():
        emit(O_PU - O_MI - 3 * GROUP_W)

    @pl.when(d == 0)
    def _():
        gate_ref[...] = jnp.zeros_like(gate_ref)

    @pl.when(d == wide_before_ml_gates)
    def _():
        for l in range(depth):
            gate_ref[l] += jnp.where(col < GC_FF, a_ref[:, l, :], 0.0).T

    @pl.when(d == O_FF // LANES)
    def _():
        for l in range(depth):
            g_fx = jnp.where(col >= GC_FF, jnp.where(col < GC_FF + HEADS, a_ref[:, l, :], 0.0), 0.0)
            gate_ref[l] += g_fx.T


def _split_w_in(w_all):
    depth = w_all.shape[0]
    w_t = jnp.transpose(w_all, (2, 0, 1))
    blk = lambda off: pl.BlockSpec((LANES, depth, D_MODEL), lambda d: (d + off, 0, 0))
    return pl.pallas_call(
        _split_w_in_kernel,
        grid=(SPLIT_BLOCKS,),
        in_specs=[blk(0), blk(1)],
        out_specs=[pl.BlockSpec((depth, D_MODEL, LANES), lambda d: (0, 0, d)),
                   pl.BlockSpec((depth, D_MODEL, LANES), lambda d: (0, 0, 0))],
        out_shape=[jax.ShapeDtypeStruct((depth, D_MODEL, N_MAIN), BF16),
                   jax.ShapeDtypeStruct((depth, D_MODEL, LANES), F32)],
        compiler_params=_params(("arbitrary",)),
        name="split_w_in",
    )(w_t, w_t)


def _s5_matrices(bbre, bbim, c_re, c_im):
    gh = SSM_G // 2
    mask = (jnp.arange(GROUP_W // 2)[:, None] // SSM_H == jnp.arange(S5_HALF)[None, :] // SSM_P).astype(F32)
    c_t = lambda c: c.transpose(0, 2, 1).reshape(SSM_N, SSM_H)
    bmats, cmats = [], []
    for k in range(2):
        st = slice(S5_HALF * k, S5_HALF * (k + 1))
        expand = lambda bb: jnp.tile(bb[:, st], (gh, 1)) * mask
        cont = lambda c: jnp.tile(c_t(c)[st, :], (1, gh)) * mask.T
        bmats.append(jnp.concatenate([expand(bbre), expand(bbim)], axis=1))
        cmats.append(jnp.concatenate([cont(c_re), -cont(c_im)], axis=0))
    return jnp.stack(bmats).astype(BF16), jnp.stack(cmats).astype(BF16)


def kernel(x, meta_tokens, g_pre_mix, g_post_mix, g_pre_ffn, g_post_ffn, w_in, ml_gate_bias, fx_gate_bias, ssm_lam_re, ssm_lam_im, ssm_log_dt, ssm_b_re, ssm_b_im, ssm_c_re, ssm_c_im, ssm_d, ssm_glu_w, ssm_glu_b, ml_conv_w, ml_norm_g, pool_w, pool_scale, w_out, mlp_w1, mlp_w2):
    bsz, seq, d = x.shape
    depth = w_in.shape[0]
    lp = PAD + N_META + seq
    m = bsz * lp
    assert d == D_MODEL and lp % BLOCK == 0
    row = lambda a: a.reshape(1, -1).astype(F32)
    colscale = jnp.ones((1, N_MAIN), F32).at[:, COL_FQ * GROUP_W:(COL_FQ + 1) * GROUP_W].set(DH ** -0.5)
    w_out16, w1_16, w2_16 = w_out.astype(BF16), mlp_w1.astype(BF16), mlp_w2.astype(BF16)

    w_main, w_gate = _split_w_in(w_in)
    w_gate = w_gate.astype(BF16)

    for l in range(depth):
        if l == 0:
            proj, graw, h = _inproj_first(x.reshape(bsz * seq, d).astype(F32), meta_tokens.astype(F32),
                                          row(g_pre_mix[l]), w_main, w_gate, colscale, bsz, lp)
        else:
            proj, graw = _inproj(h, row(g_pre_mix[l]), w_main, w_gate, colscale, l)
        proj3 = proj.reshape(bsz, lp, N_MAIN)

        gbias = jnp.pad(jnp.concatenate([ml_gate_bias[l], fx_gate_bias[l]]).astype(F32), (0, LANES - 3 * HEADS))
        gcol, grow = _gates(graw, gbias.reshape(1, LANES), bsz, lp)

        consts, bbre, bbim = _s5_prep(ssm_lam_re[l], ssm_lam_im[l], ssm_log_dt[l], ssm_b_re[l], ssm_b_im[l], bsz)
        bmat, cmat = _s5_matrices(bbre, bbim, ssm_c_re[l], ssm_c_im[l])
        y_ssm = _s5(proj3, bmat, cmat, row(ssm_d[l]), consts, ssm_glu_w[l].astype(BF16), row(ssm_glu_b[l]))
        y_ml = _mlstm(proj3, gcol, grow, ml_conv_w[l].astype(F32), row(ml_norm_g[l]))
        y_fx = _fox(proj3, gcol)

        h = _outproj(y_ssm.reshape(m, GROUP_W), y_ml.reshape(m, GROUP_W), y_fx,
                     proj, pool_w[l].astype(BF16), row(pool_scale[l]),
                     w_out16, row(g_post_mix[l]), h, lp, l)
        if l + 1 < depth:
            h = _ffn(h, row(g_pre_ffn[l]), w1_16, w2_16, row(g_post_ffn[l]), l)
        else:
            h = _ffn_final(h, row(g_pre_ffn[l]), w1_16, w2_16, row(g_post_ffn[l]), l, bsz, lp)

    return h.reshape(bsz, seq, d)
```

```python
import functools
import math

import jax
import jax.numpy as jnp
from jax import lax
from jax.experimental import pallas as pl
from jax.experimental.pallas import tpu as pltpu

F32 = jnp.float32
BF16 = jnp.bfloat16

D_MODEL = 2048
N_META = 16
BLOCK = 128
PAD = BLOCK - N_META
GROUP_W = 512
SSM_H = 16
SSM_G = GROUP_W // SSM_H
SSM_P = 64
SSM_N = SSM_G * SSM_P
HEADS = 4
DH = GROUP_W // HEADS
ML_CONV = 4
POOL_WINDOWS = (2, 4, 8, 16)
POOL_MAXW = 16
D_FF = 4 * D_MODEL
EPS = 1e-6
NEG = -1e30
N_MAIN = 9 * GROUP_W
LANES = 128
SUBLANES = 8
VMEM_LIMIT = 60 * 1024 * 1024

COL_SU, COL_MQ, COL_MK, COL_MV, COL_MO, COL_FQ, COL_FK, COL_FV, COL_PU = range(9)
GC_MI, GC_MF, GC_FF, GC_RM = 0, HEADS, 2 * HEADS, 3 * HEADS


def _pick(n, target, mult):
    best = None
    for d in range(mult, min(n, target) + 1, mult):
        if n % d == 0:
            best = d
    assert best is not None, (n, target, mult)
    return best


def _params(sem):
    return pltpu.CompilerParams(dimension_semantics=sem, vmem_limit_bytes=VMEM_LIMIT)


def _rms(x, g):
    return x * lax.rsqrt(jnp.mean(x * x, axis=-1, keepdims=True) + EPS) * g


def _inproj_steps(src_ref, g_ref, w_ref, wg_ref, cs_ref, o_ref, og_ref, xn_ref):
    tm = src_ref.shape[0]

    def project(rows):
        acc = jnp.dot(xn_ref[rows, :], w_ref[...], preferred_element_type=F32)
        o_ref[rows, :] = (acc * cs_ref[...]).astype(BF16)

    @pl.when(pl.program_id(1) == 0)
    def _():
        for rows in (slice(0, tm // 2), slice(tm // 2, tm)):
            xn = _rms(src_ref[rows, :], g_ref[...]).astype(BF16)
            xn_ref[rows, :] = xn
            og_ref[rows, :] = jnp.dot(xn, wg_ref[...], preferred_element_type=F32)
            project(rows)

    @pl.when(pl.program_id(1) > 0)
    def _():
        project(slice(0, tm))


def _inproj_kernel(x_ref, g_ref, w_ref, wg_ref, cs_ref, o_ref, og_ref, xn_ref):
    _inproj_steps(x_ref, g_ref, w_ref, wg_ref, cs_ref, o_ref, og_ref, xn_ref)


def _inproj(h2, g, w_main, w_gate, colscale, l):
    m = h2.shape[0]
    tm = _pick(m, 1056, 16)
    tn = 1536
    return pl.pallas_call(
        _inproj_kernel,
        grid=(m // tm, N_MAIN // tn),
        in_specs=[
            pl.BlockSpec((tm, D_MODEL), lambda i, j: (i, 0)),
            pl.BlockSpec((1, D_MODEL), lambda i, j: (0, 0)),
            pl.BlockSpec((None, D_MODEL, tn), lambda i, j: (l, 0, j)),
            pl.BlockSpec((None, D_MODEL, LANES), lambda i, j: (l, 0, 0)),
            pl.BlockSpec((1, tn), lambda i, j: (0, j)),
        ],
        out_specs=[
            pl.BlockSpec((tm, tn), lambda i, j: (i, j)),
            pl.BlockSpec((tm, LANES), lambda i, j: (i, 0)),
        ],
        out_shape=[jax.ShapeDtypeStruct((m, N_MAIN), BF16), jax.ShapeDtypeStruct((m, LANES), F32)],
        scratch_shapes=[pltpu.VMEM((tm, D_MODEL), BF16)],
        compiler_params=_params(("arbitrary", "arbitrary")),
        name="inproj",
    )(h2, g, w_main, w_gate, colscale)


def _inproj_first_kernel(x_ref, meta_ref, g_ref, w_ref, wg_ref, cs_ref, o_ref, og_ref, h_ref, xn_ref,
                         *, tm, tiles_per_seq):
    head = PAD + N_META
    first_n = pl.program_id(1) == 0
    seq_start = pl.program_id(0) % tiles_per_seq == 0

    @pl.when(first_n & seq_start)
    def _():
        h_ref[0:PAD, :] = jnp.zeros((PAD, D_MODEL), F32)
        h_ref[PAD:head, :] = meta_ref[...]
        h_ref[head:tm, :] = x_ref[0:tm - head, :]

    @pl.when(first_n & jnp.logical_not(seq_start))
    def _():
        h_ref[...] = x_ref[...]

    _inproj_steps(h_ref, g_ref, w_ref, wg_ref, cs_ref, o_ref, og_ref, xn_ref)


def _inproj_first(x2, meta, g, w_main, w_gate, colscale, bsz, lp, l=0):
    seq = lp - PAD - N_META
    tm = _pick(lp, min(704, lp // 2), 16)
    assert tm > PAD + N_META
    per_seq = lp // tm
    tn = 1536
    m = bsz * lp

    def x_index(i, j):
        b, t = i // per_seq, i % per_seq
        return pl.multiple_of(b * seq + jnp.maximum(t * tm - (PAD + N_META), 0), SUBLANES), 0

    return pl.pallas_call(
        functools.partial(_inproj_first_kernel, tm=tm, tiles_per_seq=per_seq),
        grid=(m // tm, N_MAIN // tn),
        in_specs=[
            pl.BlockSpec((pl.Element(tm), pl.Element(D_MODEL)), x_index),
            pl.BlockSpec((N_META, D_MODEL), lambda i, j: (0, 0)),
            pl.BlockSpec((1, D_MODEL), lambda i, j: (0, 0)),
            pl.BlockSpec((None, D_MODEL, tn), lambda i, j: (l, 0, j)),
            pl.BlockSpec((None, D_MODEL, LANES), lambda i, j: (l, 0, 0)),
            pl.BlockSpec((1, tn), lambda i, j: (0, j)),
        ],
        out_specs=[
            pl.BlockSpec((tm, tn), lambda i, j: (i, j)),
            pl.BlockSpec((tm, LANES), lambda i, j: (i, 0)),
            pl.BlockSpec((tm, D_MODEL), lambda i, j: (i, 0)),
        ],
        out_shape=[jax.ShapeDtypeStruct((m, N_MAIN), BF16), jax.ShapeDtypeStruct((m, LANES), F32),
                   jax.ShapeDtypeStruct((m, D_MODEL), F32)],
        scratch_shapes=[pltpu.VMEM((tm, D_MODEL), BF16)],
        compiler_params=_params(("arbitrary", "arbitrary")),
        name="inproj_first",
    )(x2, meta, g, w_main, w_gate, colscale)


def _gates_kernel(raw_ref, bias_ref, col_ref, row_ref, carry_ref):
    c = pl.program_id(0)

    @pl.when(c == 0)
    def _():
        carry_ref[...] = jnp.zeros_like(carry_ref)

    r_io = lax.broadcasted_iota(jnp.int32, (BLOCK, LANES), 0)
    c_io = lax.broadcasted_iota(jnp.int32, (BLOCK, LANES), 1)
    valid = (r_io + c * BLOCK) >= PAD
    tri = (r_io >= c_io).astype(F32)
    for b in range(raw_ref.shape[0]):
        g = raw_ref[b] + bias_ref[...]
        logsig = jnp.minimum(g, 0.0) - jnp.log(1.0 + jnp.exp(-jnp.abs(g)))
        x = jnp.where(valid, logsig, 0.0)
        x = jnp.where(c_io >= GC_MF, jnp.where(c_io < GC_FF + HEADS, x, 0.0), 0.0)
        cum = jnp.dot(tri, x, precision=lax.Precision.HIGHEST, preferred_element_type=F32)
        cum = cum + jnp.where(c_io >= GC_FF, carry_ref[b], 0.0)
        carry_ref[b] = cum[BLOCK - 1:BLOCK, :]
        log_i = jnp.where(valid, g, NEG)
        r = log_i - pltpu.roll(cum, LANES - GC_MF, axis=1)
        rmax = r
        d = 1
        while d < BLOCK:
            rmax = jnp.maximum(rmax, jnp.where(r_io >= d, pltpu.roll(rmax, d, axis=0), NEG))
            d *= 2
        out = jnp.where(c_io < GC_MF, r, jnp.where(c_io < GC_RM, cum, jnp.where(
            c_io < GC_RM + HEADS, pltpu.roll(rmax, GC_RM, axis=1), 0.0)))
        col_ref[b] = out
        row_ref[b, 0] = out.T[0:2 * SUBLANES, :]


def _gates(raw, bias, bsz, lp):
    nc = lp // BLOCK
    return pl.pallas_call(
        _gates_kernel,
        grid=(nc,),
        in_specs=[
            pl.BlockSpec((bsz, BLOCK, LANES), lambda c: (0, c, 0)),
            pl.BlockSpec((1, LANES), lambda c: (0, 0)),
        ],
        out_specs=[
            pl.BlockSpec((bsz, BLOCK, LANES), lambda c: (0, c, 0)),
            pl.BlockSpec((bsz, 1, 2 * SUBLANES, BLOCK), lambda c: (0, c, 0, 0)),
        ],
        out_shape=[
            jax.ShapeDtypeStruct((bsz, lp, LANES), F32),
            jax.ShapeDtypeStruct((bsz, nc, 2 * SUBLANES, BLOCK), F32),
        ],
        scratch_shapes=[pltpu.VMEM((bsz, 1, LANES), F32)],
        compiler_params=_params(("arbitrary",)),
        name="gates",
    )(raw.reshape(bsz, lp, LANES), bias)


S5_CONST_ROWS = 64
S5_STRIPS = SSM_N // LANES
S5_STRIPS_PER_LOOP = 2
S5_HALF = SSM_N // 2


def _s5_hs_steps(bsz):
    return [d for d in (1, 2, 4) if d < SUBLANES // bsz]


def _s5_prep_kernel(lre_ref, lim_ref, ldt_ref, bre_ref, bim_ref, consts_ref, bbre_ref, bbim_ref, *, bsz):
    lre = lre_ref[...]
    lim = lim_ref[...]
    dt = jnp.exp(ldt_ref[...])
    r_io = lax.broadcasted_iota(jnp.int32, (SUBLANES, SSM_N), 0)
    kk = (r_io + 1).astype(F32)
    mag = jnp.exp(kk * (lre * dt))
    ang = kk * (lim * dt)
    pw_re = mag * jnp.cos(ang)
    pw_im = mag * jnp.sin(ang)
    consts_ref[...] = jnp.zeros_like(consts_ref)
    steps = _s5_hs_steps(bsz)
    for i, d in enumerate(steps):
        keep = r_io >= d * bsz
        consts_ref[16 * i:16 * i + 8, :] = jnp.where(keep, pw_re[d - 1:d, :], 0.0)
        consts_ref[16 * i + 8:16 * i + 16, :] = jnp.where(keep, pw_im[d - 1:d, :], 0.0)
    base = 16 * len(steps)
    consts_ref[base:base + 8, :] = jnp.where(r_io < bsz, pw_re[0:1, :], 0.0)
    consts_ref[base + 8:base + 16, :] = jnp.where(r_io < bsz, pw_im[0:1, :], 0.0)
    nr = pw_re[0:1, :] - 1.0
    ni = pw_im[0:1, :]
    den = lre * lre + lim * lim
    cr = (nr * lre + ni * lim) / den
    ci = (ni * lre - nr * lim) / den
    bre = bre_ref[...]
    bim = bim_ref[...]
    bbre_ref[...] = cr * bre - ci * bim
    bbim_ref[...] = cr * bim + ci * bre


def _s5_prep(lam_re, lam_im, log_dt, b_re, b_im, bsz):
    flat = lambda a: a.reshape(1, SSM_N)
    ldt = jnp.repeat(log_dt, SSM_P).reshape(1, SSM_N)
    bt = lambda a: a.transpose(2, 0, 1).reshape(SSM_H, SSM_N)
    return pl.pallas_call(
        functools.partial(_s5_prep_kernel, bsz=bsz),
        out_shape=[
            jax.ShapeDtypeStruct((S5_CONST_ROWS, SSM_N), F32),
            jax.ShapeDtypeStruct((SSM_H, SSM_N), F32),
            jax.ShapeDtypeStruct((SSM_H, SSM_N), F32),
        ],
        name="s5_prep",
    )(flat(lam_re), flat(lam_im), ldt, bt(b_re), bt(b_im))


def _s5_kernel(u_ref, bmat_ref, cmat_ref, d_ref, consts_ref, gw_ref, gb_ref, y_ref,
               uil_ref, oil_ref, x_ref, carry_ref, *, tb, bsz):
    rows = bsz * tb
    n_us = GROUP_W // LANES

    @pl.when(pl.program_id(0) == 0)
    def _():
        carry_ref[...] = jnp.zeros_like(carry_ref)

    for b in range(bsz):
        ub = u_ref[b].astype(F32)
        for s in range(n_us):
            uil_ref[s, pl.ds(b, tb, stride=bsz), :] = ub[:, s * LANES:(s + 1) * LANES]
    u = jnp.concatenate([uil_ref[s] for s in range(n_us)], axis=1)
    u16 = u.astype(BF16)
    half_w = GROUP_W // 2
    n_groups = S5_STRIPS // S5_STRIPS_PER_LOOP
    gw_lanes = S5_STRIPS_PER_LOOP * LANES

    def group_lanes(g):
        k, j = divmod(g, n_groups // 2)
        re0 = j * gw_lanes
        return k, slice(re0, re0 + gw_lanes), slice(S5_HALF + re0, S5_HALF + re0 + gw_lanes)

    def expand(g):
        k, re, im = group_lanes(g)
        uk = u16[:, half_w * k:half_w * (k + 1)]
        base = 2 * S5_HALF * k
        for part in (re, im):
            x_ref[:, base + part.start:base + part.stop] = jnp.dot(
                uk, bmat_ref[k, :, part], preferred_element_type=F32)

    def contract(g):
        k, re, im = group_lanes(g)
        base = 2 * S5_HALF * k
        return sum(jnp.dot(x_ref[:, base + part.start:base + part.stop].astype(BF16), cmat_ref[k, part, :],
                           preferred_element_type=F32) for part in (re, im))

    def cmul_add(xr, xi, ar, ai, sr, si):
        return xr + ar * sr - ai * si, xi + ar * si + ai * sr

    steps = _s5_hs_steps(bsz)
    cbase = 16 * len(steps)

    def to_first_step(c):
        return pltpu.roll(c, bsz, axis=0) if bsz < SUBLANES else c

    def lanes_of(s):
        k, j = divmod(s, S5_STRIPS // 2)
        re0 = 2 * S5_HALF * k + j * LANES
        return slice(re0, re0 + LANES), slice(re0 + S5_HALF, re0 + S5_HALF + LANES), slice(s * LANES, (s + 1) * LANES)

    def scan(g):
        strips = range(g * S5_STRIPS_PER_LOOP, (g + 1) * S5_STRIPS_PER_LOOP)
        carry = []
        for s in strips:
            lr, li, _ = lanes_of(s)
            carry += [carry_ref[:, lr], carry_ref[:, li]]
        for r in range(rows // SUBLANES):
            r8 = slice(r * SUBLANES, (r + 1) * SUBLANES)
            for n, s in enumerate(strips):
                lr, li, lc = lanes_of(s)
                xr = x_ref[r8, lr]
                xi = x_ref[r8, li]
                xr, xi = cmul_add(xr, xi, consts_ref[cbase:cbase + 8, lc], consts_ref[cbase + 8:cbase + 16, lc],
                                  carry[2 * n], carry[2 * n + 1])
                for i, d in enumerate(steps):
                    xr, xi = cmul_add(xr, xi, consts_ref[16 * i:16 * i + 8, lc], consts_ref[16 * i + 8:16 * i + 16, lc],
                                      pltpu.roll(xr, d * bsz, axis=0), pltpu.roll(xi, d * bsz, axis=0))
                x_ref[r8, lr] = xr
                x_ref[r8, li] = xi
                carry[2 * n] = to_first_step(xr)
                carry[2 * n + 1] = to_first_step(xi)
        for n, s in enumerate(strips):
            lr, li, _ = lanes_of(s)
            carry_ref[:, lr] = carry[2 * n]
            carry_ref[:, li] = carry[2 * n + 1]

    expand(0)
    y_half = [None, None]
    for g in range(n_groups):
        if g + 1 < n_groups:
            expand(g + 1)
        scan(g)
        k = group_lanes(g)[0]
        y_half[k] = contract(g) if y_half[k] is None else y_half[k] + contract(g)
    y = jnp.concatenate(y_half, axis=1)
    y = jax.nn.gelu(y + d_ref[...] * u, approximate=True)
    z = jnp.dot(y.astype(BF16), gw_ref[...], preferred_element_type=F32) + gb_ref[...]
    out = z[:, :GROUP_W] * jax.nn.sigmoid(z[:, GROUP_W:])
    for s in range(n_us):
        oil_ref[s] = out[:, s * LANES:(s + 1) * LANES]
    for b in range(bsz):
        y_ref[b] = jnp.concatenate([oil_ref[s, pl.ds(b, tb, stride=bsz), :] for s in range(n_us)],
                                   axis=1).astype(BF16)


def _s5(proj3, bmat, cmat, dvec, consts, glu_w, glu_b):
    bsz, lp, _ = proj3.shape
    assert SUBLANES % bsz == 0
    tb = _pick(lp, 176, 16)
    rows = bsz * tb
    const = lambda shape: pl.BlockSpec(shape, lambda t: (0,) * len(shape))
    return pl.pallas_call(
        functools.partial(_s5_kernel, tb=tb, bsz=bsz),
        grid=(lp // tb,),
        in_specs=[
            pl.BlockSpec((bsz, tb, GROUP_W), lambda t: (0, t, COL_SU)),
            const((2, GROUP_W // 2, 2 * S5_HALF)),
            const((2, 2 * S5_HALF, GROUP_W // 2)),
            const((1, GROUP_W)),
            const((S5_CONST_ROWS, SSM_N)),
            const((GROUP_W, 2 * GROUP_W)),
            const((1, 2 * GROUP_W)),
        ],
        out_specs=pl.BlockSpec((bsz, tb, GROUP_W), lambda t: (0, t, 0)),
        out_shape=jax.ShapeDtypeStruct((bsz, lp, GROUP_W), BF16),
        scratch_shapes=[
            pltpu.VMEM((GROUP_W // LANES, rows, LANES), F32),
            pltpu.VMEM((GROUP_W // LANES, rows, LANES), F32),
            pltpu.VMEM((rows, 2 * SSM_N), F32),
            pltpu.VMEM((SUBLANES, 2 * SSM_N), F32),
        ],
        compiler_params=_params(("arbitrary",)),
        name="s5",
    )(proj3, bmat, cmat, dvec, consts, glu_w, glu_b)


ML_SUM_ROWS = 16


def _mlstm_kernel(q_ref, k_ref, v_ref, o_ref, gcol_ref, grow_ref, cw_ref, ng_ref, y_ref,
                  buf_ref, cn_ref, m_ref):
    bsz = q_ref.shape[0]

    @pl.when(pl.program_id(0) == 0)
    def _():
        buf_ref[:, 0:SUBLANES, :] = jnp.zeros((bsz, SUBLANES, 2 * GROUP_W), F32)
        cn_ref[...] = jnp.zeros_like(cn_ref)
        m_ref[...] = jnp.zeros_like(m_ref)

    @pl.when(pl.program_id(0) > 0)
    def _():
        buf_ref[:, 0:SUBLANES, :] = buf_ref[:, BLOCK:BLOCK + SUBLANES, :]

    s_io = lax.broadcasted_iota(jnp.int32, (BLOCK, BLOCK), 0)
    t_io = lax.broadcasted_iota(jnp.int32, (BLOCK, BLOCK), 1)
    causal = s_io <= t_io
    ones_row = jnp.where(s_io[0:ML_SUM_ROWS] == 0, 1.0, 0.0)
    nt_dims = (((1,), (1,)), ((), ()))

    for b in range(bsz):
        buf_ref[b, SUBLANES:SUBLANES + BLOCK, 0:GROUP_W] = q_ref[b].astype(F32)
        buf_ref[b, SUBLANES:SUBLANES + BLOCK, GROUP_W:] = k_ref[b].astype(F32)
        conv = cw_ref[ML_CONV - 1:ML_CONV, :] * buf_ref[b, SUBLANES:SUBLANES + BLOCK, :]
        for j in range(1, ML_CONV):
            conv = conv + cw_ref[ML_CONV - 1 - j:ML_CONV - j, :] * buf_ref[b, SUBLANES - j:SUBLANES - j + BLOCK, :]
        qk = conv * jax.nn.sigmoid(conv)
        gcol = gcol_ref[b]
        grow = grow_ref[b, 0]

        for h in range(HEADS):
            idx = b * HEADS + h
            cols = slice(h * DH, (h + 1) * DH)
            qh = qk[:, cols].astype(BF16)
            kh = (qk[:, GROUP_W + h * DH:GROUP_W + (h + 1) * DH] * (DH ** -0.5)).astype(BF16)
            v_t = jnp.concatenate([v_ref[b, :, cols].astype(F32).T, ones_row], axis=0)
            r_col = gcol[:, GC_MI + h:GC_MI + h + 1]
            r_row = grow[GC_MI + h:GC_MI + h + 1, :]
            b_row = grow[GC_MF + h:GC_MF + h + 1, :]
            rm_row = grow[GC_RM + h:GC_RM + h + 1, :]
            g = b_row[:, BLOCK - 1:BLOCK]
            r_max = rm_row[:, BLOCK - 1:BLOCK]
            cn_prev = cn_ref[idx]
            m_prev = m_ref[idx][:, 0:1]

            big_m = jnp.maximum(m_prev, rm_row)
            decay_t = jnp.where(causal, jnp.exp(r_col - big_m), 0.0)
            qk_t = lax.dot_general(kh, qh, nt_dims, preferred_element_type=F32)
            p_t = (qk_t * decay_t).astype(BF16)
            inter_w = jnp.exp(m_prev - big_m)
            tot = (jnp.dot(v_t.astype(BF16), p_t, preferred_element_type=F32)
                   + inter_w * lax.dot_general(cn_prev.astype(BF16), qh, nt_dims, preferred_element_type=F32))
            num = tot[0:DH, :]
            inv = 1.0 / jnp.maximum(jnp.abs(tot[DH:DH + 1, :]), jnp.exp(-(b_row + big_m)))
            scale = inv * lax.rsqrt(inv * inv * jnp.mean(num * num, axis=0, keepdims=True) + EPS)
            hn = (num * scale).T * ng_ref[:, cols]
            y_ref[b, :, cols] = (hn * jax.nn.sigmoid(o_ref[b, :, cols].astype(F32))).astype(BF16)

            vw_t = (v_t * jnp.exp(r_row - r_max)).astype(BF16)
            cn_loc = jnp.dot(vw_t, kh, preferred_element_type=F32)
            m_keep = jnp.maximum(m_prev, r_max)
            cn_ref[idx] = jnp.exp(m_prev - m_keep) * cn_prev + jnp.exp(r_max - m_keep) * cn_loc
            m_ref[idx] = jnp.broadcast_to(g + m_keep, (1, LANES))


def _mlstm(proj3, gcol, grow, conv_w, norm_g):
    bsz, lp, _ = proj3.shape
    blk = lambda col: pl.BlockSpec((bsz, BLOCK, GROUP_W), lambda c: (0, c, col))
    return pl.pallas_call(
        _mlstm_kernel,
        grid=(lp // BLOCK,),
        in_specs=[
            blk(COL_MQ), blk(COL_MK), blk(COL_MV), blk(COL_MO),
            pl.BlockSpec((bsz, BLOCK, LANES), lambda c: (0, c, 0)),
            pl.BlockSpec((bsz, 1, 2 * SUBLANES, BLOCK), lambda c: (0, c, 0, 0)),
            pl.BlockSpec((ML_CONV, 2 * GROUP_W), lambda c: (0, 0)),
            pl.BlockSpec((1, GROUP_W), lambda c: (0, 0)),
        ],
        out_specs=pl.BlockSpec((bsz, BLOCK, GROUP_W), lambda c: (0, c, 0)),
        out_shape=jax.ShapeDtypeStruct((bsz, lp, GROUP_W), BF16),
        scratch_shapes=[
            pltpu.VMEM((bsz, SUBLANES + BLOCK, 2 * GROUP_W), F32),
            pltpu.VMEM((bsz * HEADS, DH + ML_SUM_ROWS, DH), F32),
            pltpu.VMEM((bsz * HEADS, 1, LANES), F32),
        ],
        compiler_params=_params(("arbitrary",)),
        name="mlstm",
    )(proj3, proj3, proj3, proj3, gcol, grow, conv_w, norm_g)


FOX_ROWS = 32
FOX_BIG = 1e30


def _split3(c):
    hi = c.astype(BF16).astype(F32)
    r = c - hi
    mid = r.astype(BF16).astype(F32)
    return hi, mid, r - mid


def _fox_kernel(q_ref, k_ref, v_ref, gcol_ref, y_ref, kx_ref, vx_ref, qx_ref, s_ref, p_ref, acc_ref,
                m_ref, al_ref, *, tq, lp):
    qi = pl.program_id(1)
    nq = lp // tq
    lane = lax.broadcasted_iota(jnp.int32, (tq, LANES), 1)
    nt_dims = (((1,), (1,)), ((), ()))

    @pl.when(qi == 0)
    def _():
        def build(cb, carry):
            r0 = pl.multiple_of(cb * tq, tq)
            rows = pl.ds(r0, tq)
            is_token = (r0 + lax.broadcasted_iota(jnp.int32, (tq, 1), 0)) >= PAD
            g = gcol_ref[rows, :]
            ones_col = jnp.where(lane == 0, 1.0, 0.0).astype(BF16)
            for h in range(HEADS):
                cols = slice(h * DH, (h + 1) * DH)
                hi, mid, lo = _split3(jnp.where(is_token, g[:, GC_FF + h:GC_FF + h + 1], FOX_BIG))
                ext = jnp.where(lane < 3, 1.0, jnp.where(lane == 3, -hi, jnp.where(lane == 4, -mid,
                                jnp.where(lane == 5, -lo, 0.0))))
                kx_ref[h, rows, 0:DH] = k_ref[rows, cols]
                kx_ref[h, rows, DH:] = ext.astype(BF16)
                vx_ref[h, rows, 0:DH] = v_ref[rows, cols]
                vx_ref[h, rows, DH:] = ones_col
            return carry

        lax.fori_loop(0, nq, build, 0)

    gq = gcol_ref[pl.ds(pl.multiple_of(qi * tq, tq), tq), :]
    row_io = lax.broadcasted_iota(jnp.int32, (FOX_ROWS, tq), 0)
    col_io = lax.broadcasted_iota(jnp.int32, (FOX_ROWS, tq), 1)

    for h in range(HEADS):
        hi, mid, lo = _split3(gq[:, GC_FF + h:GC_FF + h + 1])
        ext = jnp.where(lane == 0, hi, jnp.where(lane == 1, mid, jnp.where(lane == 2, lo,
                        jnp.where(lane < 6, 1.0, 0.0))))
        qx_ref[h, :, 0:DH] = q_ref[:, h * DH:(h + 1) * DH]
        qx_ref[h, :, DH:] = ext.astype(BF16)
    m_ref[...] = jnp.full(m_ref.shape, NEG, F32)
    acc_ref[...] = jnp.zeros_like(acc_ref)

    def scores(kb, h):
        krows = pl.ds(pl.multiple_of(kb * tq, tq), tq)
        s_ref[h] = lax.dot_general(qx_ref[h], kx_ref[h, krows, :], nt_dims, preferred_element_type=F32)

    def step(kb, diagonal):
        krows = pl.ds(pl.multiple_of(kb * tq, tq), tq)
        for h in range(HEADS):
            for r0 in range(0, tq, FOX_ROWS):
                rr = slice(r0, r0 + FOX_ROWS)
                sc = s_ref[h, rr, :]
                if diagonal:
                    sc = jnp.where(col_io <= row_io + r0, sc, NEG)
                m_old = m_ref[h, rr, :]
                m_new = jnp.maximum(m_old, jnp.max(sc, axis=-1, keepdims=True))
                al_ref[h, rr, :] = jnp.exp(m_old - m_new)
                m_ref[h, rr, :] = m_new
                p_ref[h, rr, :] = jnp.exp(sc - m_new).astype(BF16)
            if not diagonal:
                scores(kb + 1, h)
            pv = jnp.dot(p_ref[h], vx_ref[h, krows, :], preferred_element_type=F32)
            acc_ref[h] = al_ref[h] * acc_ref[h] + pv

    def off_diagonal(kb, carry):
        step(kb, False)
        return carry

    for h in range(HEADS):
        scores(0, h)
    lax.fori_loop(0, qi, off_diagonal, 0)
    step(qi, True)
    for h in range(HEADS):
        acc = acc_ref[h]
        y_ref[:, h * DH:(h + 1) * DH] = (acc[:, 0:DH] / acc[:, DH:DH + 1]).astype(BF16)


def _fox(proj3, gcol):
    bsz, lp, _ = proj3.shape
    tq = _pick(lp, 384, BLOCK)
    nq = lp // tq
    return pl.pallas_call(
        functools.partial(_fox_kernel, tq=tq, lp=lp),
        grid=(bsz, nq),
        in_specs=[
            pl.BlockSpec((None, tq, GROUP_W), lambda b, i: (b, i, COL_FQ)),
            pl.BlockSpec((None, lp, GROUP_W), lambda b, i: (b, 0, COL_FK)),
            pl.BlockSpec((None, lp, GROUP_W), lambda b, i: (b, 0, COL_FV)),
            pl.BlockSpec((None, lp, LANES), lambda b, i: (b, 0, 0)),
        ],
        out_specs=pl.BlockSpec((tq, GROUP_W), lambda b, i: (b * nq + i, 0)),
        out_shape=jax.ShapeDtypeStruct((bsz * lp, GROUP_W), BF16),
        scratch_shapes=[
            pltpu.VMEM((HEADS, lp, 2 * DH), BF16),
            pltpu.VMEM((HEADS, lp, 2 * DH), BF16),
            pltpu.VMEM((HEADS, tq, 2 * DH), BF16),
            pltpu.VMEM((HEADS, tq, tq), F32),
            pltpu.VMEM((HEADS, tq, tq), BF16),
            pltpu.VMEM((HEADS, tq, 2 * DH), F32),
            pltpu.VMEM((HEADS, tq, 1), F32),
            pltpu.VMEM((HEADS, tq, 1), F32),
        ],
        compiler_params=_params(("arbitrary", "arbitrary")),
        name="fox",
    )(proj3, proj3, proj3, gcol)


def _pool_rows(u_ref, pw_ref, ps_ref, buf_ref, t, tm):
    @pl.when(t == 0)
    def _():
        buf_ref[0:POOL_MAXW, :] = jnp.zeros((POOL_MAXW, GROUP_W), F32)

    @pl.when(t > 0)
    def _():
        buf_ref[0:POOL_MAXW, :] = buf_ref[tm:tm + POOL_MAXW, :]

    buf_ref[POOL_MAXW:POOL_MAXW + tm, :] = u_ref[...].astype(F32)
    pos = t * tm + lax.broadcasted_iota(jnp.int32, (tm, 1), 0) - (PAD - 1)
    posf = jnp.maximum(pos, 1).astype(F32)
    gw = GROUP_W // len(POOL_WINDOWS)
    out = []
    for gi, w in enumerate(POOL_WINDOWS):
        cols = slice(gi * gw, (gi + 1) * gw)
        x = buf_ref[POOL_MAXW:POOL_MAXW + tm, cols]
        s = x
        for j in range(1, w):
            s = s + buf_ref[POOL_MAXW - j:POOL_MAXW - j + tm, cols]
        pooled = s / jnp.minimum(posf, float(w)) - x
        mixed = jnp.dot(pooled.astype(BF16), pw_ref[gi], preferred_element_type=F32)
        out.append((mixed * ps_ref[:, cols]).astype(BF16))
    return jnp.concatenate(out, axis=1)


def _outproj_kernel(ys_ref, ym_ref, yf_ref, pu_ref, pw_ref, ps_ref, w_ref, g_ref, h_ref, o_ref, buf_ref,
                    *, tm, blocks_per_seq):
    t = pl.program_id(0) % blocks_per_seq
    y_pool = _pool_rows(pu_ref, pw_ref, ps_ref, buf_ref, t, tm)
    mix = jnp.dot(ys_ref[...], w_ref[0:GROUP_W, :], preferred_element_type=F32)
    for n, y_ref in enumerate((ym_ref, yf_ref), start=1):
        mix = mix + jnp.dot(y_ref[...], w_ref[n * GROUP_W:(n + 1) * GROUP_W, :], preferred_element_type=F32)
    mix = mix + jnp.dot(y_pool, w_ref[3 * GROUP_W:4 * GROUP_W, :], preferred_element_type=F32)
    valid = (t * tm + lax.broadcasted_iota(jnp.int32, (tm, 1), 0)) >= PAD
    o_ref[...] = jnp.where(valid, h_ref[...] + _rms(mix, g_ref[...]), 0.0)


def _outproj(ys, ym, yf, proj, pool_w, pool_scale, w_out_all, g, h2, lp, l):
    m = h2.shape[0]
    tm = _pick(lp, 528, 16)
    gw = GROUP_W // len(POOL_WINDOWS)
    yspec = pl.BlockSpec((tm, GROUP_W), lambda i: (i, 0))
    return pl.pallas_call(
        functools.partial(_outproj_kernel, tm=tm, blocks_per_seq=lp // tm),
        grid=(m // tm,),
        in_specs=[
            yspec, yspec, yspec,
            pl.BlockSpec((tm, GROUP_W), lambda i: (i, COL_PU)),
            pl.BlockSpec((len(POOL_WINDOWS), gw, gw), lambda i: (0, 0, 0)),
            pl.BlockSpec((1, GROUP_W), lambda i: (0, 0)),
            pl.BlockSpec((None, D_MODEL, D_MODEL), lambda i: (l, 0, 0)),
            pl.BlockSpec((1, D_MODEL), lambda i: (0, 0)),
            pl.BlockSpec((tm, D_MODEL), lambda i: (i, 0)),
        ],
        out_specs=pl.BlockSpec((tm, D_MODEL), lambda i: (i, 0)),
        out_shape=jax.ShapeDtypeStruct((m, D_MODEL), F32),
        scratch_shapes=[pltpu.VMEM((POOL_MAXW + tm, GROUP_W), F32)],
        input_output_aliases={8: 0},
        compiler_params=_params(("arbitrary",)),
        name="outproj",
    )(ys, ym, yf, proj, pool_w, pool_scale, w_out_all, g, h2)


def _ffn_kernel(h_ref, g1_ref, w1_ref, w2_ref, g2_ref, o_ref, hn_ref, acc_ref):
    f = pl.program_id(1)
    last = pl.num_programs(1) - 1
    tm = h_ref.shape[0]
    halves = (slice(0, tm // 2), slice(tm // 2, tm))

    def update(rows):
        a = jnp.maximum(jnp.dot(hn_ref[rows, :], w1_ref[...], preferred_element_type=F32), 0.0)
        return jnp.dot((a * a).astype(BF16), w2_ref[...], preferred_element_type=F32)

    @pl.when(f == 0)
    def _():
        for rows in halves:
            hn_ref[rows, :] = _rms(h_ref[rows, :], g1_ref[...]).astype(BF16)
            acc_ref[rows, :] = update(rows)

    @pl.when((f > 0) & (f < last))
    def _():
        acc_ref[...] += update(slice(0, tm))

    @pl.when(f == last)
    def _():
        for rows in halves:
            o_ref[rows, :] = h_ref[rows, :] + _rms(acc_ref[rows, :] + update(rows), g2_ref[...])


FFN_TF = 1024


def _ffn_specs(l, tm, h_spec):
    return [
        h_spec,
        pl.BlockSpec((1, D_MODEL), lambda i, f: (0, 0)),
        pl.BlockSpec((None, D_MODEL, FFN_TF), lambda i, f: (l, 0, f)),
        pl.BlockSpec((None, FFN_TF, D_MODEL), lambda i, f: (l, f, 0)),
        pl.BlockSpec((1, D_MODEL), lambda i, f: (0, 0)),
    ], [pltpu.VMEM((tm, D_MODEL), BF16), pltpu.VMEM((tm, D_MODEL), F32)]


def _ffn(h2, g1, w1_all, w2_all, g2, l):
    m = h2.shape[0]
    tm = _pick(m, 768, 16)
    in_specs, scratch = _ffn_specs(l, tm, pl.BlockSpec((tm, D_MODEL), lambda i, f: (i, 0)))
    return pl.pallas_call(
        _ffn_kernel,
        grid=(m // tm, D_FF // FFN_TF),
        in_specs=in_specs,
        out_specs=pl.BlockSpec((tm, D_MODEL), lambda i, f: (i, 0)),
        out_shape=jax.ShapeDtypeStruct((m, D_MODEL), F32),
        scratch_shapes=scratch,
        input_output_aliases={0: 0},
        compiler_params=_params(("arbitrary", "arbitrary")),
        name="ffn",
    )(h2, g1, w1_all, w2_all, g2)


def _ffn_final(h2, g1, w1_all, w2_all, g2, l, bsz, lp):
    seq = lp - PAD - N_META
    tm = _pick(seq, 512, 16)
    per_seq = seq // tm
    h_spec = pl.BlockSpec((pl.Element(tm), pl.Element(D_MODEL)),
                          lambda i, f: (pl.multiple_of((i // per_seq) * lp + PAD + N_META + (i % per_seq) * tm,
                                                       SUBLANES), 0))
    in_specs, scratch = _ffn_specs(l, tm, h_spec)
    return pl.pallas_call(
        _ffn_kernel,
        grid=(bsz * per_seq, D_FF // FFN_TF),
        in_specs=in_specs,
        out_specs=pl.BlockSpec((tm, D_MODEL), lambda i, f: (i, 0)),
        out_shape=jax.ShapeDtypeStruct((bsz * seq, D_MODEL), F32),
        scratch_shapes=scratch,
        compiler_params=_params(("arbitrary", "arbitrary")),
        name="ffn_final",
    )(h2, g1, w1_all, w2_all, g2)


IN_COLS = N_MAIN + 3 * HEADS
IN_COLS_PAD = -(-IN_COLS // LANES) * LANES
O_MI = 5 * GROUP_W
O_FQ = O_MI + 2 * HEADS
O_FF = O_FQ + 3 * GROUP_W
O_PU = O_FF + HEADS


SPLIT_BLOCKS = N_MAIN // LANES
assert O_MI % LANES == 0 and (O_FQ - O_MI) + 3 * GROUP_W == (O_FF - O_MI) and O_FF % LANES == GC_FF


def _split_w_in_kernel(a_ref, b_ref, main_ref, gate_ref):
    d = pl.program_id(0)
    depth = a_ref.shape[1]
    col = lax.broadcasted_iota(jnp.int32, (LANES, D_MODEL), 0)

    def emit(shift):
        for l in range(depth):
            a = a_ref[:, l, :]
            if shift:
                a = pltpu.roll(jnp.where(col >= shift, a, b_ref[:, l, :]), LANES - shift, axis=0)
            main_ref[l] = a.T.astype(BF16)

    wide_before_ml_gates = O_MI // LANES
    wide_before_fx_gate = (O_FF - (O_FQ - O_MI)) // LANES

    @pl.when(d < wide_before_ml_gates)
    def _():
        emit(0)

    @pl.when((d >= wide_before_ml_gates) & (d < wide_before_fx_gate))
    def _():
        emit(O_FQ - O_MI)

    @pl.when(d >= wide_before_fx_gate)
    def _():
        emit(O_PU - O_MI - 3 * GROUP_W)

    @pl.when(d == 0)
    def _():
        gate_ref[...] = jnp.zeros_like(gate_ref)

    @pl.when(d == wide_before_ml_gates)
    def _():
        for l in range(depth):
            gate_ref[l] += jnp.where(col < GC_FF, a_ref[:, l, :], 0.0).T

    @pl.when(d == O_FF // LANES)
    def _():
        for l in range(depth):
            g_fx = jnp.where(col >= GC_FF, jnp.where(col < GC_FF + HEADS, a_ref[:, l, :], 0.0), 0.0)
            gate_ref[l] += g_fx.T


def _split_w_in(w_all):
    depth = w_all.shape[0]
    w_t = jnp.transpose(w_all, (2, 0, 1))
    blk = lambda off: pl.BlockSpec((LANES, depth, D_MODEL), lambda d: (d + off, 0, 0))
    return pl.pallas_call(
        _split_w_in_kernel,
        grid=(SPLIT_BLOCKS,),
        in_specs=[blk(0), blk(1)],
        out_specs=[pl.BlockSpec((depth, D_MODEL, LANES), lambda d: (0, 0, d)),
                   pl.BlockSpec((depth, D_MODEL, LANES), lambda d: (0, 0, 0))],
        out_shape=[jax.ShapeDtypeStruct((depth, D_MODEL, N_MAIN), BF16),
                   jax.ShapeDtypeStruct((depth, D_MODEL, LANES), F32)],
        compiler_params=_params(("arbitrary",)),
        name="split_w_in",
    )(w_t, w_t)


def _s5_matrices(bbre, bbim, c_re, c_im):
    gh = SSM_G // 2
    mask = (jnp.arange(GROUP_W // 2)[:, None] // SSM_H == jnp.arange(S5_HALF)[None, :] // SSM_P).astype(F32)
    c_t = lambda c: c.transpose(0, 2, 1).reshape(SSM_N, SSM_H)
    bmats, cmats = [], []
    for k in range(2):
        st = slice(S5_HALF * k, S5_HALF * (k + 1))
        expand = lambda bb: jnp.tile(bb[:, st], (gh, 1)) * mask
        cont = lambda c: jnp.tile(c_t(c)[st, :], (1, gh)) * mask.T
        bmats.append(jnp.concatenate([expand(bbre), expand(bbim)], axis=1))
        cmats.append(jnp.concatenate([cont(c_re), -cont(c_im)], axis=0))
    return jnp.stack(bmats).astype(BF16), jnp.stack(cmats).astype(BF16)


def kernel(x, meta_tokens, g_pre_mix, g_post_mix, g_pre_ffn, g_post_ffn, w_in, ml_gate_bias, fx_gate_bias, ssm_lam_re, ssm_lam_im, ssm_log_dt, ssm_b_re, ssm_b_im, ssm_c_re, ssm_c_im, ssm_d, ssm_glu_w, ssm_glu_b, ml_conv_w, ml_norm_g, pool_w, pool_scale, w_out, mlp_w1, mlp_w2):
    bsz, seq, d = x.shape
    depth = w_in.shape[0]
    lp = PAD + N_META + seq
    m = bsz * lp
    assert d == D_MODEL and lp % BLOCK == 0
    row = lambda a: a.reshape(1, -1).astype(F32)
    colscale = jnp.ones((1, N_MAIN), F32).at[:, COL_FQ * GROUP_W:(COL_FQ + 1) * GROUP_W].set(DH ** -0.5)
    w_out16, w1_16, w2_16 = w_out.astype(BF16), mlp_w1.astype(BF16), mlp_w2.astype(BF16)

    w_main, w_gate = _split_w_in(w_in)
    w_gate = w_gate.astype(BF16)

    for l in range(depth):
        if l == 0:
            proj, graw, h = _inproj_first(x.reshape(bsz * seq, d).astype(F32), meta_tokens.astype(F32),
                                          row(g_pre_mix[l]), w_main, w_gate, colscale, bsz, lp)
        else:
            proj, graw = _inproj(h, row(g_pre_mix[l]), w_main, w_gate, colscale, l)
        proj3 = proj.reshape(bsz, lp, N_MAIN)

        gbias = jnp.pad(jnp.concatenate([ml_gate_bias[l], fx_gate_bias[l]]).astype(F32), (0, LANES - 3 * HEADS))
        gcol, grow = _gates(graw, gbias.reshape(1, LANES), bsz, lp)

        consts, bbre, bbim = _s5_prep(ssm_lam_re[l], ssm_lam_im[l], ssm_log_dt[l], ssm_b_re[l], ssm_b_im[l], bsz)
        bmat, cmat = _s5_matrices(bbre, bbim, ssm_c_re[l], ssm_c_im[l])
        y_ssm = _s5(proj3, bmat, cmat, row(ssm_d[l]), consts, ssm_glu_w[l].astype(BF16), row(ssm_glu_b[l]))
        y_ml = _mlstm(proj3, gcol, grow, ml_conv_w[l].astype(F32), row(ml_norm_g[l]))
        y_fx = _fox(proj3, gcol)

        h = _outproj(y_ssm.reshape(m, GROUP_W), y_ml.reshape(m, GROUP_W), y_fx,
                     proj, pool_w[l].astype(BF16), row(pool_scale[l]),
                     w_out16, row(g_post_mix[l]), h, lp, l)
        if l + 1 < depth:
            h = _ffn(h, row(g_pre_ffn[l]), w1_16, w2_16, row(g_post_ffn[l]), l)
        else:
            h = _ffn_final(h, row(g_pre_ffn[l]), w1_16, w2_16, row(g_post_ffn[l]), l, bsz, lp)

    return h.reshape(bsz, seq, d)
```

```python
import functools
import math

import jax
import jax.numpy as jnp
from jax import lax
from jax.experimental import pallas as pl
from jax.experimental.pallas import tpu as pltpu

F32 = jnp.float32
BF16 = jnp.bfloat16

D_MODEL = 2048
N_META = 16
BLOCK = 128
PAD = BLOCK - N_META
GROUP_W = 512
SSM_H = 16
SSM_G = GROUP_W // SSM_H
SSM_P = 64
SSM_N = SSM_G * SSM_P
HEADS = 4
DH = GROUP_W // HEADS
ML_CONV = 4
POOL_WINDOWS = (2, 4, 8, 16)
POOL_MAXW = 16
D_FF = 4 * D_MODEL
EPS = 1e-6
NEG = -1e30
N_MAIN = 9 * GROUP_W
LANES = 128
SUBLANES = 8
VMEM_LIMIT = 60 * 1024 * 1024

COL_SU, COL_MQ, COL_MK, COL_MV, COL_MO, COL_FQ, COL_FK, COL_FV, COL_PU = range(9)
GC_MI, GC_MF, GC_FF, GC_RM = 0, HEADS, 2 * HEADS, 3 * HEADS


def _pick(n, target, mult):
    best = None
    for d in range(mult, min(n, target) + 1, mult):
        if n % d == 0:
            best = d
    assert best is not None, (n, target, mult)
    return best


def _params(sem):
    return pltpu.CompilerParams(dimension_semantics=sem, vmem_limit_bytes=VMEM_LIMIT)


def _rms(x, g):
    return x * lax.rsqrt(jnp.mean(x * x, axis=-1, keepdims=True) + EPS) * g


def _inproj_steps(src_ref, g_ref, w_ref, wg_ref, cs_ref, o_ref, og_ref, xn_ref):
    tm = src_ref.shape[0]

    def project(rows):
        acc = jnp.dot(xn_ref[rows, :], w_ref[...], preferred_element_type=F32)
        o_ref[rows, :] = (acc * cs_ref[...]).astype(BF16)

    @pl.when(pl.program_id(1) == 0)
    def _():
        for rows in (slice(0, tm // 2), slice(tm // 2, tm)):
            xn = _rms(src_ref[rows, :], g_ref[...]).astype(BF16)
            xn_ref[rows, :] = xn
            og_ref[rows, :] = jnp.dot(xn, wg_ref[...], preferred_element_type=F32)
            project(rows)

    @pl.when(pl.program_id(1) > 0)
    def _():
        project(slice(0, tm))


def _inproj_kernel(x_ref, g_ref, w_ref, wg_ref, cs_ref, o_ref, og_ref, xn_ref):
    _inproj_steps(x_ref, g_ref, w_ref, wg_ref, cs_ref, o_ref, og_ref, xn_ref)


def _inproj(h2, g, w_main, w_gate, colscale, l):
    m = h2.shape[0]
    tm = _pick(m, 1056, 16)
    tn = 1536
    return pl.pallas_call(
        _inproj_kernel,
        grid=(m // tm, N_MAIN // tn),
        in_specs=[
            pl.BlockSpec((tm, D_MODEL), lambda i, j: (i, 0)),
            pl.BlockSpec((1, D_MODEL), lambda i, j: (0, 0)),
            pl.BlockSpec((None, D_MODEL, tn), lambda i, j: (l, 0, j)),
            pl.BlockSpec((None, D_MODEL, LANES), lambda i, j: (l, 0, 0)),
            pl.BlockSpec((1, tn), lambda i, j: (0, j)),
        ],
        out_specs=[
            pl.BlockSpec((tm, tn), lambda i, j: (i, j)),
            pl.BlockSpec((tm, LANES), lambda i, j: (i, 0)),
        ],
        out_shape=[jax.ShapeDtypeStruct((m, N_MAIN), BF16), jax.ShapeDtypeStruct((m, LANES), F32)],
        scratch_shapes=[pltpu.VMEM((tm, D_MODEL), BF16)],
        compiler_params=_params(("arbitrary", "arbitrary")),
        name="inproj",
    )(h2, g, w_main, w_gate, colscale)


def _inproj_first_kernel(x_ref, meta_ref, g_ref, w_ref, wg_ref, cs_ref, o_ref, og_ref, h_ref, xn_ref,
                         *, tm, tiles_per_seq):
    head = PAD + N_META
    first_n = pl.program_id(1) == 0
    seq_start = pl.program_id(0) % tiles_per_seq == 0

    @pl.when(first_n & seq_start)
    def _():
        h_ref[0:PAD, :] = jnp.zeros((PAD, D_MODEL), F32)
        h_ref[PAD:head, :] = meta_ref[...]
        h_ref[head:tm, :] = x_ref[0:tm - head, :]

    @pl.when(first_n & jnp.logical_not(seq_start))
    def _():
        h_ref[...] = x_ref[...]

    _inproj_steps(h_ref, g_ref, w_ref, wg_ref, cs_ref, o_ref, og_ref, xn_ref)


def _inproj_first(x2, meta, g, w_main, w_gate, colscale, bsz, lp, l=0):
    seq = lp - PAD - N_META
    tm = _pick(lp, min(704, lp // 2), 16)
    assert tm > PAD + N_META
    per_seq = lp // tm
    tn = 1536
    m = bsz * lp

    def x_index(i, j):
        b, t = i // per_seq, i % per_seq
        return pl.multiple_of(b * seq + jnp.maximum(t * tm - (PAD + N_META), 0), SUBLANES), 0

    return pl.pallas_call(
        functools.partial(_inproj_first_kernel, tm=tm, tiles_per_seq=per_seq),
        grid=(m // tm, N_MAIN // tn),
        in_specs=[
            pl.BlockSpec((pl.Element(tm), pl.Element(D_MODEL)), x_index),
            pl.BlockSpec((N_META, D_MODEL), lambda i, j: (0, 0)),
            pl.BlockSpec((1, D_MODEL), lambda i, j: (0, 0)),
            pl.BlockSpec((None, D_MODEL, tn), lambda i, j: (l, 0, j)),
            pl.BlockSpec((None, D_MODEL, LANES), lambda i, j: (l, 0, 0)),
            pl.BlockSpec((1, tn), lambda i, j: (0, j)),
        ],
        out_specs=[
            pl.BlockSpec((tm, tn), lambda i, j: (i, j)),
            pl.BlockSpec((tm, LANES), lambda i, j: (i, 0)),
            pl.BlockSpec((tm, D_MODEL), lambda i, j: (i, 0)),
        ],
        out_shape=[jax.ShapeDtypeStruct((m, N_MAIN), BF16), jax.ShapeDtypeStruct((m, LANES), F32),
                   jax.ShapeDtypeStruct((m, D_MODEL), F32)],
        scratch_shapes=[pltpu.VMEM((tm, D_MODEL), BF16)],
        compiler_params=_params(("arbitrary", "arbitrary")),
        name="inproj_first",
    )(x2, meta, g, w_main, w_gate, colscale)


def _gates_kernel(raw_ref, bias_ref, col_ref, row_ref, carry_ref):
    c = pl.program_id(0)

    @pl.when(c == 0)
    def _():
        carry_ref[...] = jnp.zeros_like(carry_ref)

    r_io = lax.broadcasted_iota(jnp.int32, (BLOCK, LANES), 0)
    c_io = lax.broadcasted_iota(jnp.int32, (BLOCK, LANES), 1)
    valid = (r_io + c * BLOCK) >= PAD
    tri = (r_io >= c_io).astype(F32)
    for b in range(raw_ref.shape[0]):
        g = raw_ref[b] + bias_ref[...]
        logsig = jnp.minimum(g, 0.0) - jnp.log(1.0 + jnp.exp(-jnp.abs(g)))
        x = jnp.where(valid, logsig, 0.0)
        x = jnp.where(c_io >= GC_MF, jnp.where(c_io < GC_FF + HEADS, x, 0.0), 0.0)
        cum = jnp.dot(tri, x, precision=lax.Precision.HIGHEST, preferred_element_type=F32)
        cum = cum + jnp.where(c_io >= GC_FF, carry_ref[b], 0.0)
        carry_ref[b] = cum[BLOCK - 1:BLOCK, :]
        log_i = jnp.where(valid, g, NEG)
        r = log_i - pltpu.roll(cum, LANES - GC_MF, axis=1)
        rmax = r
        d = 1
        while d < BLOCK:
            rmax = jnp.maximum(rmax, jnp.where(r_io >= d, pltpu.roll(rmax, d, axis=0), NEG))
            d *= 2
        out = jnp.where(c_io < GC_MF, r, jnp.where(c_io < GC_RM, cum, jnp.where(
            c_io < GC_RM + HEADS, pltpu.roll(rmax, GC_RM, axis=1), 0.0)))
        col_ref[b] = out
        row_ref[b, 0] = out.T[0:2 * SUBLANES, :]


def _gates(raw, bias, bsz, lp):
    nc = lp // BLOCK
    return pl.pallas_call(
        _gates_kernel,
        grid=(nc,),
        in_specs=[
            pl.BlockSpec((bsz, BLOCK, LANES), lambda c: (0, c, 0)),
            pl.BlockSpec((1, LANES), lambda c: (0, 0)),
        ],
        out_specs=[
            pl.BlockSpec((bsz, BLOCK, LANES), lambda c: (0, c, 0)),
            pl.BlockSpec((bsz, 1, 2 * SUBLANES, BLOCK), lambda c: (0, c, 0, 0)),
        ],
        out_shape=[
            jax.ShapeDtypeStruct((bsz, lp, LANES), F32),
            jax.ShapeDtypeStruct((bsz, nc, 2 * SUBLANES, BLOCK), F32),
        ],
        scratch_shapes=[pltpu.VMEM((bsz, 1, LANES), F32)],
        compiler_params=_params(("arbitrary",)),
        name="gates",
    )(raw.reshape(bsz, lp, LANES), bias)


S5_CONST_ROWS = 64
S5_STRIPS = SSM_N // LANES
S5_STRIPS_PER_LOOP = 2
S5_HALF = SSM_N // 2


def _s5_hs_steps(bsz):
    return [d for d in (1, 2, 4) if d < SUBLANES // bsz]


def _s5_prep_kernel(lre_ref, lim_ref, ldt_ref, bre_ref, bim_ref, consts_ref, bbre_ref, bbim_ref, *, bsz):
    lre = lre_ref[...]
    lim = lim_ref[...]
    dt = jnp.exp(ldt_ref[...])
    r_io = lax.broadcasted_iota(jnp.int32, (SUBLANES, SSM_N), 0)
    kk = (r_io + 1).astype(F32)
    mag = jnp.exp(kk * (lre * dt))
    ang = kk * (lim * dt)
    pw_re = mag * jnp.cos(ang)
    pw_im = mag * jnp.sin(ang)
    consts_ref[...] = jnp.zeros_like(consts_ref)
    steps = _s5_hs_steps(bsz)
    for i, d in enumerate(steps):
        keep = r_io >= d * bsz
        consts_ref[16 * i:16 * i + 8, :] = jnp.where(keep, pw_re[d - 1:d, :], 0.0)
        consts_ref[16 * i + 8:16 * i + 16, :] = jnp.where(keep, pw_im[d - 1:d, :], 0.0)
    t_in_vreg = lax.shift_right_logical(r_io, bsz.bit_length() - 1)
    cw_re = jnp.zeros((SUBLANES, SSM_N), F32)
    cw_im = jnp.zeros((SUBLANES, SSM_N), F32)
    for j in range(SUBLANES // bsz):
        cw_re = jnp.where(t_in_vreg == j, pw_re[j:j + 1, :], cw_re)
        cw_im = jnp.where(t_in_vreg == j, pw_im[j:j + 1, :], cw_im)
    base = 16 * len(steps)
    consts_ref[base:base + 8, :] = cw_re
    consts_ref[base + 8:base + 16, :] = cw_im
    nr = pw_re[0:1, :] - 1.0
    ni = pw_im[0:1, :]
    den = lre * lre + lim * lim
    cr = (nr * lre + ni * lim) / den
    ci = (ni * lre - nr * lim) / den
    bre = bre_ref[...]
    bim = bim_ref[...]
    bbre_ref[...] = cr * bre - ci * bim
    bbim_ref[...] = cr * bim + ci * bre


def _s5_prep(lam_re, lam_im, log_dt, b_re, b_im, bsz):
    flat = lambda a: a.reshape(1, SSM_N)
    ldt = jnp.repeat(log_dt, SSM_P).reshape(1, SSM_N)
    bt = lambda a: a.transpose(2, 0, 1).reshape(SSM_H, SSM_N)
    return pl.pallas_call(
        functools.partial(_s5_prep_kernel, bsz=bsz),
        out_shape=[
            jax.ShapeDtypeStruct((S5_CONST_ROWS, SSM_N), F32),
            jax.ShapeDtypeStruct((SSM_H, SSM_N), F32),
            jax.ShapeDtypeStruct((SSM_H, SSM_N), F32),
        ],
        name="s5_prep",
    )(flat(lam_re), flat(lam_im), ldt, bt(b_re), bt(b_im))


def _s5_kernel(u_ref, bmat_ref, cmat_ref, d_ref, consts_ref, gw_ref, gb_ref, y_ref,
               uil_ref, oil_ref, x_ref, carry_ref, *, tb, bsz):
    rows = bsz * tb
    n_us = GROUP_W // LANES

    @pl.when(pl.program_id(0) == 0)
    def _():
        carry_ref[...] = jnp.zeros_like(carry_ref)

    for b in range(bsz):
        ub = u_ref[b].astype(F32)
        for s in range(n_us):
            uil_ref[s, pl.ds(b, tb, stride=bsz), :] = ub[:, s * LANES:(s + 1) * LANES]
    u = jnp.concatenate([uil_ref[s] for s in range(n_us)], axis=1)
    u16 = u.astype(BF16)
    half_w = GROUP_W // 2
    n_groups = S5_STRIPS // S5_STRIPS_PER_LOOP
    gw_lanes = S5_STRIPS_PER_LOOP * LANES

    def group_lanes(g):
        k, j = divmod(g, n_groups // 2)
        re0 = j * gw_lanes
        return k, slice(re0, re0 + gw_lanes), slice(S5_HALF + re0, S5_HALF + re0 + gw_lanes)

    def expand(g):
        k, re, im = group_lanes(g)
        uk = u16[:, half_w * k:half_w * (k + 1)]
        base = 2 * S5_HALF * k
        for part in (re, im):
            x_ref[:, base + part.start:base + part.stop] = jnp.dot(
                uk, bmat_ref[k, :, part], preferred_element_type=F32)

    def contract(g):
        k, re, im = group_lanes(g)
        base = 2 * S5_HALF * k
        return sum(jnp.dot(x_ref[:, base + part.start:base + part.stop].astype(BF16), cmat_ref[k, part, :],
                           preferred_element_type=F32) for part in (re, im))

    def cmul_add(xr, xi, ar, ai, sr, si):
        return xr + ar * sr - ai * si, xi + ar * si + ai * sr

    row8 = lax.broadcasted_iota(jnp.int32, (SUBLANES, LANES), 0)
    steps = _s5_hs_steps(bsz)
    cbase = 16 * len(steps)

    def last_step_tile(c):
        span = SUBLANES
        while span > bsz:
            half = span // 2
            c = jnp.where((row8 & (span - 1)) < half, pltpu.roll(c, half, axis=0), c)
            span = half
        return c

    def lanes_of(s):
        k, j = divmod(s, S5_STRIPS // 2)
        re0 = 2 * S5_HALF * k + j * LANES
        return slice(re0, re0 + LANES), slice(re0 + S5_HALF, re0 + S5_HALF + LANES), slice(s * LANES, (s + 1) * LANES)

    def scan(g):
        strips = range(g * S5_STRIPS_PER_LOOP, (g + 1) * S5_STRIPS_PER_LOOP)
        carry = []
        for s in strips:
            lr, li, _ = lanes_of(s)
            carry += [carry_ref[:, lr], carry_ref[:, li]]
        for r in range(rows // SUBLANES):
            r8 = slice(r * SUBLANES, (r + 1) * SUBLANES)
            for n, s in enumerate(strips):
                lr, li, lc = lanes_of(s)
                xr = x_ref[r8, lr]
                xi = x_ref[r8, li]
                for i, d in enumerate(steps):
                    xr, xi = cmul_add(xr, xi, consts_ref[16 * i:16 * i + 8, lc], consts_ref[16 * i + 8:16 * i + 16, lc],
                                      pltpu.roll(xr, d * bsz, axis=0), pltpu.roll(xi, d * bsz, axis=0))
                xr, xi = cmul_add(xr, xi, consts_ref[cbase:cbase + 8, lc], consts_ref[cbase + 8:cbase + 16, lc],
                                  carry[2 * n], carry[2 * n + 1])
                x_ref[r8, lr] = xr
                x_ref[r8, li] = xi
                carry[2 * n] = last_step_tile(xr)
                carry[2 * n + 1] = last_step_tile(xi)
        for n, s in enumerate(strips):
            lr, li, _ = lanes_of(s)
            carry_ref[:, lr] = carry[2 * n]
            carry_ref[:, li] = carry[2 * n + 1]

    expand(0)
    y_half = [None, None]
    for g in range(n_groups):
        if g + 1 < n_groups:
            expand(g + 1)
        scan(g)
        k = group_lanes(g)[0]
        y_half[k] = contract(g) if y_half[k] is None else y_half[k] + contract(g)
    y = jnp.concatenate(y_half, axis=1)
    y = jax.nn.gelu(y + d_ref[...] * u, approximate=True)
    z = jnp.dot(y.astype(BF16), gw_ref[...], preferred_element_type=F32) + gb_ref[...]
    out = z[:, :GROUP_W] * jax.nn.sigmoid(z[:, GROUP_W:])
    for s in range(n_us):
        oil_ref[s] = out[:, s * LANES:(s + 1) * LANES]
    for b in range(bsz):
        y_ref[b] = jnp.concatenate([oil_ref[s, pl.ds(b, tb, stride=bsz), :] for s in range(n_us)],
                                   axis=1).astype(BF16)


def _s5(proj3, bmat, cmat, dvec, consts, glu_w, glu_b):
    bsz, lp, _ = proj3.shape
    assert SUBLANES % bsz == 0
    tb = _pick(lp, 176, 16)
    rows = bsz * tb
    const = lambda shape: pl.BlockSpec(shape, lambda t: (0,) * len(shape))
    return pl.pallas_call(
        functools.partial(_s5_kernel, tb=tb, bsz=bsz),
        grid=(lp // tb,),
        in_specs=[
            pl.BlockSpec((bsz, tb, GROUP_W), lambda t: (0, t, COL_SU)),
            const((2, GROUP_W // 2, 2 * S5_HALF)),
            const((2, 2 * S5_HALF, GROUP_W // 2)),
            const((1, GROUP_W)),
            const((S5_CONST_ROWS, SSM_N)),
            const((GROUP_W, 2 * GROUP_W)),
            const((1, 2 * GROUP_W)),
        ],
        out_specs=pl.BlockSpec((bsz, tb, GROUP_W), lambda t: (0, t, 0)),
        out_shape=jax.ShapeDtypeStruct((bsz, lp, GROUP_W), BF16),
        scratch_shapes=[
            pltpu.VMEM((GROUP_W // LANES, rows, LANES), F32),
            pltpu.VMEM((GROUP_W // LANES, rows, LANES), F32),
            pltpu.VMEM((rows, 2 * SSM_N), F32),
            pltpu.VMEM((SUBLANES, 2 * SSM_N), F32),
        ],
        compiler_params=_params(("arbitrary",)),
        name="s5",
    )(proj3, bmat, cmat, dvec, consts, glu_w, glu_b)


def _mlstm_kernel(q_ref, k_ref, v_ref, o_ref, gcol_ref, grow_ref, cw_ref, ng_ref, y_ref,
                  buf_ref, cn_ref, m_ref):
    bsz = q_ref.shape[0]

    @pl.when(pl.program_id(0) == 0)
    def _():
        buf_ref[:, 0:SUBLANES, :] = jnp.zeros((bsz, SUBLANES, 2 * GROUP_W), F32)
        cn_ref[...] = jnp.zeros_like(cn_ref)
        m_ref[...] = jnp.zeros_like(m_ref)

    @pl.when(pl.program_id(0) > 0)
    def _():
        buf_ref[:, 0:SUBLANES, :] = buf_ref[:, BLOCK:BLOCK + SUBLANES, :]

    s_io = lax.broadcasted_iota(jnp.int32, (BLOCK, BLOCK), 0)
    t_io = lax.broadcasted_iota(jnp.int32, (BLOCK, BLOCK), 1)
    causal = s_io <= t_io
    ones_row = jnp.where(s_io == 0, 1.0, 0.0)
    nt_dims = (((1,), (1,)), ((), ()))

    for b in range(bsz):
        buf_ref[b, SUBLANES:SUBLANES + BLOCK, 0:GROUP_W] = q_ref[b].astype(F32)
        buf_ref[b, SUBLANES:SUBLANES + BLOCK, GROUP_W:] = k_ref[b].astype(F32)
        conv = cw_ref[ML_CONV - 1:ML_CONV, :] * buf_ref[b, SUBLANES:SUBLANES + BLOCK, :]
        for j in range(1, ML_CONV):
            conv = conv + cw_ref[ML_CONV - 1 - j:ML_CONV - j, :] * buf_ref[b, SUBLANES - j:SUBLANES - j + BLOCK, :]
        qk = conv * jax.nn.sigmoid(conv)
        gcol = gcol_ref[b]
        grow = grow_ref[b, 0]

        for h in range(HEADS):
            idx = b * HEADS + h
            cols = slice(h * DH, (h + 1) * DH)
            qh = qk[:, cols].astype(BF16)
            kh = (qk[:, GROUP_W + h * DH:GROUP_W + (h + 1) * DH] * (DH ** -0.5)).astype(BF16)
            v_t = jnp.concatenate([v_ref[b, :, cols].astype(F32).T, ones_row], axis=0)
            r_col = gcol[:, GC_MI + h:GC_MI + h + 1]
            r_row = grow[GC_MI + h:GC_MI + h + 1, :]
            b_row = grow[GC_MF + h:GC_MF + h + 1, :]
            rm_row = grow[GC_RM + h:GC_RM + h + 1, :]
            g = b_row[:, BLOCK - 1:BLOCK]
            r_max = rm_row[:, BLOCK - 1:BLOCK]
            cn_prev = cn_ref[idx]
            m_prev = m_ref[idx][:, 0:1]

            big_m = jnp.maximum(m_prev, rm_row)
            decay_t = jnp.where(causal, jnp.exp(r_col - big_m), 0.0)
            qk_t = lax.dot_general(kh, qh, nt_dims, preferred_element_type=F32)
            p_t = (qk_t * decay_t).astype(BF16)
            inter_w = jnp.exp(m_prev - big_m)
            tot = (jnp.dot(v_t.astype(BF16), p_t, preferred_element_type=F32)
                   + inter_w * lax.dot_general(cn_prev.astype(BF16), qh, nt_dims, preferred_element_type=F32))
            num = tot[0:DH, :]
            inv = 1.0 / jnp.maximum(jnp.abs(tot[DH:DH + 1, :]), jnp.exp(-(b_row + big_m)))
            scale = inv * lax.rsqrt(inv * inv * jnp.mean(num * num, axis=0, keepdims=True) + EPS)
            hn = (num * scale).T * ng_ref[:, cols]
            y_ref[b, :, cols] = (hn * jax.nn.sigmoid(o_ref[b, :, cols].astype(F32))).astype(BF16)

            vw_t = (v_t * jnp.exp(r_row - r_max)).astype(BF16)
            cn_loc = jnp.dot(vw_t, kh, preferred_element_type=F32)
            m_keep = jnp.maximum(m_prev, r_max)
            cn_ref[idx] = jnp.exp(m_prev - m_keep) * cn_prev + jnp.exp(r_max - m_keep) * cn_loc
            m_ref[idx] = jnp.broadcast_to(g + m_keep, (1, LANES))


def _mlstm(proj3, gcol, grow, conv_w, norm_g):
    bsz, lp, _ = proj3.shape
    blk = lambda col: pl.BlockSpec((bsz, BLOCK, GROUP_W), lambda c: (0, c, col))
    return pl.pallas_call(
        _mlstm_kernel,
        grid=(lp // BLOCK,),
        in_specs=[
            blk(COL_MQ), blk(COL_MK), blk(COL_MV), blk(COL_MO),
            pl.BlockSpec((bsz, BLOCK, LANES), lambda c: (0, c, 0)),
            pl.BlockSpec((bsz, 1, 2 * SUBLANES, BLOCK), lambda c: (0, c, 0, 0)),
            pl.BlockSpec((ML_CONV, 2 * GROUP_W), lambda c: (0, 0)),
            pl.BlockSpec((1, GROUP_W), lambda c: (0, 0)),
        ],
        out_specs=pl.BlockSpec((bsz, BLOCK, GROUP_W), lambda c: (0, c, 0)),
        out_shape=jax.ShapeDtypeStruct((bsz, lp, GROUP_W), BF16),
        scratch_shapes=[
            pltpu.VMEM((bsz, SUBLANES + BLOCK, 2 * GROUP_W), F32),
            pltpu.VMEM((bsz * HEADS, 2 * DH, DH), F32),
            pltpu.VMEM((bsz * HEADS, 1, LANES), F32),
        ],
        compiler_params=_params(("arbitrary",)),
        name="mlstm",
    )(proj3, proj3, proj3, proj3, gcol, grow, conv_w, norm_g)


FOX_ROWS = 32
FOX_BIG = 1e30


def _split3(c):
    hi = c.astype(BF16).astype(F32)
    r = c - hi
    mid = r.astype(BF16).astype(F32)
    return hi, mid, r - mid


def _fox_kernel(q_ref, k_ref, v_ref, gcol_ref, y_ref, kx_ref, vx_ref, qx_ref, s_ref, p_ref, acc_ref,
                m_ref, al_ref, *, tq, lp):
    qi = pl.program_id(1)
    nq = lp // tq
    lane = lax.broadcasted_iota(jnp.int32, (tq, LANES), 1)
    nt_dims = (((1,), (1,)), ((), ()))

    @pl.when(qi == 0)
    def _():
        def build(cb, carry):
            r0 = pl.multiple_of(cb * tq, tq)
            rows = pl.ds(r0, tq)
            is_token = (r0 + lax.broadcasted_iota(jnp.int32, (tq, 1), 0)) >= PAD
            g = gcol_ref[rows, :]
            ones_col = jnp.where(lane == 0, 1.0, 0.0).astype(BF16)
            for h in range(HEADS):
                cols = slice(h * DH, (h + 1) * DH)
                hi, mid, lo = _split3(jnp.where(is_token, g[:, GC_FF + h:GC_FF + h + 1], FOX_BIG))
                ext = jnp.where(lane < 3, 1.0, jnp.where(lane == 3, -hi, jnp.where(lane == 4, -mid,
                                jnp.where(lane == 5, -lo, 0.0))))
                kx_ref[h, rows, 0:DH] = k_ref[rows, cols]
                kx_ref[h, rows, DH:] = ext.astype(BF16)
                vx_ref[h, rows, 0:DH] = v_ref[rows, cols]
                vx_ref[h, rows, DH:] = ones_col
            return carry

        lax.fori_loop(0, nq, build, 0)

    gq = gcol_ref[pl.ds(pl.multiple_of(qi * tq, tq), tq), :]
    row_io = lax.broadcasted_iota(jnp.int32, (FOX_ROWS, tq), 0)
    col_io = lax.broadcasted_iota(jnp.int32, (FOX_ROWS, tq), 1)

    for h in range(HEADS):
        hi, mid, lo = _split3(gq[:, GC_FF + h:GC_FF + h + 1])
        ext = jnp.where(lane == 0, hi, jnp.where(lane == 1, mid, jnp.where(lane == 2, lo,
                        jnp.where(lane < 6, 1.0, 0.0))))
        qx_ref[h, :, 0:DH] = q_ref[:, h * DH:(h + 1) * DH]
        qx_ref[h, :, DH:] = ext.astype(BF16)
    m_ref[...] = jnp.full(m_ref.shape, NEG, F32)
    acc_ref[...] = jnp.zeros_like(acc_ref)

    def scores(kb, h):
        krows = pl.ds(pl.multiple_of(kb * tq, tq), tq)
        s_ref[h] = lax.dot_general(qx_ref[h], kx_ref[h, krows, :], nt_dims, preferred_element_type=F32)

    def step(kb, diagonal):
        krows = pl.ds(pl.multiple_of(kb * tq, tq), tq)
        for h in range(HEADS):
            for r0 in range(0, tq, FOX_ROWS):
                rr = slice(r0, r0 + FOX_ROWS)
                sc = s_ref[h, rr, :]
                if diagonal:
                    sc = jnp.where(col_io <= row_io + r0, sc, NEG)
                m_old = m_ref[h, rr, :]
                m_new = jnp.maximum(m_old, jnp.max(sc, axis=-1, keepdims=True))
                al_ref[h, rr, :] = jnp.exp(m_old - m_new)
                m_ref[h, rr, :] = m_new
                p_ref[h, rr, :] = jnp.exp(sc - m_new).astype(BF16)
            if not diagonal:
                scores(kb + 1, h)
            pv = jnp.dot(p_ref[h], vx_ref[h, krows, :], preferred_element_type=F32)
            acc_ref[h] = al_ref[h] * acc_ref[h] + pv

    def off_diagonal(kb, carry):
        step(kb, False)
        return carry

    for h in range(HEADS):
        scores(0, h)
    lax.fori_loop(0, qi, off_diagonal, 0)
    step(qi, True)
    for h in range(HEADS):
        acc = acc_ref[h]
        y_ref[:, h * DH:(h + 1) * DH] = (acc[:, 0:DH] / acc[:, DH:DH + 1]).astype(BF16)


def _fox(proj3, gcol):
    bsz, lp, _ = proj3.shape
    tq = _pick(lp, 384, BLOCK)
    nq = lp // tq
    return pl.pallas_call(
        functools.partial(_fox_kernel, tq=tq, lp=lp),
        grid=(bsz, nq),
        in_specs=[
            pl.BlockSpec((None, tq, GROUP_W), lambda b, i: (b, i, COL_FQ)),
            pl.BlockSpec((None, lp, GROUP_W), lambda b, i: (b, 0, COL_FK)),
            pl.BlockSpec((None, lp, GROUP_W), lambda b, i: (b, 0, COL_FV)),
            pl.BlockSpec((None, lp, LANES), lambda b, i: (b, 0, 0)),
        ],
        out_specs=pl.BlockSpec((tq, GROUP_W), lambda b, i: (b * nq + i, 0)),
        out_shape=jax.ShapeDtypeStruct((bsz * lp, GROUP_W), BF16),
        scratch_shapes=[
            pltpu.VMEM((HEADS, lp, 2 * DH), BF16),
            pltpu.VMEM((HEADS, lp, 2 * DH), BF16),
            pltpu.VMEM((HEADS, tq, 2 * DH), BF16),
            pltpu.VMEM((HEADS, tq, tq), F32),
            pltpu.VMEM((HEADS, tq, tq), BF16),
            pltpu.VMEM((HEADS, tq, 2 * DH), F32),
            pltpu.VMEM((HEADS, tq, 1), F32),
            pltpu.VMEM((HEADS, tq, 1), F32),
        ],
        compiler_params=_params(("arbitrary", "arbitrary")),
        name="fox",
    )(proj3, proj3, proj3, gcol)


def _pool_rows(u_ref, pw_ref, ps_ref, buf_ref, t, tm):
    @pl.when(t == 0)
    def _():
        buf_ref[0:POOL_MAXW, :] = jnp.zeros((POOL_MAXW, GROUP_W), F32)

    @pl.when(t > 0)
    def _():
        buf_ref[0:POOL_MAXW, :] = buf_ref[tm:tm + POOL_MAXW, :]

    buf_ref[POOL_MAXW:POOL_MAXW + tm, :] = u_ref[...].astype(F32)
    pos = t * tm + lax.broadcasted_iota(jnp.int32, (tm, 1), 0) - (PAD - 1)
    posf = jnp.maximum(pos, 1).astype(F32)
    gw = GROUP_W // len(POOL_WINDOWS)
    out = []
    for gi, w in enumerate(POOL_WINDOWS):
        cols = slice(gi * gw, (gi + 1) * gw)
        x = buf_ref[POOL_MAXW:POOL_MAXW + tm, cols]
        s = x
        for j in range(1, w):
            s = s + buf_ref[POOL_MAXW - j:POOL_MAXW - j + tm, cols]
        pooled = s / jnp.minimum(posf, float(w)) - x
        mixed = jnp.dot(pooled.astype(BF16), pw_ref[gi], preferred_element_type=F32)
        out.append((mixed * ps_ref[:, cols]).astype(BF16))
    return jnp.concatenate(out, axis=1)


def _outproj_kernel(ys_ref, ym_ref, yf_ref, pu_ref, pw_ref, ps_ref, w_ref, g_ref, h_ref, o_ref, buf_ref,
                    *, tm, blocks_per_seq):
    t = pl.program_id(0) % blocks_per_seq
    y_pool = _pool_rows(pu_ref, pw_ref, ps_ref, buf_ref, t, tm)
    mix = jnp.dot(ys_ref[...], w_ref[0:GROUP_W, :], preferred_element_type=F32)
    for n, y_ref in enumerate((ym_ref, yf_ref), start=1):
        mix = mix + jnp.dot(y_ref[...], w_ref[n * GROUP_W:(n + 1) * GROUP_W, :], preferred_element_type=F32)
    mix = mix + jnp.dot(y_pool, w_ref[3 * GROUP_W:4 * GROUP_W, :], preferred_element_type=F32)
    valid = (t * tm + lax.broadcasted_iota(jnp.int32, (tm, 1), 0)) >= PAD
    o_ref[...] = jnp.where(valid, h_ref[...] + _rms(mix, g_ref[...]), 0.0)


def _outproj(ys, ym, yf, proj, pool_w, pool_scale, w_out_all, g, h2, lp, l):
    m = h2.shape[0]
    tm = _pick(lp, 704, 16)
    gw = GROUP_W // len(POOL_WINDOWS)
    yspec = pl.BlockSpec((tm, GROUP_W), lambda i: (i, 0))
    return pl.pallas_call(
        functools.partial(_outproj_kernel, tm=tm, blocks_per_seq=lp // tm),
        grid=(m // tm,),
        in_specs=[
            yspec, yspec, yspec,
            pl.BlockSpec((tm, GROUP_W), lambda i: (i, COL_PU)),
            pl.BlockSpec((len(POOL_WINDOWS), gw, gw), lambda i: (0, 0, 0)),
            pl.BlockSpec((1, GROUP_W), lambda i: (0, 0)),
            pl.BlockSpec((None, D_MODEL, D_MODEL), lambda i: (l, 0, 0), pipeline_mode=pl.Buffered(1)),
            pl.BlockSpec((1, D_MODEL), lambda i: (0, 0)),
            pl.BlockSpec((tm, D_MODEL), lambda i: (i, 0)),
        ],
        out_specs=pl.BlockSpec((tm, D_MODEL), lambda i: (i, 0)),
        out_shape=jax.ShapeDtypeStruct((m, D_MODEL), F32),
        scratch_shapes=[pltpu.VMEM((POOL_MAXW + tm, GROUP_W), F32)],
        input_output_aliases={8: 0},
        compiler_params=_params(("arbitrary",)),
        name="outproj",
    )(ys, ym, yf, proj, pool_w, pool_scale, w_out_all, g, h2)


def _ffn_kernel(h_ref, g1_ref, w1_ref, w2_ref, g2_ref, o_ref, hn_ref, acc_ref):
    f = pl.program_id(1)
    last = pl.num_programs(1) - 1
    tm = h_ref.shape[0]
    halves = (slice(0, tm // 2), slice(tm // 2, tm))

    def update(rows):
        a = jnp.maximum(jnp.dot(hn_ref[rows, :], w1_ref[...], preferred_element_type=F32), 0.0)
        return jnp.dot((a * a).astype(BF16), w2_ref[...], preferred_element_type=F32)

    @pl.when(f == 0)
    def _():
        for rows in halves:
            hn_ref[rows, :] = _rms(h_ref[rows, :], g1_ref[...]).astype(BF16)
            acc_ref[rows, :] = update(rows)

    @pl.when((f > 0) & (f < last))
    def _():
        acc_ref[...] += update(slice(0, tm))

    @pl.when(f == last)
    def _():
        for rows in halves:
            o_ref[rows, :] = h_ref[rows, :] + _rms(acc_ref[rows, :] + update(rows), g2_ref[...])


FFN_TF = 1024


def _ffn_specs(l, tm, h_spec):
    return [
        h_spec,
        pl.BlockSpec((1, D_MODEL), lambda i, f: (0, 0)),
        pl.BlockSpec((None, D_MODEL, FFN_TF), lambda i, f: (l, 0, f)),
        pl.BlockSpec((None, FFN_TF, D_MODEL), lambda i, f: (l, f, 0)),
        pl.BlockSpec((1, D_MODEL), lambda i, f: (0, 0)),
    ], [pltpu.VMEM((tm, D_MODEL), BF16), pltpu.VMEM((tm, D_MODEL), F32)]


def _ffn(h2, g1, w1_all, w2_all, g2, l):
    m = h2.shape[0]
    tm = _pick(m, 768, 16)
    in_specs, scratch = _ffn_specs(l, tm, pl.BlockSpec((tm, D_MODEL), lambda i, f: (i, 0)))
    return pl.pallas_call(
        _ffn_kernel,
        grid=(m // tm, D_FF // FFN_TF),
        in_specs=in_specs,
        out_specs=pl.BlockSpec((tm, D_MODEL), lambda i, f: (i, 0)),
        out_shape=jax.ShapeDtypeStruct((m, D_MODEL), F32),
        scratch_shapes=scratch,
        input_output_aliases={0: 0},
        compiler_params=_params(("arbitrary", "arbitrary")),
        name="ffn",
    )(h2, g1, w1_all, w2_all, g2)


def _ffn_final(h2, g1, w1_all, w2_all, g2, l, bsz, lp):
    seq = lp - PAD - N_META
    tm = _pick(seq, 512, 16)
    per_seq = seq // tm
    h_spec = pl.BlockSpec((pl.Element(tm), pl.Element(D_MODEL)),
                          lambda i, f: (pl.multiple_of((i // per_seq) * lp + PAD + N_META + (i % per_seq) * tm,
                                                       SUBLANES), 0))
    in_specs, scratch = _ffn_specs(l, tm, h_spec)
    return pl.pallas_call(
        _ffn_kernel,
        grid=(bsz * per_seq, D_FF // FFN_TF),
        in_specs=in_specs,
        out_specs=pl.BlockSpec((tm, D_MODEL), lambda i, f: (i, 0)),
        out_shape=jax.ShapeDtypeStruct((bsz * seq, D_MODEL), F32),
        scratch_shapes=scratch,
        compiler_params=_params(("arbitrary", "arbitrary")),
        name="ffn_final",
    )(h2, g1, w1_all, w2_all, g2)


IN_COLS = N_MAIN + 3 * HEADS
IN_COLS_PAD = -(-IN_COLS // LANES) * LANES
O_MI = 5 * GROUP_W
O_FQ = O_MI + 2 * HEADS
O_FF = O_FQ + 3 * GROUP_W
O_PU = O_FF + HEADS


SPLIT_BLOCKS = N_MAIN // LANES
assert O_MI % LANES == 0 and (O_FQ - O_MI) + 3 * GROUP_W == (O_FF - O_MI) and O_FF % LANES == GC_FF


def _split_w_in_kernel(a_ref, b_ref, main_ref, gate_ref):
    d = pl.program_id(0)
    depth = a_ref.shape[1]
    col = lax.broadcasted_iota(jnp.int32, (LANES, D_MODEL), 0)

    def emit(shift):
        for l in range(depth):
            a = a_ref[:, l, :]
            if shift:
                a = pltpu.roll(jnp.where(col >= shift, a, b_ref[:, l, :]), LANES - shift, axis=0)
            main_ref[l] = a.T.astype(BF16)

    wide_before_ml_gates = O_MI // LANES
    wide_before_fx_gate = (O_FF - (O_FQ - O_MI)) // LANES

    @pl.when(d < wide_before_ml_gates)
    def _():
        emit(0)

    @pl.when((d >= wide_before_ml_gates) & (d < wide_before_fx_gate))
    def _():
        emit(O_FQ - O_MI)

    @pl.when(d >= wide_before_fx_gate)
    def _():
        emit(O_PU - O_MI - 3 * GROUP_W)

    @pl.when(d == 0)
    def _():
        gate_ref[...] = jnp.zeros_like(gate_ref)

    @pl.when(d == wide_before_ml_gates)
    def _():
        for l in range(depth):
            gate_ref[l] += jnp.where(col < GC_FF, a_ref[:, l, :], 0.0).T

    @pl.when(d == O_FF // LANES)
    def _():
        for l in range(depth):
            g_fx = jnp.where(col >= GC_FF, jnp.where(col < GC_FF + HEADS, a_ref[:, l, :], 0.0), 0.0)
            gate_ref[l] += g_fx.T


def _split_w_in(w_all):
    depth = w_all.shape[0]
    w_t = jnp.transpose(w_all, (2, 0, 1))
    blk = lambda off: pl.BlockSpec((LANES, depth, D_MODEL), lambda d: (d + off, 0, 0))
    return pl.pallas_call(
        _split_w_in_kernel,
        grid=(SPLIT_BLOCKS,),
        in_specs=[blk(0), blk(1)],
        out_specs=[pl.BlockSpec((depth, D_MODEL, LANES), lambda d: (0, 0, d)),
                   pl.BlockSpec((depth, D_MODEL, LANES), lambda d: (0, 0, 0))],
        out_shape=[jax.ShapeDtypeStruct((depth, D_MODEL, N_MAIN), BF16),
                   jax.ShapeDtypeStruct((depth, D_MODEL, LANES), F32)],
        compiler_params=_params(("arbitrary",)),
        name="split_w_in",
    )(w_t, w_t)


def _s5_matrices(bbre, bbim, c_re, c_im):
    gh = SSM_G // 2
    mask = (jnp.arange(GROUP_W // 2)[:, None] // SSM_H == jnp.arange(S5_HALF)[None, :] // SSM_P).astype(F32)
    c_t = lambda c: c.transpose(0, 2, 1).reshape(SSM_N, SSM_H)
    bmats, cmats = [], []
    for k in range(2):
        st = slice(S5_HALF * k, S5_HALF * (k + 1))
        expand = lambda bb: jnp.tile(bb[:, st], (gh, 1)) * mask
        cont = lambda c: jnp.tile(c_t(c)[st, :], (1, gh)) * mask.T
        bmats.append(jnp.concatenate([expand(bbre), expand(bbim)], axis=1))
        cmats.append(jnp.concatenate([cont(c_re), -cont(c_im)], axis=0))
    return jnp.stack(bmats).astype(BF16), jnp.stack(cmats).astype(BF16)


def kernel(x, meta_tokens, g_pre_mix, g_post_mix, g_pre_ffn, g_post_ffn, w_in, ml_gate_bias, fx_gate_bias, ssm_lam_re, ssm_lam_im, ssm_log_dt, ssm_b_re, ssm_b_im, ssm_c_re, ssm_c_im, ssm_d, ssm_glu_w, ssm_glu_b, ml_conv_w, ml_norm_g, pool_w, pool_scale, w_out, mlp_w1, mlp_w2):
    bsz, seq, d = x.shape
    depth = w_in.shape[0]
    lp = PAD + N_META + seq
    m = bsz * lp
    assert d == D_MODEL and lp % BLOCK == 0
    row = lambda a: a.reshape(1, -1).astype(F32)
    colscale = jnp.ones((1, N_MAIN), F32).at[:, COL_FQ * GROUP_W:(COL_FQ + 1) * GROUP_W].set(DH ** -0.5)
    w_out16, w1_16, w2_16 = w_out.astype(BF16), mlp_w1.astype(BF16), mlp_w2.astype(BF16)

    w_main, w_gate = _split_w_in(w_in)
    w_gate = w_gate.astype(BF16)

    for l in range(depth):
        if l == 0:
            proj, graw, h = _inproj_first(x.reshape(bsz * seq, d).astype(F32), meta_tokens.astype(F32),
                                          row(g_pre_mix[l]), w_main, w_gate, colscale, bsz, lp)
        else:
            proj, graw = _inproj(h, row(g_pre_mix[l]), w_main, w_gate, colscale, l)
        proj3 = proj.reshape(bsz, lp, N_MAIN)

        gbias = jnp.pad(jnp.concatenate([ml_gate_bias[l], fx_gate_bias[l]]).astype(F32), (0, LANES - 3 * HEADS))
        gcol, grow = _gates(graw, gbias.reshape(1, LANES), bsz, lp)

        consts, bbre, bbim = _s5_prep(ssm_lam_re[l], ssm_lam_im[l], ssm_log_dt[l], ssm_b_re[l], ssm_b_im[l], bsz)
        bmat, cmat = _s5_matrices(bbre, bbim, ssm_c_re[l], ssm_c_im[l])
        y_ssm = _s5(proj3, bmat, cmat, row(ssm_d[l]), consts, ssm_glu_w[l].astype(BF16), row(ssm_glu_b[l]))
        y_ml = _mlstm(proj3, gcol, grow, ml_conv_w[l].astype(F32), row(ml_norm_g[l]))
        y_fx = _fox(proj3, gcol)

        h = _outproj(y_ssm.reshape(m, GROUP_W), y_ml.reshape(m, GROUP_W), y_fx,
                     proj, pool_w[l].astype(BF16), row(pool_scale[l]),
                     w_out16, row(g_post_mix[l]), h, lp, l)
        if l + 1 < depth:
            h = _ffn(h, row(g_pre_ffn[l]), w1_16, w2_16, row(g_post_ffn[l]), l)
        else:
            h = _ffn_final(h, row(g_pre_ffn[l]), w1_16, w2_16, row(g_post_ffn[l]), l, bsz, lp)

    return h.reshape(bsz, seq, d)
```
